```python
import math, functools
import jax, jax.numpy as jnp
from jax import lax
import numpy as np

D_MODEL = 1024
BATCH = 4
SEQ = 4096
DEPTH = 2
DEC_BATCH = 32
DEC_SEQ = 8
PAST_LEN = 8192
PAGE_SIZE = 128

D_MIX = D_MODEL
HEAD_DIM = 64
SSM_GROUP = 16
D_SSM = D_MIX // 4
SSM_GROUPS = D_SSM // SSM_GROUP
SSM_STATE = 64
D_RET = 3 * D_MIX // 8
RET_HEADS = D_RET // HEAD_DIM
RET_CHUNK = 128
ROPE_BASE = 10000.0
D_ATT = D_MIX - D_SSM - D_RET
ATT_HEADS = D_ATT // HEAD_DIM
IDX_HEADS = 4
IDX_DIM = 64
TOPK_MAX = 256
Q_BLOCK = 128
D_IN = 2 * D_SSM + 4 * D_RET + 4 * D_ATT + IDX_HEADS * IDX_DIM + IDX_DIM + IDX_HEADS
NORM_EPS = 1e-6
NEG_INF = -1e30

kernel_name = "hymba_s5_retnet_dsa_step"

F32 = jnp.float32


def rms_norm(x, g):
    xf = x.astype(F32)
    y = xf * lax.rsqrt(jnp.mean(xf * xf, axis=-1, keepdims=True) + NORM_EPS)
    return (y * g.astype(F32)).astype(x.dtype)


def split_in(z):
    sizes = (D_SSM, D_SSM, D_RET, D_RET, D_RET, D_RET, D_ATT, D_ATT, D_ATT, D_ATT,
             IDX_HEADS * IDX_DIM, IDX_DIM, IDX_HEADS)
    points = np.cumsum(sizes)[:-1].tolist()
    return jnp.split(z, points, axis=-1)


def rope(x, pos):
    half = x.shape[-1] // 2
    inv = ROPE_BASE ** (-jnp.arange(half, dtype=F32) / half)
    ang = pos.astype(F32)[:, None] * inv[None, :]
    cos = jnp.cos(ang)[None, :, None, :]
    sin = jnp.sin(ang)[None, :, None, :]
    x1, x2 = x[..., :half], x[..., half:]
    return jnp.concatenate([x1 * cos - x2 * sin, x1 * sin + x2 * cos], axis=-1)


def cplx_combine(e1, e2):
    a1r, a1i, b1r, b1i = e1
    a2r, a2i, b2r, b2i = e2
    ar = a1r * a2r - a1i * a2i
    ai = a1r * a2i + a1i * a2r
    br = a2r * b1r - a2i * b1i + b2r
    bi = a2r * b1i + a2i * b1r + b2i
    return ar, ai, br, bi


def ssm_mixer(u, gate, h0_re, h0_im, lam_re, lam_im, b_re, b_im, c_re, c_im, d_skip, log_step, w_glu, b_glu):
    bsz, t, _ = u.shape
    uf = u.astype(F32).reshape(bsz, t, SSM_GROUPS, SSM_GROUP)
    lr, li = lam_re.astype(F32), lam_im.astype(F32)
    step = jnp.exp(log_step.astype(F32))[:, None]
    mag = jnp.exp(lr * step)
    ab_re, ab_im = mag * jnp.cos(li * step), mag * jnp.sin(li * step)
    den = lr * lr + li * li
    nr, ni = ab_re - 1.0, ab_im
    f_re = (nr * lr + ni * li) / den
    f_im = (ni * lr - nr * li) / den
    br, bi = b_re.astype(F32), b_im.astype(F32)
    bb_re = f_re[..., None] * br - f_im[..., None] * bi
    bb_im = f_re[..., None] * bi + f_im[..., None] * br
    bu_re = jnp.einsum('btgc,gpc->btgp', uf, bb_re)
    bu_im = jnp.einsum('btgc,gpc->btgp', uf, bb_im)
    h0r, h0i = h0_re.astype(F32), h0_im.astype(F32)
    bu_re = bu_re.at[:, 0].add(ab_re * h0r - ab_im * h0i)
    bu_im = bu_im.at[:, 0].add(ab_re * h0i + ab_im * h0r)
    a_re = jnp.broadcast_to(ab_re, bu_re.shape)
    a_im = jnp.broadcast_to(ab_im, bu_im.shape)
    _, _, h_re, h_im = lax.associative_scan(cplx_combine, (a_re, a_im, bu_re, bu_im), axis=1)
    y = (jnp.einsum('btgp,gcp->btgc', h_re, c_re.astype(F32))
         - jnp.einsum('btgp,gcp->btgc', h_im, c_im.astype(F32))
         + d_skip.astype(F32) * uf).reshape(bsz, t, D_SSM)
    g = jax.nn.gelu(y)
    y = g * jax.nn.sigmoid(g @ w_glu.astype(F32) + b_glu.astype(F32))
    out = y * jax.nn.silu(gate.astype(F32))
    return out.astype(u.dtype), h_re[:, -1], h_im[:, -1]


def retention_mixer(q, k, v, gate, s0, norm_g, pos):
    bsz, t, _ = q.shape
    qh = rope(q.astype(F32).reshape(bsz, t, RET_HEADS, HEAD_DIM), pos)
    kh = rope(k.astype(F32).reshape(bsz, t, RET_HEADS, HEAD_DIM), pos) * HEAD_DIM ** -0.5
    vh = v.astype(F32).reshape(bsz, t, RET_HEADS, HEAD_DIM)
    chunk = RET_CHUNK if t % RET_CHUNK == 0 else t
    nc = t // chunk
    log_g = jnp.log1p(-jnp.exp2(-5.0 - jnp.arange(RET_HEADS, dtype=F32)))
    j = jnp.arange(chunk, dtype=F32)
    rel = j[:, None] - j[None, :]
    decay_in = jnp.where(rel[None] >= 0, jnp.exp(log_g[:, None, None] * jnp.maximum(rel, 0.0)[None]), 0.0)
    decay_q = jnp.exp(log_g[None, :] * (j[:, None] + 1.0))
    decay_k = jnp.exp(log_g[None, :] * (chunk - 1.0 - j[:, None]))
    decay_s = jnp.exp(log_g * chunk)

    def to_chunks(a):
        return a.reshape(bsz, nc, chunk, RET_HEADS, HEAD_DIM).transpose(1, 0, 2, 3, 4)

    def step(s, blk):
        qb, kb, vb = blk
        att = jnp.einsum('bjhd,blhd->bhjl', qb, kb) * decay_in[None]
        inner = jnp.einsum('bhjl,blhe->bjhe', att, vb)
        cross = jnp.einsum('bjhd,bhde->bjhe', qb, s) * decay_q[None, :, :, None]
        s_new = s * decay_s[None, :, None, None] + jnp.einsum('bjhd,bjhe->bhde', kb * decay_k[None, :, :, None], vb)
        return s_new, inner + cross

    s_t, o = lax.scan(step, s0.astype(F32), (to_chunks(qh), to_chunks(kh), to_chunks(vh)))
    o = o.transpose(1, 0, 2, 3, 4).reshape(bsz, t, RET_HEADS, HEAD_DIM)
    mu = jnp.mean(o, axis=-1, keepdims=True)
    var = jnp.mean(jnp.square(o - mu), axis=-1, keepdims=True)
    o = ((o - mu) * lax.rsqrt(var + NORM_EPS)).reshape(bsz, t, D_RET) * norm_g.astype(F32)
    out = o * jax.nn.silu(gate.astype(F32))
    return out.astype(q.dtype), s_t


def indexer_scores(qi, w, ki):
    s = jnp.einsum('bthd,bsd->bths', qi.astype(F32), ki.astype(F32)) * IDX_DIM ** -0.5
    return jnp.einsum('bths,bth->bts', jax.nn.relu(s), w.astype(F32)) * IDX_HEADS ** -0.5


def gather_rows(rows, idx):
    return jax.vmap(lambda r, i: r[i])(rows, idx)


def sparse_attend(q, kg, vg, valid):
    logits = jnp.einsum('bthd,btjhd->bthj', q.astype(F32), kg.astype(F32)) * HEAD_DIM ** -0.5
    logits = jnp.where(valid[:, :, None, :], logits, NEG_INF)
    p = jax.nn.softmax(logits, axis=-1)
    return jnp.einsum('bthj,btjhd->bthd', p, vg.astype(F32)).astype(q.dtype)


def dsa_prompt(q, k, v, qi, ki, w):
    bsz, t = q.shape[0], q.shape[1]
    topk = min(TOPK_MAX, t // 4)
    key_pos = jnp.arange(t)

    def block(n):
        start = n * Q_BLOCK
        qb = lax.dynamic_slice_in_dim(q, start, Q_BLOCK, axis=1)
        qib = lax.dynamic_slice_in_dim(qi, start, Q_BLOCK, axis=1)
        wb = lax.dynamic_slice_in_dim(w, start, Q_BLOCK, axis=1)
        qpos = start + jnp.arange(Q_BLOCK)
        sc = indexer_scores(qib, wb, ki)
        sc = jnp.where((key_pos[None, :] <= qpos[:, None])[None], sc, NEG_INF)
        _, idx = lax.top_k(sc, topk)
        valid = idx <= qpos[None, :, None]
        return sparse_attend(qb, gather_rows(k, idx), gather_rows(v, idx), valid)

    o = lax.map(block, jnp.arange(t // Q_BLOCK))
    return o.transpose(1, 0, 2, 3, 4).reshape(bsz, t, D_ATT)


def dsa_sample(cache_k_l, cache_v_l, cache_ki_l, page_table, q, k, v, qi, ki, w):
    db, t = q.shape[0], q.shape[1]
    past = page_table.shape[1] * PAGE_SIZE
    total = past + t
    topk = min(TOPK_MAX, total // 4)
    ki_past = cache_ki_l[page_table].reshape(db, past, IDX_DIM).astype(ki.dtype)
    ki_all = jnp.concatenate([ki_past, ki], axis=1)
    qpos = past + jnp.arange(t)
    sc = indexer_scores(qi, w, ki_all)
    sc = jnp.where((jnp.arange(total)[None, :] <= qpos[:, None])[None], sc, NEG_INF)
    _, idx = lax.top_k(sc, topk)
    valid = idx <= qpos[None, :, None]
    in_past = (idx < past)[..., None, None]
    pidx = jnp.minimum(idx, past - 1)
    phys = gather_rows(page_table, pidx // PAGE_SIZE)
    off = pidx % PAGE_SIZE
    nidx = jnp.clip(idx - past, 0, t - 1)
    kg = jnp.where(in_past, cache_k_l[phys, off].astype(k.dtype), gather_rows(k, nidx))
    vg = jnp.where(in_past, cache_v_l[phys, off].astype(v.dtype), gather_rows(v, nidx))
    return sparse_attend(q, kg, vg, valid).reshape(db, t, D_ATT)


def mixer_layer(x, pos, h0_re, h0_im, s0, attend, g, w_in_l, w_out_l, ssm_p, ret_g):
    bsz, t, _ = x.shape
    z = rms_norm(x, g) @ w_in_l
    u, u_gate, rq, rk, rv, r_gate, aq, ak, av, a_gate, iq, ik, iw = split_in(z)
    y_ssm, h_re, h_im = ssm_mixer(u, u_gate, h0_re, h0_im, *ssm_p)
    y_ret, s_new = retention_mixer(rq, rk, rv, r_gate, s0, ret_g, pos)
    ak4 = ak.reshape(bsz, t, ATT_HEADS, HEAD_DIM)
    av4 = av.reshape(bsz, t, ATT_HEADS, HEAD_DIM)
    o = attend(aq.reshape(bsz, t, ATT_HEADS, HEAD_DIM), ak4, av4,
               iq.reshape(bsz, t, IDX_HEADS, IDX_DIM), ik, iw)
    y_att = (o.astype(F32) * jax.nn.silu(a_gate.astype(F32))).astype(x.dtype)
    y = jnp.concatenate([y_ssm, y_ret, y_att], axis=-1) @ w_out_l
    return x + y, (ak4, av4, ik, h_re, h_im, s_new)


def setup_inputs(seed: int = 0) -> dict:
    key = jax.random.key(seed)
    ks = jax.random.split(key, 24)
    n_pages = PAST_LEN // PAGE_SIZE
    n_used = DEC_BATCH * n_pages
    n_phys = n_used + (n_used + 3) // 4
    nrm = jax.random.normal
    page_table = jax.random.permutation(ks[0], n_phys)[:n_used].reshape(DEC_BATCH, n_pages).astype(jnp.int32)
    lam_n = jnp.arange(SSM_STATE, dtype=F32)
    return {
        'x_prompt': nrm(ks[1], (BATCH, SEQ, D_MODEL), F32),
        'x_sample': nrm(ks[2], (DEC_BATCH, DEC_SEQ, D_MODEL), F32),
        'cache_k': nrm(ks[3], (DEPTH, n_phys, PAGE_SIZE, ATT_HEADS, HEAD_DIM), F32),
        'cache_v': nrm(ks[4], (DEPTH, n_phys, PAGE_SIZE, ATT_HEADS, HEAD_DIM), F32),
        'cache_kidx': nrm(ks[5], (DEPTH, n_phys, PAGE_SIZE, IDX_DIM), F32),
        'state_ssm_re': 0.5 * nrm(ks[6], (DEPTH, DEC_BATCH, SSM_GROUPS, SSM_STATE), F32),
        'state_ssm_im': 0.5 * nrm(ks[7], (DEPTH, DEC_BATCH, SSM_GROUPS, SSM_STATE), F32),
        'state_ret': 0.3 * nrm(ks[8], (DEPTH, DEC_BATCH, RET_HEADS, HEAD_DIM, HEAD_DIM), F32),
        'page_table': page_table,
        'norm_g': 1.0 + 0.02 * nrm(ks[9], (DEPTH, D_MODEL), F32),
        'w_in': nrm(ks[10], (DEPTH, D_MODEL, D_IN), F32) * D_MODEL ** -0.5,
        'w_out': nrm(ks[11], (DEPTH, D_MIX, D_MODEL), F32) * (0.5 * D_MIX ** -0.5),
        'ssm_lambda_re': -0.5 + 0.01 * nrm(ks[12], (DEPTH, SSM_GROUPS, SSM_STATE), F32),
        'ssm_lambda_im': math.pi * lam_n + 0.01 * nrm(ks[13], (DEPTH, SSM_GROUPS, SSM_STATE), F32),
        'ssm_b_re': nrm(ks[14], (DEPTH, SSM_GROUPS, SSM_STATE, SSM_GROUP), F32) * (2 * SSM_GROUP) ** -0.5,
        'ssm_b_im': nrm(ks[15], (DEPTH, SSM_GROUPS, SSM_STATE, SSM_GROUP), F32) * (2 * SSM_GROUP) ** -0.5,
        'ssm_c_re': nrm(ks[16], (DEPTH, SSM_GROUPS, SSM_GROUP, SSM_STATE), F32) * SSM_STATE ** -0.5,
        'ssm_c_im': nrm(ks[17], (DEPTH, SSM_GROUPS, SSM_GROUP, SSM_STATE), F32) * SSM_STATE ** -0.5,
        'ssm_d': nrm(ks[18], (DEPTH, SSM_GROUPS, SSM_GROUP), F32),
        'ssm_log_step': jax.random.uniform(ks[19], (DEPTH, SSM_GROUPS), F32, math.log(1e-3), math.log(1e-1)),
        'ssm_w_glu': nrm(ks[20], (DEPTH, D_SSM, D_SSM), F32) * D_SSM ** -0.5,
        'ssm_b_glu': 0.02 * nrm(ks[21], (DEPTH, D_SSM), F32),
        'ret_norm_g': 1.0 + 0.02 * nrm(ks[22], (DEPTH, D_RET), F32),
        'final_norm_g': 1.0 + 0.02 * nrm(ks[23], (D_MODEL,), F32),
    }


def reference(x_prompt, x_sample, cache_k, cache_v, cache_kidx, state_ssm_re, state_ssm_im, state_ret,
              page_table, norm_g, w_in, w_out, ssm_lambda_re, ssm_lambda_im, ssm_b_re, ssm_b_im,
              ssm_c_re, ssm_c_im, ssm_d, ssm_log_step, ssm_w_glu, ssm_b_glu, ret_norm_g, final_norm_g):
    bsz, t_p = x_prompt.shape[0], x_prompt.shape[1]
    t_s = x_sample.shape[1]
    past = page_table.shape[1] * PAGE_SIZE
    pos_p = jnp.arange(t_p)
    pos_s = past + jnp.arange(t_s)
    zero_h = jnp.zeros((bsz, SSM_GROUPS, SSM_STATE), F32)
    zero_s = jnp.zeros((bsz, RET_HEADS, HEAD_DIM, HEAD_DIM), F32)
    hp, hs = x_prompt, x_sample
    outs_p, outs_s = [], []
    for l in range(DEPTH):
        ssm_p = (ssm_lambda_re[l], ssm_lambda_im[l], ssm_b_re[l], ssm_b_im[l], ssm_c_re[l], ssm_c_im[l],
                 ssm_d[l], ssm_log_step[l], ssm_w_glu[l], ssm_b_glu[l])
        hp, st_p = mixer_layer(hp, pos_p, zero_h, zero_h, zero_s, dsa_prompt,
                               norm_g[l], w_in[l], w_out[l], ssm_p, ret_norm_g[l])
        attend_s = functools.partial(dsa_sample, cache_k[l], cache_v[l], cache_kidx[l], page_table)
        hs, st_s = mixer_layer(hs, pos_s, state_ssm_re[l], state_ssm_im[l], state_ret[l], attend_s,
                               norm_g[l], w_in[l], w_out[l], ssm_p, ret_norm_g[l])
        outs_p.append(st_p)
        outs_s.append(st_s)
    y_prompt = rms_norm(hp, final_norm_g)
    y_sample = rms_norm(hs, final_norm_g)
    new_k_p = jnp.stack([o[0] for o in outs_p])
    new_v_p = jnp.stack([o[1] for o in outs_p])
    new_kidx_p = jnp.stack([o[2] for o in outs_p])
    new_ssm_re_p = jnp.stack([o[3] for o in outs_p])
    new_ssm_im_p = jnp.stack([o[4] for o in outs_p])
    new_ret_p = jnp.stack([o[5] for o in outs_p])
    new_k_s = jnp.stack([o[0] for o in outs_s])
    new_v_s = jnp.stack([o[1] for o in outs_s])
    new_kidx_s = jnp.stack([o[2] for o in outs_s])
    new_ssm_re_s = jnp.stack([o[3] for o in outs_s])
    new_ssm_im_s = jnp.stack([o[4] for o in outs_s])
    new_ret_s = jnp.stack([o[5] for o in outs_s])
    return (y_prompt, y_sample, new_k_p, new_v_p, new_kidx_p, new_ssm_re_p, new_ssm_im_p, new_ret_p,
            new_k_s, new_v_s, new_kidx_s, new_ssm_re_s, new_ssm_im_s, new_ret_s)
```

```python
import functools
import math

import jax
import jax.numpy as jnp
import numpy as np
from jax import lax
from jax.experimental import pallas as pl
from jax.experimental.pallas import tpu as pltpu

F32 = jnp.float32
BF16 = jnp.bfloat16
I32 = jnp.int32

D_MODEL = 1024
HEAD_DIM = 64
SSM_GROUP = 16
D_SSM = 256
SSM_GROUPS = D_SSM // SSM_GROUP
SSM_STATE = 64
N_STATE = SSM_GROUPS * SSM_STATE
D_RET = 384
RET_HEADS = D_RET // HEAD_DIM
RET_CHUNK = 128
ROPE_BASE = 10000.0
D_ATT = 384
ATT_HEADS = D_ATT // HEAD_DIM
IDX_HEADS = 4
IDX_DIM = 64
TOPK_MAX = 256
Q_BLOCK = 128
PAGE_SIZE = 128
NORM_EPS = 1e-6
NEG_INF = -1e30

_SIZES = (D_SSM, D_SSM, D_RET, D_RET, D_RET, D_RET, D_ATT, D_ATT, D_ATT, D_ATT,
          IDX_HEADS * IDX_DIM, IDX_DIM, IDX_HEADS)
_OFF = np.concatenate([[0], np.cumsum(_SIZES)]).tolist()
C_U, C_RET, C_AQ, C_AK, C_AV, C_AG, C_IQ, C_IK, C_IW, C_END = (
    _OFF[0], _OFF[2], _OFF[6], _OFF[7], _OFF[8], _OFF[9], _OFF[10], _OFF[11], _OFF[12], _OFF[13])

LANES = 128
SUBLANES = 8
F32_INF_BITS = 0x7F800000
VMEM_LIMIT = 56 * 1024 * 1024

_NT = (((1,), (1,)), ((), ()))


def _cparams(n_axes):
    return pltpu.CompilerParams(dimension_semantics=("arbitrary",) * n_axes,
                                vmem_limit_bytes=VMEM_LIMIT)


def _inproj_kernel(x_ref, g_ref, wm_ref, wvt_ref, wwt_ref,
                   u2_ref, r4_ref, ak_ref, av_ref, ik_ref, ag_ref,
                   aqb_ref, akb_ref, avtb_ref, iqb_ref, ikb_ref, iwt_ref):
    x = x_ref[...]
    ms = jnp.mean(x * x, axis=-1, keepdims=True)
    xn = (x * lax.rsqrt(ms + NORM_EPS) * g_ref[...]).astype(BF16)

    def seg(a, b):
        return jnp.dot(xn, wm_ref[:, a:b], preferred_element_type=F32)

    u2_ref[...] = seg(C_U, C_RET)
    r4_ref[...] = seg(C_RET, C_AQ)
    aqb_ref[...] = (seg(C_AQ, C_AK) * HEAD_DIM ** -0.5).astype(BF16)
    ak = seg(C_AK, C_AV)
    ak_ref[...] = ak
    akb_ref[...] = ak.astype(BF16)
    av_ref[...] = seg(C_AV, C_AG)
    ag_ref[...] = seg(C_AG, C_IQ)
    iqb_ref[...] = seg(C_IQ, C_IK).astype(BF16)
    ik = seg(C_IK, C_IW)
    ik_ref[...] = ik
    ikb_ref[...] = ik.astype(BF16)
    avtb_ref[...] = lax.dot_general(wvt_ref[...], xn, _NT, preferred_element_type=F32).astype(BF16)
    iwt_ref[...] = lax.dot_general(wwt_ref[...], xn, _NT, preferred_element_type=F32) * (
        IDX_HEADS ** -0.5 * IDX_DIM ** -0.5)


def _inproj(x, g, wm, wvt, wwt, tm):
    n = x.shape[0]
    row = lambda w: pl.BlockSpec((tm, w), lambda i: (i, 0))
    col = lambda h: pl.BlockSpec((h, tm), lambda i: (0, i))
    full = lambda a: pl.BlockSpec(a.shape, lambda i: (0,) * a.ndim)
    widths_f32 = (C_RET - C_U, C_AQ - C_RET, D_ATT, D_ATT, IDX_DIM, D_ATT)
    out_shape = tuple(jax.ShapeDtypeStruct((n, w), F32) for w in widths_f32) + (
        jax.ShapeDtypeStruct((n, D_ATT), BF16),
        jax.ShapeDtypeStruct((n, D_ATT), BF16),
        jax.ShapeDtypeStruct((D_ATT, n), BF16),
        jax.ShapeDtypeStruct((n, IDX_HEADS * IDX_DIM), BF16),
        jax.ShapeDtypeStruct((n, IDX_DIM), BF16),
        jax.ShapeDtypeStruct((SUBLANES, n), F32),
    )
    out_specs = tuple(row(w) for w in widths_f32) + (
        row(D_ATT), row(D_ATT), col(D_ATT), row(IDX_HEADS * IDX_DIM), row(IDX_DIM), col(SUBLANES))
    return pl.pallas_call(
        _inproj_kernel,
        grid=(n // tm,),
        in_specs=[row(D_MODEL), full(g), full(wm), full(wvt), full(wwt)],
        out_specs=out_specs,
        out_shape=out_shape,
        compiler_params=_cparams(1),
        name="inproj",
    )(x, g, wm, wvt, wwt)


def _outproj_kernel(x_ref, ys_ref, yr_ref, ya_ref, wo_ref, gf_ref, o_ref, *, final):
    y = x_ref[...]
    y = y + jnp.dot(ys_ref[...], wo_ref[0:D_SSM, :], preferred_element_type=F32)
    y = y + jnp.dot(yr_ref[...], wo_ref[D_SSM:D_SSM + D_RET, :], preferred_element_type=F32)
    y = y + jnp.dot(ya_ref[...], wo_ref[D_SSM + D_RET:, :], preferred_element_type=F32)
    if final:
        ms = jnp.mean(y * y, axis=-1, keepdims=True)
        y = y * lax.rsqrt(ms + NORM_EPS) * gf_ref[...]
    o_ref[...] = y


def _outproj(x, ys, yr, ya, wo, gf, tm, final):
    n = x.shape[0]
    row = lambda w: pl.BlockSpec((tm, w), lambda i: (i, 0))
    full = lambda a: pl.BlockSpec(a.shape, lambda i: (0,) * a.ndim)
    return pl.pallas_call(
        functools.partial(_outproj_kernel, final=final),
        grid=(n // tm,),
        in_specs=[row(D_MODEL), row(D_SSM), row(D_RET), row(D_ATT), full(wo), full(gf)],
        out_specs=row(D_MODEL),
        out_shape=jax.ShapeDtypeStruct((n, D_MODEL), F32),
        compiler_params=_cparams(1),
        name="outproj",
    )(x, ys, yr, ya, wo, gf)


def _cmul(ar, ai, br, bi):
    return ar * br - ai * bi, ar * bi + ai * br


def _ssm_kernel(u2_ref, h0r_ref, h0i_ref, lr_ref, li_ref, ls_ref, btr_ref, bti_ref,
                ctr_ref, cti_ref, d_ref, wg_ref, bg_ref,
                y_ref, hr_out_ref, hi_out_ref,
                bbr_ref, bbi_ref, apr_ref, api_ref, cr_ref, ci_ref, hr_ref, hi_ref, *, tl):
    b = pl.program_id(0)
    c = pl.program_id(1)
    nc = pl.num_programs(1)

    @pl.when((b == 0) & (c == 0))
    def _prep():
        lr, li = lr_ref[...], li_ref[...]
        dt = jnp.exp(ls_ref[...])
        mag = jnp.exp(lr * dt)
        abr, abi = mag * jnp.cos(li * dt), mag * jnp.sin(li * dt)
        den = lr * lr + li * li
        nr, ni = abr - 1.0, abi
        fr = (nr * lr + ni * li) / den
        fi = (ni * lr - nr * li) / den
        bbr_ref[...] = (fr * btr_ref[...] - fi * bti_ref[...]).astype(BF16)
        bbi_ref[...] = (fr * bti_ref[...] + fi * btr_ref[...]).astype(BF16)
        pr, pi = abr, abi
        rows_r, rows_i = [pr], [pi]
        for _ in range(SUBLANES - 1):
            pr, pi = _cmul(pr, pi, abr, abi)
            rows_r.append(pr)
            rows_i.append(pi)
        apr_ref[...] = jnp.concatenate(rows_r, axis=0)
        api_ref[...] = jnp.concatenate(rows_i, axis=0)

    @pl.when(c == 0)
    def _init():
        cr_ref[...] = h0r_ref[...]
        ci_ref[...] = h0i_ref[...]

    u = u2_ref[:, 0:D_SSM]
    gate = u2_ref[:, D_SSM:2 * D_SSM]
    ub = u.astype(BF16)
    hr_ref[...] = jnp.dot(ub, bbr_ref[...], preferred_element_type=F32)
    hi_ref[...] = jnp.dot(ub, bbi_ref[...], preferred_element_type=F32)

    apr, api = apr_ref[...], api_ref[...]
    rowid = lax.broadcasted_iota(I32, (SUBLANES, N_STATE), 0)

    def blk(j, carry):
        cr, ci = carry
        r0 = pl.multiple_of(j * SUBLANES, SUBLANES)
        br = hr_ref[pl.ds(r0, SUBLANES), :]
        bi = hi_ref[pl.ds(r0, SUBLANES), :]
        for k in (1, 2, 4):
            keep = rowid >= k
            sr = jnp.where(keep, pltpu.roll(br, k, 0), 0.0)
            si = jnp.where(keep, pltpu.roll(bi, k, 0), 0.0)
            akr, aki = apr[k - 1:k, :], api[k - 1:k, :]
            br, bi = br + (akr * sr - aki * si), bi + (akr * si + aki * sr)
        hr = br + (apr * cr - api * ci)
        hi = bi + (apr * ci + api * cr)
        hr_ref[pl.ds(r0, SUBLANES), :] = hr
        hi_ref[pl.ds(r0, SUBLANES), :] = hi
        return hr[SUBLANES - 1:SUBLANES, :], hi[SUBLANES - 1:SUBLANES, :]

    cr, ci = lax.fori_loop(0, tl // SUBLANES, blk, (cr_ref[...], ci_ref[...]))
    cr_ref[...] = cr
    ci_ref[...] = ci

    @pl.when(c == nc - 1)
    def _fin():
        hr_out_ref[...] = cr
        hi_out_ref[...] = ci

    y = (jnp.dot(hr_ref[...].astype(BF16), ctr_ref[...], preferred_element_type=F32)
         - jnp.dot(hi_ref[...].astype(BF16), cti_ref[...], preferred_element_type=F32)
         + d_ref[...] * u)
    g = jax.nn.gelu(y)
    y = g * jax.nn.sigmoid(jnp.dot(g.astype(BF16), wg_ref[...], preferred_element_type=F32) + bg_ref[...])
    y_ref[...] = (y * (gate * jax.nn.sigmoid(gate))).astype(BF16)


def _ssm(u2, h0r, h0i, prm, tl):
    bv, tv, _ = u2.shape
    full = lambda a: pl.BlockSpec(a.shape, lambda b, c: (0,) * a.ndim)
    st = pl.BlockSpec((None, 1, N_STATE), lambda b, c: (b, 0, 0))
    return pl.pallas_call(
        functools.partial(_ssm_kernel, tl=tl),
        grid=(bv, tv // tl),
        in_specs=[pl.BlockSpec((None, tl, 2 * D_SSM), lambda b, c: (b, c, 0)), st, st]
        + [full(a) for a in prm],
        out_specs=(pl.BlockSpec((None, tl, D_SSM), lambda b, c: (b, c, 0)), st, st),
        out_shape=(jax.ShapeDtypeStruct((bv, tv, D_SSM), BF16),
                   jax.ShapeDtypeStruct((bv, 1, N_STATE), F32),
                   jax.ShapeDtypeStruct((bv, 1, N_STATE), F32)),
        scratch_shapes=[pltpu.VMEM((D_SSM, N_STATE), BF16), pltpu.VMEM((D_SSM, N_STATE), BF16),
                        pltpu.VMEM((SUBLANES, N_STATE), F32), pltpu.VMEM((SUBLANES, N_STATE), F32),
                        pltpu.VMEM((1, N_STATE), F32), pltpu.VMEM((1, N_STATE), F32),
                        pltpu.VMEM((tl, N_STATE), F32), pltpu.VMEM((tl, N_STATE), F32)],
        compiler_params=_cparams(2),
        name="ssm",
    )(u2, h0r, h0i, *prm)


def _ssm_params(lam_re, lam_im, b_re, b_im, c_re, c_im, d_skip, log_step, w_glu, b_glu):
    eye = jnp.eye(SSM_GROUPS, dtype=F32)

    def bdiag(m):
        g, r, c = m.shape
        return (eye[:, None, :, None] * m[:, :, None, :]).reshape(g * r, g * c)

    flat = lambda a: a.astype(F32).reshape(1, N_STATE)
    ls = jnp.broadcast_to(log_step.astype(F32)[:, None], (SSM_GROUPS, SSM_STATE))
    btr = bdiag(jnp.swapaxes(b_re.astype(F32), 1, 2))
    bti = bdiag(jnp.swapaxes(b_im.astype(F32), 1, 2))
    ctr = bdiag(jnp.swapaxes(c_re.astype(F32), 1, 2)).astype(BF16)
    cti = bdiag(jnp.swapaxes(c_im.astype(F32), 1, 2)).astype(BF16)
    return (flat(lam_re), flat(lam_im), flat(ls), btr, bti, ctr, cti,
            d_skip.astype(F32).reshape(1, D_SSM), w_glu.astype(BF16),
            b_glu.astype(F32).reshape(1, D_SSM))


def _swap_halves(x):
    lane = lax.broadcasted_iota(I32, (x.shape[0], LANES), 1)
    first = (lane % HEAD_DIM) < (HEAD_DIM // 2)
    tiles = []
    for t in range(x.shape[1] // LANES):
        xt = x[:, t * LANES:(t + 1) * LANES]
        up = pltpu.roll(xt, LANES - HEAD_DIM // 2, 1)
        dn = pltpu.roll(xt, HEAD_DIM // 2, 1)
        tiles.append(jnp.where(first, up, dn))
    return jnp.concatenate(tiles, axis=1)


def _ret_kernel(r4_ref, s0_ref, cos_ref, sin_ref, din_ref, dq_ref, dk_ref, ds_ref, g_ref,
                y_ref, s_out_ref, s_ref):
    c = pl.program_id(1)
    nc = pl.num_programs(1)

    @pl.when(c == 0)
    def _init():
        s_ref[...] = s0_ref[...]

    q = r4_ref[:, 0:D_RET]
    k = r4_ref[:, D_RET:2 * D_RET]
    v = r4_ref[:, 2 * D_RET:3 * D_RET]
    gate = r4_ref[:, 3 * D_RET:4 * D_RET]
    cos = jnp.concatenate([cos_ref[...]] * (D_RET // LANES), axis=1)
    sin = jnp.concatenate([sin_ref[...]] * (D_RET // LANES), axis=1)
    qr = q * cos + _swap_halves(q) * sin
    kr = (k * cos + _swap_halves(k) * sin) * HEAD_DIM ** -0.5
    qb = qr.astype(BF16)
    kb = kr.astype(BF16)
    vb = v.astype(BF16)
    kdt = (kr * dk_ref[...]).T.astype(BF16)
    dq = dq_ref[...]
    gsil = gate * jax.nn.sigmoid(gate) * g_ref[...]
    for h in range(RET_HEADS):
        sl = slice(h * HEAD_DIM, (h + 1) * HEAD_DIM)
        qh, kh, vh = qb[:, sl], kb[:, sl], vb[:, sl]
        s = s_ref[h]
        att = lax.dot_general(qh, kh, _NT, preferred_element_type=F32) * din_ref[h]
        inner = jnp.dot(att.astype(BF16), vh, preferred_element_type=F32)
        cross = jnp.dot(qh, s.astype(BF16), preferred_element_type=F32) * dq[:, sl]
        s_ref[h] = s * ds_ref[h] + jnp.dot(kdt[sl, :], vh, preferred_element_type=F32)
        o = inner + cross
        mu = jnp.mean(o, axis=-1, keepdims=True)
        var = jnp.mean(jnp.square(o - mu), axis=-1, keepdims=True)
        y_ref[:, sl] = ((o - mu) * lax.rsqrt(var + NORM_EPS) * gsil[:, sl]).astype(BF16)

    @pl.when(c == nc - 1)
    def _fin():
        s_out_ref[...] = s_ref[...]


def _ret_tables(pos, chunk):
    half = HEAD_DIM // 2
    inv = ROPE_BASE ** (-jnp.arange(half, dtype=F32) / half)
    ang = pos.astype(F32)[:, None] * inv[None, :]
    cos, sin = jnp.cos(ang), jnp.sin(ang)
    cos_t = jnp.concatenate([cos, cos] * (LANES // HEAD_DIM), axis=1)
    sin_t = jnp.concatenate([-sin, sin] * (LANES // HEAD_DIM), axis=1)
    log_g = jnp.log1p(-jnp.exp2(-5.0 - jnp.arange(RET_HEADS, dtype=F32)))
    j = jnp.arange(chunk, dtype=F32)
    rel = j[:, None] - j[None, :]
    din = jnp.where(rel[None] >= 0, jnp.exp(log_g[:, None, None] * jnp.maximum(rel, 0.0)[None]), 0.0)
    dq = jnp.repeat(jnp.exp(log_g[None, :] * (j[:, None] + 1.0)), HEAD_DIM, axis=1)
    dk = jnp.repeat(jnp.exp(log_g[None, :] * (chunk - 1.0 - j[:, None])), HEAD_DIM, axis=1)
    ds = jnp.broadcast_to(jnp.exp(log_g * chunk)[:, None, None], (RET_HEADS, HEAD_DIM, HEAD_DIM))
    return cos_t, sin_t, din, dq, dk, ds


def _ret(r4, s0, pos, norm_g):
    bv, tv, _ = r4.shape
    chunk = RET_CHUNK if tv % RET_CHUNK == 0 else tv
    cos_t, sin_t, din, dq, dk, ds = _ret_tables(pos, chunk)
    g = norm_g.astype(F32).reshape(1, D_RET)
    full = lambda a: pl.BlockSpec(a.shape, lambda b, c: (0,) * a.ndim)
    st = pl.BlockSpec((None, RET_HEADS, HEAD_DIM, HEAD_DIM), lambda b, c: (b, 0, 0, 0))
    tab = pl.BlockSpec((chunk, LANES), lambda b, c: (c, 0))
    return pl.pallas_call(
        _ret_kernel,
        grid=(bv, tv // chunk),
        in_specs=[pl.BlockSpec((None, chunk, 4 * D_RET), lambda b, c: (b, c, 0)), st, tab, tab,
                  full(din), full(dq), full(dk), full(ds), full(g)],
        out_specs=(pl.BlockSpec((None, chunk, D_RET), lambda b, c: (b, c, 0)), st),
        out_shape=(jax.ShapeDtypeStruct((bv, tv, D_RET), BF16),
                   jax.ShapeDtypeStruct((bv, RET_HEADS, HEAD_DIM, HEAD_DIM), F32)),
        scratch_shapes=[pltpu.VMEM((RET_HEADS, HEAD_DIM, HEAD_DIM), F32)],
        compiler_params=_cparams(2),
        name="retention",
    )(r4, s0, cos_t, sin_t, din, dq, dk, ds, g)


def _count(sc_ref, nk, kb, pred):
    def body(c, cnt):
        r0 = pl.multiple_of(c * kb, kb)
        for j in range(kb // SUBLANES):
            blk = sc_ref[pl.ds(r0 + j * SUBLANES, SUBLANES), :]
            cnt = cnt + jnp.where(pred(blk), 1, 0)
        return cnt
    cnt = lax.fori_loop(0, nk, body, jnp.zeros((SUBLANES, LANES), I32))
    return jnp.sum(cnt, axis=0, keepdims=True)


def _select_bias(sc_ref, bias_ref, tri_ref, nk, kb, n_beyond, qpos, topk):
    def count_ge(t):
        return _count(sc_ref, nk, kb, lambda blk: blk >= t) + jnp.where(t <= NEG_INF, n_beyond, 0)

    pos = jnp.where(count_ge(jnp.zeros((1, LANES), F32)) >= topk, 1, 0)
    sign = jnp.where(pos == 1, 0, np.int32(-2 ** 31))

    def bit_step(i, cur):
        cand = cur | lax.shift_left(jnp.int32(1), 30 - i)
        t = lax.bitcast_convert_type(cand | sign, F32)
        ok = jnp.where(count_ge(t) >= topk, 1, 0)
        finite = jnp.where(cand <= F32_INF_BITS, 1, 0)
        keep = jnp.where(pos == 1, ok, (1 - ok) * finite)
        return jnp.where(keep == 1, cand, cur)

    cur = lax.fori_loop(0, 31, bit_step, jnp.zeros((1, LANES), I32))
    mag = jnp.where(pos == 1, cur, cur + 1)
    tau = lax.bitcast_convert_type(mag | sign, F32)
    c_gt = _count(sc_ref, nk, kb, lambda blk: blk > tau) + jnp.where(NEG_INF > tau, n_beyond, 0)
    need = (topk - c_gt).astype(F32)

    def body(c, carry):
        r0 = pl.multiple_of(c * kb, kb)
        blk = sc_ref[pl.ds(r0, kb), :]
        eq = blk == tau
        pref = jnp.dot(tri_ref[...], jnp.where(eq, 1.0, 0.0).astype(BF16),
                       preferred_element_type=F32) + carry
        kidx = r0 + lax.broadcasted_iota(I32, (kb, LANES), 0)
        sel = ((blk > tau) | (eq & (pref <= need))) & (kidx <= qpos)
        bias_ref[pl.ds(r0, kb), :] = jnp.where(sel, 0.0, NEG_INF)
        return pref[kb - 1:kb, :]

    lax.fori_loop(0, nk, body, jnp.zeros((1, LANES), F32))


def _tri(kb):
    r = np.arange(kb)
    return jnp.asarray((r[None, :] <= r[:, None]).astype(np.float32), dtype=BF16)


def _dsa_prompt_kernel(ikb_ref, akb_ref, avtb_ref, iqb_ref, iwt_ref, aqb_ref, ag_ref, tri_ref,
                       y_ref, sc_ref, bias_ref, ot_ref, *, kb, t_total, topk):
    i = pl.program_id(1)
    q0 = i * Q_BLOCK
    nk = (q0 + Q_BLOCK + kb - 1) // kb
    qpos = q0 + lax.broadcasted_iota(I32, (1, LANES), 1)
    iq = iqb_ref[...]
    iw = iwt_ref[...]

    def score_body(c, _):
        r0 = pl.multiple_of(c * kb, kb)
        keys = ikb_ref[pl.ds(r0, kb), :]
        acc = jnp.zeros((kb, LANES), F32)
        for h in range(IDX_HEADS):
            s = lax.dot_general(keys, iq[:, h * IDX_DIM:(h + 1) * IDX_DIM], _NT,
                                preferred_element_type=F32)
            acc = acc + jnp.maximum(s, 0.0) * iw[h:h + 1, :]
        kidx = r0 + lax.broadcasted_iota(I32, (kb, LANES), 0)
        sc_ref[pl.ds(r0, kb), :] = jnp.where(kidx <= qpos, acc, NEG_INF)
        return 0

    lax.fori_loop(0, nk, score_body, 0)
    _select_bias(sc_ref, bias_ref, tri_ref, nk, kb, t_total - nk * kb, qpos, topk)

    aq = aqb_ref[...]
    for h in range(ATT_HEADS):
        sl = slice(h * HEAD_DIM, (h + 1) * HEAD_DIM)
        qh = aq[:, sl]

        def att_body(c, carry):
            m, l, acc = carry
            r0 = pl.multiple_of(c * kb, kb)
            kh = akb_ref[pl.ds(r0, kb), sl]
            logit = lax.dot_general(kh, qh, _NT, preferred_element_type=F32) + bias_ref[pl.ds(r0, kb), :]
            m_new = jnp.maximum(m, jnp.max(logit, axis=0, keepdims=True))
            alpha = jnp.exp(m - m_new)
            p = jnp.exp(logit - m_new)
            l = alpha * l + jnp.sum(p, axis=0, keepdims=True)
            vt = avtb_ref[sl, pl.ds(r0, kb)]
            acc = alpha * acc + jnp.dot(vt, p.astype(BF16), preferred_element_type=F32)
            return m_new, l, acc

        m0 = jnp.full((1, LANES), NEG_INF, F32)
        l0 = jnp.zeros((1, LANES), F32)
        a0 = jnp.zeros((HEAD_DIM, LANES), F32)
        _, l, acc = lax.fori_loop(0, nk, att_body, (m0, l0, a0))
        ot_ref[sl, :] = acc / l
    gate = ag_ref[...]
    y_ref[...] = (ot_ref[...].T * (gate * jax.nn.sigmoid(gate))).astype(BF16)


def _dsa_prompt(bv, tv, ikb, akb, avtb, iqb, iwt, aqb, ag, kb=256):
    topk = min(TOPK_MAX, tv // 4)
    kb = min(kb, tv)
    nq = tv // Q_BLOCK
    tri = _tri(kb)
    per_b = lambda w: pl.BlockSpec((tv, w), lambda b, i: (b, 0))
    qrow = lambda w: pl.BlockSpec((Q_BLOCK, w), lambda b, i: (b * nq + i, 0))
    return pl.pallas_call(
        functools.partial(_dsa_prompt_kernel, kb=kb, t_total=tv, topk=topk),
        grid=(bv, nq),
        in_specs=[per_b(IDX_DIM), per_b(D_ATT),
                  pl.BlockSpec((D_ATT, tv), lambda b, i: (0, b)),
                  qrow(IDX_HEADS * IDX_DIM),
                  pl.BlockSpec((SUBLANES, Q_BLOCK), lambda b, i: (0, b * nq + i)),
                  qrow(D_ATT), qrow(D_ATT),
                  pl.BlockSpec(tri.shape, lambda b, i: (0, 0))],
        out_specs=qrow(D_ATT),
        out_shape=jax.ShapeDtypeStruct((bv * tv, D_ATT), BF16),
        scratch_shapes=[pltpu.VMEM((tv, LANES), F32), pltpu.VMEM((tv, LANES), F32),
                        pltpu.VMEM((D_ATT, LANES), F32)],
        compiler_params=_cparams(2),
        name="dsa_prompt",
    )(ikb, akb, avtb, iqb, iwt, aqb, ag, tri)


def _samp_score_kernel(pt_ref, *refs, npg, ts):
    del pt_ref
    pages = refs[:npg]
    qi_ref, w_ref, iknew_ref, sc_ref, scn_ref = refs[npg:]
    qi = qi_ref[...]
    w = w_ref[...]

    def score(keys):
        s = lax.dot_general(qi, keys.astype(BF16), _NT, preferred_element_type=F32)
        r = jnp.maximum(s, 0.0) * w
        out = r[0:ts]
        for h in range(1, IDX_HEADS):
            out = out + r[h * ts:(h + 1) * ts]
        return out

    for i in range(npg):
        sc_ref[:, i * PAGE_SIZE:(i + 1) * PAGE_SIZE] = score(pages[i][...])

    @pl.when(pl.program_id(1) == 0)
    def _new_keys():
        sn = score(iknew_ref[...])
        col = lax.broadcasted_iota(I32, (ts, PAGE_SIZE), 1)
        row = lax.broadcasted_iota(I32, (ts, PAGE_SIZE), 0)
        scn_ref[...] = jnp.where(col <= row, sn, NEG_INF)


def _samp_scores(page_table, cache_kidx, layer, qi, w, iknew, npg=8):
    bs, n_pages = page_table.shape
    ts = qi.shape[1] // IDX_HEADS
    page_spec = lambda i: pl.BlockSpec(
        (None, None, PAGE_SIZE, IDX_DIM), lambda b, j, pt, i=i: (layer, pt[b, j * npg + i], 0, 0))
    per_b = lambda a: pl.BlockSpec((None,) + a.shape[1:], lambda b, j, pt: (b, 0, 0))
    gs = pltpu.PrefetchScalarGridSpec(
        num_scalar_prefetch=1,
        grid=(bs, n_pages // npg),
        in_specs=[page_spec(i) for i in range(npg)] + [per_b(qi), per_b(w), per_b(iknew)],
        out_specs=(pl.BlockSpec((ts, npg * PAGE_SIZE), lambda b, j, pt: (b, j)),
                   pl.BlockSpec((ts, PAGE_SIZE), lambda b, j, pt: (b, 0))),
    )
    return pl.pallas_call(
        functools.partial(_samp_score_kernel, npg=npg, ts=ts),
        grid_spec=gs,
        out_shape=(jax.ShapeDtypeStruct((bs * ts, n_pages * PAGE_SIZE), F32),
                   jax.ShapeDtypeStruct((bs * ts, PAGE_SIZE), F32)),
        compiler_params=_cparams(2),
        name="sample_scores",
    )(page_table, *([cache_kidx] * npg), qi, w, iknew)


def _select_kernel(sc_ref, qpos_ref, tri_ref, bias_ref, *, kb, nk, topk):
    _select_bias(sc_ref, bias_ref, tri_ref, nk, kb, 0, qpos_ref[...], topk)


def _select(sc_t, qpos, topk, kb=128):
    nkeys, nq = sc_t.shape
    tri = _tri(kb)
    col = pl.BlockSpec((nkeys, LANES), lambda i: (0, i))
    return pl.pallas_call(
        functools.partial(_select_kernel, kb=kb, nk=nkeys // kb, topk=topk),
        grid=(nq // LANES,),
        in_specs=[col, pl.BlockSpec((1, LANES), lambda i: (0, i)),
                  pl.BlockSpec(tri.shape, lambda i: (0, 0))],
        out_specs=col,
        out_shape=jax.ShapeDtypeStruct((nkeys, nq), F32),
        compiler_params=_cparams(1),
        name="sample_select",
    )(sc_t, qpos, tri)


def _samp_att_kernel(pt_ref, *refs, npg, ts):
    del pt_ref
    kp = refs[:npg]
    vp = refs[npg:2 * npg]
    q_ref, bias_ref, biasn_ref, kn_ref, vn_ref, ag_ref, y_ref, m_ref, l_ref, acc_ref = refs[2 * npg:]
    j = pl.program_id(1)
    nj = pl.num_programs(1)

    @pl.when(j == 0)
    def _init():
        m_ref[...] = jnp.full(m_ref.shape, NEG_INF, F32)
        l_ref[...] = jnp.zeros(l_ref.shape, F32)
        acc_ref[...] = jnp.zeros(acc_ref.shape, F32)

    q = q_ref[...]

    def update(ks, vs, bias):
        logit = jnp.concatenate(
            [lax.dot_general(q, k.astype(BF16), _NT, preferred_element_type=F32) for k in ks], axis=1)
        logit = logit + jnp.concatenate([bias] * ATT_HEADS, axis=0)
        m_old = m_ref[...]
        m_new = jnp.maximum(m_old, jnp.max(logit, axis=1, keepdims=True))
        alpha = jnp.exp(m_old - m_new)
        p = jnp.exp(logit - m_new)
        l_ref[...] = alpha * l_ref[...] + jnp.sum(p, axis=1, keepdims=True)
        pb = p.astype(BF16)
        pv = jnp.dot(pb[:, 0:PAGE_SIZE], vs[0].astype(BF16), preferred_element_type=F32)
        for i in range(1, len(vs)):
            pv = pv + jnp.dot(pb[:, i * PAGE_SIZE:(i + 1) * PAGE_SIZE], vs[i].astype(BF16),
                              preferred_element_type=F32)
        acc_ref[...] = alpha * acc_ref[...] + pv
        m_ref[...] = m_new

    update([r[...] for r in kp], [r[...] for r in vp], bias_ref[...])

    @pl.when(j == nj - 1)
    def _fin():
        update([kn_ref[...]], [vn_ref[...]], biasn_ref[...])
        o = acc_ref[...] / l_ref[...]
        gate = ag_ref[...]
        gsil = gate * jax.nn.sigmoid(gate)
        for h in range(ATT_HEADS):
            sl = slice(h * HEAD_DIM, (h + 1) * HEAD_DIM)
            y_ref[:, sl] = (o[h * ts:(h + 1) * ts, sl] * gsil[:, sl]).astype(BF16)


def _samp_attention(page_table, cache_k, cache_v, layer, qbd, bias, biasn, knew, vnew, ag, npg=8):
    bs, n_pages = page_table.shape
    ts = qbd.shape[1] // ATT_HEADS
    page_spec = lambda i: pl.BlockSpec(
        (None, None, PAGE_SIZE, D_ATT), lambda b, j, pt, i=i: (layer, pt[b, j * npg + i], 0, 0))
    per_b = lambda a: pl.BlockSpec((None,) + a.shape[1:], lambda b, j, pt: (b, 0, 0))
    gs = pltpu.PrefetchScalarGridSpec(
        num_scalar_prefetch=1,
        grid=(bs, n_pages // npg),
        in_specs=[page_spec(i) for i in range(npg)] * 2
        + [per_b(qbd),
           pl.BlockSpec((ts, npg * PAGE_SIZE), lambda b, j, pt: (b, j)),
           pl.BlockSpec((ts, PAGE_SIZE), lambda b, j, pt: (b, 0)),
           per_b(knew), per_b(vnew),
           pl.BlockSpec((ts, D_ATT), lambda b, j, pt: (b, 0))],
        out_specs=pl.BlockSpec((ts, D_ATT), lambda b, j, pt: (b, 0)),
        scratch_shapes=[pltpu.VMEM((ATT_HEADS * ts, 1), F32), pltpu.VMEM((ATT_HEADS * ts, 1), F32),
                        pltpu.VMEM((ATT_HEADS * ts, D_ATT), F32)],
    )
    return pl.pallas_call(
        functools.partial(_samp_att_kernel, npg=npg, ts=ts),
        grid_spec=gs,
        out_shape=jax.ShapeDtypeStruct((bs * ts, D_ATT), BF16),
        compiler_params=_cparams(2),
        name="sample_attention",
    )(page_table, *([cache_k] * npg), *([cache_v] * npg), qbd, bias, biasn, knew, vnew, ag)


def _dsa_sample(page_table, cache_k4, cache_v4, cache_kidx, layer, ts, ak, av, ik, ag, aqb, iqb, iwt):
    bs, n_pages = page_table.shape
    past = n_pages * PAGE_SIZE
    n = bs * ts
    topk = min(TOPK_MAX, (past + ts) // 4)
    pad_rows = lambda a: jnp.pad(a.reshape(bs, ts, a.shape[-1]), ((0, 0), (0, PAGE_SIZE - ts), (0, 0)))
    qi = iqb.reshape(bs, ts, IDX_HEADS, IDX_DIM).transpose(0, 2, 1, 3).reshape(bs, IDX_HEADS * ts, IDX_DIM)
    w = iwt[:IDX_HEADS].reshape(IDX_HEADS, bs, ts).transpose(1, 0, 2).reshape(bs, IDX_HEADS * ts, 1)
    w = jnp.broadcast_to(w, (bs, IDX_HEADS * ts, LANES))
    sc, scn = _samp_scores(page_table, cache_kidx, layer, qi, w, pad_rows(ik))
    sc_t = jnp.concatenate([sc, scn], axis=1).T
    qpos = (past + jnp.arange(n, dtype=I32) % ts).reshape(1, n)
    bias_all = _select(sc_t, qpos, topk).T
    eye = jnp.eye(ATT_HEADS, dtype=BF16)
    q4 = aqb.reshape(bs, ts, ATT_HEADS, HEAD_DIM)
    qbd = (eye[None, :, None, :, None] * q4.transpose(0, 2, 1, 3)[:, :, :, None, :]).reshape(
        bs, ATT_HEADS * ts, D_ATT)
    return _samp_attention(page_table, cache_k4, cache_v4, layer, qbd, bias_all[:, :past],
                           bias_all[:, past:], pad_rows(ak), pad_rows(av), ag)


def _mixer_layer(x, pos, h0r, h0i, s0, attend, g, w_in_l, w_out_l, ssm_prm, ret_g, gf, final, tm, tl):
    bv, tv, _ = x.shape
    n = bv * tv
    tm = min(tm, n)
    wm = w_in_l[:, :C_IW].astype(BF16)
    wvt = w_in_l[:, C_AV:C_AG].T.astype(BF16)
    wwt = jnp.pad(w_in_l[:, C_IW:C_END].T, ((0, SUBLANES - IDX_HEADS), (0, 0))).astype(BF16)
    xf = x.reshape(n, D_MODEL)
    (u2, r4, ak, av, ik, ag, aqb, akb, avtb, iqb, ikb, iwt) = _inproj(
        xf, g.astype(F32).reshape(1, D_MODEL), wm, wvt, wwt, tm)
    y_ssm, hr, hi = _ssm(u2.reshape(bv, tv, -1), h0r, h0i, ssm_prm, tl)
    y_ret, s_new = _ret(r4.reshape(bv, tv, -1), s0, pos, ret_g)
    y_att = attend(ak, av, ik, ag, aqb, akb, avtb, iqb, ikb, iwt)
    y = _outproj(xf, y_ssm.reshape(n, D_SSM), y_ret.reshape(n, D_RET), y_att,
                 w_out_l.astype(BF16), gf.astype(F32).reshape(1, D_MODEL), tm, final)
    return y.reshape(bv, tv, D_MODEL), (ak.reshape(bv, tv, -1), av.reshape(bv, tv, -1),
                                        ik.reshape(bv, tv, -1), hr, hi, s_new)


def kernel(x_prompt, x_sample, cache_k, cache_v, cache_kidx, state_ssm_re, state_ssm_im, state_ret,
           page_table, norm_g, w_in, w_out, ssm_lambda_re, ssm_lambda_im, ssm_b_re, ssm_b_im,
           ssm_c_re, ssm_c_im, ssm_d, ssm_log_step, ssm_w_glu, ssm_b_glu, ret_norm_g, final_norm_g):
    bp, tp, _ = x_prompt.shape
    bs, ts, _ = x_sample.shape
    depth = w_in.shape[0]
    past = page_table.shape[1] * PAGE_SIZE
    assert ts == SUBLANES and tp % Q_BLOCK == 0
    pos_p = jnp.arange(tp)
    pos_s = past + jnp.arange(ts)
    cache_k4 = cache_k.reshape(cache_k.shape[:3] + (D_ATT,))
    cache_v4 = cache_v.reshape(cache_v.shape[:3] + (D_ATT,))
    zero_h = jnp.zeros((bp, 1, N_STATE), F32)
    zero_s = jnp.zeros((bp, RET_HEADS, HEAD_DIM, HEAD_DIM), F32)
    hp, hs = x_prompt, x_sample
    outs_p, outs_s = [], []
    for l in range(depth):
        final = l == depth - 1
        prm = _ssm_params(ssm_lambda_re[l], ssm_lambda_im[l], ssm_b_re[l], ssm_b_im[l], ssm_c_re[l],
                          ssm_c_im[l], ssm_d[l], ssm_log_step[l], ssm_w_glu[l], ssm_b_glu[l])

        def attend_p(ak, av, ik, ag, aqb, akb, avtb, iqb, ikb, iwt):
            return _dsa_prompt(bp, tp, ikb, akb, avtb, iqb, iwt, aqb, ag)

        hp, st_p = _mixer_layer(hp, pos_p, zero_h, zero_h, zero_s, attend_p, norm_g[l], w_in[l], w_out[l],
                                prm, ret_norm_g[l], final_norm_g, final, tm=256, tl=min(512, tp))

        def attend_s(ak, av, ik, ag, aqb, akb, avtb, iqb, ikb, iwt):
            return _dsa_sample(page_table, cache_k4, cache_v4, cache_kidx, l, ts, ak, av, ik, ag,
                               aqb, iqb, iwt)

        hs, st_s = _mixer_layer(hs, pos_s, state_ssm_re[l].reshape(bs, 1, N_STATE),
                                state_ssm_im[l].reshape(bs, 1, N_STATE), state_ret[l], attend_s,
                                norm_g[l], w_in[l], w_out[l], prm, ret_norm_g[l], final_norm_g, final,
                                tm=256, tl=ts)
        outs_p.append(st_p)
        outs_s.append(st_s)

    def pack(outs, b, t):
        k = jnp.stack([o[0] for o in outs]).reshape(depth, b, t, ATT_HEADS, HEAD_DIM)
        v = jnp.stack([o[1] for o in outs]).reshape(depth, b, t, ATT_HEADS, HEAD_DIM)
        ki = jnp.stack([o[2] for o in outs])
        hr = jnp.stack([o[3] for o in outs]).reshape(depth, b, SSM_GROUPS, SSM_STATE)
        hi = jnp.stack([o[4] for o in outs]).reshape(depth, b, SSM_GROUPS, SSM_STATE)
        s = jnp.stack([o[5] for o in outs])
        return k, v, ki, hr, hi, s

    return (hp, hs) + pack(outs_p, bp, tp) + pack(outs_s, bs, ts)
```

```python
import functools
import math

import jax
import jax.numpy as jnp
import numpy as np
from jax import lax
from jax.experimental import pallas as pl
from jax.experimental.pallas import tpu as pltpu

F32 = jnp.float32
BF16 = jnp.bfloat16
I32 = jnp.int32

D_MODEL = 1024
HEAD_DIM = 64
SSM_GROUP = 16
D_SSM = 256
SSM_GROUPS = D_SSM // SSM_GROUP
SSM_STATE = 64
N_STATE = SSM_GROUPS * SSM_STATE
D_RET = 384
RET_HEADS = D_RET // HEAD_DIM
RET_CHUNK = 128
ROPE_BASE = 10000.0
D_ATT = 384
ATT_HEADS = D_ATT // HEAD_DIM
IDX_HEADS = 4
IDX_DIM = 64
TOPK_MAX = 256
Q_BLOCK = 128
PAGE_SIZE = 128
NORM_EPS = 1e-6
NEG_INF = -1e30

_SIZES = (D_SSM, D_SSM, D_RET, D_RET, D_RET, D_RET, D_ATT, D_ATT, D_ATT, D_ATT,
          IDX_HEADS * IDX_DIM, IDX_DIM, IDX_HEADS)
_OFF = np.concatenate([[0], np.cumsum(_SIZES)]).tolist()
C_U, C_RET, C_AQ, C_AK, C_AV, C_AG, C_IQ, C_IK, C_IW, C_END = (
    _OFF[0], _OFF[2], _OFF[6], _OFF[7], _OFF[8], _OFF[9], _OFF[10], _OFF[11], _OFF[12], _OFF[13])

LANES = 128
SUBLANES = 8
F32_INF_BITS = 0x7F800000
VMEM_LIMIT = 56 * 1024 * 1024

_NT = (((1,), (1,)), ((), ()))


def _cparams(n_axes):
    return pltpu.CompilerParams(dimension_semantics=("arbitrary",) * n_axes,
                                vmem_limit_bytes=VMEM_LIMIT)


def _inproj_kernel(x_ref, g_ref, wm_ref, wvt_ref, wwt_ref, wqp_ref, wkp_ref,
                   u2_ref, r4_ref, ak_ref, av_ref, ik_ref, ag_ref,
                   aqb_ref, akb_ref, avtb_ref, iqb_ref, ikb_ref, iwt_ref):
    x = x_ref[...]
    ms = jnp.mean(x * x, axis=-1, keepdims=True)
    xn = (x * lax.rsqrt(ms + NORM_EPS) * g_ref[...]).astype(BF16)

    def seg(a, b):
        return jnp.dot(xn, wm_ref[:, a:b], preferred_element_type=F32)

    u2_ref[...] = seg(C_U, C_RET)
    r4_ref[...] = seg(C_RET, C_AQ)
    aqb_ref[...] = (jnp.dot(xn, wqp_ref[...], preferred_element_type=F32) * HEAD_DIM ** -0.5).astype(BF16)
    ak_ref[...] = seg(C_AK, C_AV)
    akb_ref[...] = jnp.dot(xn, wkp_ref[...], preferred_element_type=F32).astype(BF16)
    av_ref[...] = seg(C_AV, C_AG)
    ag_ref[...] = seg(C_AG, C_IQ)
    iqb_ref[...] = seg(C_IQ, C_IK).astype(BF16)
    ik = seg(C_IK, C_IW)
    ik_ref[...] = ik
    ikb_ref[...] = ik.astype(BF16)
    avtb_ref[...] = lax.dot_general(wvt_ref[...], xn, _NT, preferred_element_type=F32).astype(BF16)
    iwt_ref[...] = lax.dot_general(wwt_ref[...], xn, _NT, preferred_element_type=F32) * (
        IDX_HEADS ** -0.5 * IDX_DIM ** -0.5)


def _inproj(x, g, wm, wvt, wwt, wqp, wkp, tm):
    n = x.shape[0]
    row = lambda w: pl.BlockSpec((tm, w), lambda i: (i, 0))
    col = lambda h: pl.BlockSpec((h, tm), lambda i: (0, i))
    full = lambda a: pl.BlockSpec(a.shape, lambda i: (0,) * a.ndim)
    widths_f32 = (C_RET - C_U, C_AQ - C_RET, D_ATT, D_ATT, IDX_DIM, D_ATT)
    out_shape = tuple(jax.ShapeDtypeStruct((n, w), F32) for w in widths_f32) + (
        jax.ShapeDtypeStruct((n, ATT_HEADS * LANES), BF16),
        jax.ShapeDtypeStruct((n, ATT_HEADS * LANES), BF16),
        jax.ShapeDtypeStruct((D_ATT, n), BF16),
        jax.ShapeDtypeStruct((n, IDX_HEADS * IDX_DIM), BF16),
        jax.ShapeDtypeStruct((n, IDX_DIM), BF16),
        jax.ShapeDtypeStruct((SUBLANES, n), F32),
    )
    out_specs = tuple(row(w) for w in widths_f32) + (
        row(ATT_HEADS * LANES), row(ATT_HEADS * LANES), col(D_ATT), row(IDX_HEADS * IDX_DIM), row(IDX_DIM),
        col(SUBLANES))
    return pl.pallas_call(
        _inproj_kernel,
        grid=(n // tm,),
        in_specs=[row(D_MODEL), full(g), full(wm), full(wvt), full(wwt), full(wqp), full(wkp)],
        out_specs=out_specs,
        out_shape=out_shape,
        compiler_params=_cparams(1),
        name="inproj",
    )(x, g, wm, wvt, wwt, wqp, wkp)


def _outproj_kernel(x_ref, ys_ref, yr_ref, ya_ref, wo_ref, gf_ref, o_ref, *, final):
    y = x_ref[...]
    y = y + jnp.dot(ys_ref[...], wo_ref[0:D_SSM, :], preferred_element_type=F32)
    y = y + jnp.dot(yr_ref[...], wo_ref[D_SSM:D_SSM + D_RET, :], preferred_element_type=F32)
    y = y + jnp.dot(ya_ref[...], wo_ref[D_SSM + D_RET:, :], preferred_element_type=F32)
    if final:
        ms = jnp.mean(y * y, axis=-1, keepdims=True)
        y = y * lax.rsqrt(ms + NORM_EPS) * gf_ref[...]
    o_ref[...] = y


def _outproj(x, ys, yr, ya, wo, gf, tm, final):
    n = x.shape[0]
    row = lambda w: pl.BlockSpec((tm, w), lambda i: (i, 0))
    full = lambda a: pl.BlockSpec(a.shape, lambda i: (0,) * a.ndim)
    return pl.pallas_call(
        functools.partial(_outproj_kernel, final=final),
        grid=(n // tm,),
        in_specs=[row(D_MODEL), row(D_SSM), row(D_RET), row(D_ATT), full(wo), full(gf)],
        out_specs=row(D_MODEL),
        out_shape=jax.ShapeDtypeStruct((n, D_MODEL), F32),
        compiler_params=_cparams(1),
        name="outproj",
    )(x, ys, yr, ya, wo, gf)


def _cmul(ar, ai, br, bi):
    return ar * br - ai * bi, ar * bi + ai * br


def _ssm_kernel(u2_ref, h0r_ref, h0i_ref, lr_ref, li_ref, ls_ref, btr_ref, bti_ref,
                ctr_ref, cti_ref, d_ref, wg_ref, bg_ref,
                y_ref, hr_out_ref, hi_out_ref,
                bbr_ref, bbi_ref, apr_ref, api_ref, cr_ref, ci_ref, hr_ref, hi_ref, *, tl):
    b = pl.program_id(0)
    c = pl.program_id(1)
    nc = pl.num_programs(1)

    @pl.when((b == 0) & (c == 0))
    def _prep():
        lr, li = lr_ref[...], li_ref[...]
        dt = jnp.exp(ls_ref[...])
        mag = jnp.exp(lr * dt)
        abr, abi = mag * jnp.cos(li * dt), mag * jnp.sin(li * dt)
        den = lr * lr + li * li
        nr, ni = abr - 1.0, abi
        fr = (nr * lr + ni * li) / den
        fi = (ni * lr - nr * li) / den
        bbr_ref[...] = (fr * btr_ref[...] - fi * bti_ref[...]).astype(BF16)
        bbi_ref[...] = (fr * bti_ref[...] + fi * btr_ref[...]).astype(BF16)
        pr, pi = abr, abi
        rows_r, rows_i = [pr], [pi]
        for _ in range(SUBLANES - 1):
            pr, pi = _cmul(pr, pi, abr, abi)
            rows_r.append(pr)
            rows_i.append(pi)
        apr_ref[...] = jnp.concatenate(rows_r, axis=0)
        api_ref[...] = jnp.concatenate(rows_i, axis=0)

    @pl.when(c == 0)
    def _init():
        cr_ref[...] = h0r_ref[...]
        ci_ref[...] = h0i_ref[...]

    u = u2_ref[:, 0:D_SSM]
    gate = u2_ref[:, D_SSM:2 * D_SSM]
    ub = u.astype(BF16)
    hr_ref[...] = jnp.dot(ub, bbr_ref[...], preferred_element_type=F32)
    hi_ref[...] = jnp.dot(ub, bbi_ref[...], preferred_element_type=F32)

    apr, api = apr_ref[...], api_ref[...]
    rowid = lax.broadcasted_iota(I32, (SUBLANES, N_STATE), 0)

    def blk(j, carry):
        cr, ci = carry
        r0 = pl.multiple_of(j * SUBLANES, SUBLANES)
        br = hr_ref[pl.ds(r0, SUBLANES), :]
        bi = hi_ref[pl.ds(r0, SUBLANES), :]
        for k in (1, 2, 4):
            keep = rowid >= k
            sr = jnp.where(keep, pltpu.roll(br, k, 0), 0.0)
            si = jnp.where(keep, pltpu.roll(bi, k, 0), 0.0)
            akr, aki = apr[k - 1:k, :], api[k - 1:k, :]
            br, bi = br + (akr * sr - aki * si), bi + (akr * si + aki * sr)
        hr = br + (apr * cr - api * ci)
        hi = bi + (apr * ci + api * cr)
        hr_ref[pl.ds(r0, SUBLANES), :] = hr
        hi_ref[pl.ds(r0, SUBLANES), :] = hi
        return hr[SUBLANES - 1:SUBLANES, :], hi[SUBLANES - 1:SUBLANES, :]

    cr, ci = lax.fori_loop(0, tl // SUBLANES, blk, (cr_ref[...], ci_ref[...]))
    cr_ref[...] = cr
    ci_ref[...] = ci

    @pl.when(c == nc - 1)
    def _fin():
        hr_out_ref[...] = cr
        hi_out_ref[...] = ci

    y = (jnp.dot(hr_ref[...].astype(BF16), ctr_ref[...], preferred_element_type=F32)
         - jnp.dot(hi_ref[...].astype(BF16), cti_ref[...], preferred_element_type=F32)
         + d_ref[...] * u)
    g = jax.nn.gelu(y)
    y = g * jax.nn.sigmoid(jnp.dot(g.astype(BF16), wg_ref[...], preferred_element_type=F32) + bg_ref[...])
    y_ref[...] = (y * (gate * jax.nn.sigmoid(gate))).astype(BF16)


def _ssm(u2, h0r, h0i, prm, tl):
    bv, tv, _ = u2.shape
    full = lambda a: pl.BlockSpec(a.shape, lambda b, c: (0,) * a.ndim)
    st = pl.BlockSpec((None, 1, N_STATE), lambda b, c: (b, 0, 0))
    return pl.pallas_call(
        functools.partial(_ssm_kernel, tl=tl),
        grid=(bv, tv // tl),
        in_specs=[pl.BlockSpec((None, tl, 2 * D_SSM), lambda b, c: (b, c, 0)), st, st]
        + [full(a) for a in prm],
        out_specs=(pl.BlockSpec((None, tl, D_SSM), lambda b, c: (b, c, 0)), st, st),
        out_shape=(jax.ShapeDtypeStruct((bv, tv, D_SSM), BF16),
                   jax.ShapeDtypeStruct((bv, 1, N_STATE), F32),
                   jax.ShapeDtypeStruct((bv, 1, N_STATE), F32)),
        scratch_shapes=[pltpu.VMEM((D_SSM, N_STATE), BF16), pltpu.VMEM((D_SSM, N_STATE), BF16),
                        pltpu.VMEM((SUBLANES, N_STATE), F32), pltpu.VMEM((SUBLANES, N_STATE), F32),
                        pltpu.VMEM((1, N_STATE), F32), pltpu.VMEM((1, N_STATE), F32),
                        pltpu.VMEM((tl, N_STATE), F32), pltpu.VMEM((tl, N_STATE), F32)],
        compiler_params=_cparams(2),
        name="ssm",
    )(u2, h0r, h0i, *prm)


def _ssm_params(lam_re, lam_im, b_re, b_im, c_re, c_im, d_skip, log_step, w_glu, b_glu):
    eye = jnp.eye(SSM_GROUPS, dtype=F32)

    def bdiag(m):
        g, r, c = m.shape
        return (eye[:, None, :, None] * m[:, :, None, :]).reshape(g * r, g * c)

    flat = lambda a: a.astype(F32).reshape(1, N_STATE)
    ls = jnp.broadcast_to(log_step.astype(F32)[:, None], (SSM_GROUPS, SSM_STATE))
    btr = bdiag(jnp.swapaxes(b_re.astype(F32), 1, 2))
    bti = bdiag(jnp.swapaxes(b_im.astype(F32), 1, 2))
    ctr = bdiag(jnp.swapaxes(c_re.astype(F32), 1, 2)).astype(BF16)
    cti = bdiag(jnp.swapaxes(c_im.astype(F32), 1, 2)).astype(BF16)
    return (flat(lam_re), flat(lam_im), flat(ls), btr, bti, ctr, cti,
            d_skip.astype(F32).reshape(1, D_SSM), w_glu.astype(BF16),
            b_glu.astype(F32).reshape(1, D_SSM))


def _swap_halves(x):
    lane = lax.broadcasted_iota(I32, (x.shape[0], LANES), 1)
    first = (lane % HEAD_DIM) < (HEAD_DIM // 2)
    tiles = []
    for t in range(x.shape[1] // LANES):
        xt = x[:, t * LANES:(t + 1) * LANES]
        up = pltpu.roll(xt, LANES - HEAD_DIM // 2, 1)
        dn = pltpu.roll(xt, HEAD_DIM // 2, 1)
        tiles.append(jnp.where(first, up, dn))
    return jnp.concatenate(tiles, axis=1)


def _ret_kernel(r4_ref, s0_ref, cos_ref, sin_ref, din_ref, dq_ref, dk_ref, ds_ref, g_ref,
                y_ref, s_out_ref, s_ref):
    c = pl.program_id(1)
    nc = pl.num_programs(1)

    @pl.when(c == 0)
    def _init():
        s_ref[...] = s0_ref[...]

    q = r4_ref[:, 0:D_RET]
    k = r4_ref[:, D_RET:2 * D_RET]
    v = r4_ref[:, 2 * D_RET:3 * D_RET]
    gate = r4_ref[:, 3 * D_RET:4 * D_RET]
    cos = jnp.concatenate([cos_ref[...]] * (D_RET // LANES), axis=1)
    sin = jnp.concatenate([sin_ref[...]] * (D_RET // LANES), axis=1)
    qr = q * cos + _swap_halves(q) * sin
    kr = (k * cos + _swap_halves(k) * sin) * HEAD_DIM ** -0.5
    qb = qr.astype(BF16)
    kb = kr.astype(BF16)
    vb = v.astype(BF16)
    kdt = (kr * dk_ref[...]).T.astype(BF16)
    dq = dq_ref[...]
    gsil = gate * jax.nn.sigmoid(gate) * g_ref[...]
    for h in range(RET_HEADS):
        sl = slice(h * HEAD_DIM, (h + 1) * HEAD_DIM)
        qh, kh, vh = qb[:, sl], kb[:, sl], vb[:, sl]
        s = s_ref[h]
        att = lax.dot_general(qh, kh, _NT, preferred_element_type=F32) * din_ref[h]
        inner = jnp.dot(att.astype(BF16), vh, preferred_element_type=F32)
        cross = jnp.dot(qh, s.astype(BF16), preferred_element_type=F32) * dq[:, sl]
        s_ref[h] = s * ds_ref[h] + jnp.dot(kdt[sl, :], vh, preferred_element_type=F32)
        o = inner + cross
        mu = jnp.mean(o, axis=-1, keepdims=True)
        var = jnp.mean(jnp.square(o - mu), axis=-1, keepdims=True)
        y_ref[:, sl] = ((o - mu) * lax.rsqrt(var + NORM_EPS) * gsil[:, sl]).astype(BF16)

    @pl.when(c == nc - 1)
    def _fin():
        s_out_ref[...] = s_ref[...]


def _ret_tables(pos, chunk):
    half = HEAD_DIM // 2
    inv = ROPE_BASE ** (-jnp.arange(half, dtype=F32) / half)
    ang = pos.astype(F32)[:, None] * inv[None, :]
    cos, sin = jnp.cos(ang), jnp.sin(ang)
    cos_t = jnp.concatenate([cos, cos] * (LANES // HEAD_DIM), axis=1)
    sin_t = jnp.concatenate([-sin, sin] * (LANES // HEAD_DIM), axis=1)
    log_g = jnp.log1p(-jnp.exp2(-5.0 - jnp.arange(RET_HEADS, dtype=F32)))
    j = jnp.arange(chunk, dtype=F32)
    rel = j[:, None] - j[None, :]
    din = jnp.where(rel[None] >= 0, jnp.exp(log_g[:, None, None] * jnp.maximum(rel, 0.0)[None]), 0.0)
    dq = jnp.repeat(jnp.exp(log_g[None, :] * (j[:, None] + 1.0)), HEAD_DIM, axis=1)
    dk = jnp.repeat(jnp.exp(log_g[None, :] * (chunk - 1.0 - j[:, None])), HEAD_DIM, axis=1)
    ds = jnp.broadcast_to(jnp.exp(log_g * chunk)[:, None, None], (RET_HEADS, HEAD_DIM, HEAD_DIM))
    return cos_t, sin_t, din, dq, dk, ds


def _ret(r4, s0, pos, norm_g):
    bv, tv, _ = r4.shape
    chunk = RET_CHUNK if tv % RET_CHUNK == 0 else tv
    cos_t, sin_t, din, dq, dk, ds = _ret_tables(pos, chunk)
    g = norm_g.astype(F32).reshape(1, D_RET)
    full = lambda a: pl.BlockSpec(a.shape, lambda b, c: (0,) * a.ndim)
    st = pl.BlockSpec((None, RET_HEADS, HEAD_DIM, HEAD_DIM), lambda b, c: (b, 0, 0, 0))
    tab = pl.BlockSpec((chunk, LANES), lambda b, c: (c, 0))
    return pl.pallas_call(
        _ret_kernel,
        grid=(bv, tv // chunk),
        in_specs=[pl.BlockSpec((None, chunk, 4 * D_RET), lambda b, c: (b, c, 0)), st, tab, tab,
                  full(din), full(dq), full(dk), full(ds), full(g)],
        out_specs=(pl.BlockSpec((None, chunk, D_RET), lambda b, c: (b, c, 0)), st),
        out_shape=(jax.ShapeDtypeStruct((bv, tv, D_RET), BF16),
                   jax.ShapeDtypeStruct((bv, RET_HEADS, HEAD_DIM, HEAD_DIM), F32)),
        scratch_shapes=[pltpu.VMEM((RET_HEADS, HEAD_DIM, HEAD_DIM), F32)],
        compiler_params=_cparams(2),
        name="retention",
    )(r4, s0, cos_t, sin_t, din, dq, dk, ds, g)


def _count(sc_ref, nk, kb, pred):
    n_acc = 4

    def body(c, cnts):
        r0 = pl.multiple_of(c * kb, kb)
        cnts = list(cnts)
        for j in range(kb // SUBLANES):
            blk = sc_ref[pl.ds(r0 + j * SUBLANES, SUBLANES), :]
            cnts[j % n_acc] = cnts[j % n_acc] + jnp.where(pred(blk), 1, 0)
        return tuple(cnts)
    zero = jnp.zeros((SUBLANES, LANES), I32)
    cnts = lax.fori_loop(0, nk, body, (zero,) * n_acc)
    return jnp.sum((cnts[0] + cnts[1]) + (cnts[2] + cnts[3]), axis=0, keepdims=True)


def _select_bias(sc_ref, bias_ref, tri_ref, nk, kb, n_beyond, qpos, topk):
    def count_ge(t):
        return _count(sc_ref, nk, kb, lambda blk: blk >= t) + jnp.where(t <= NEG_INF, n_beyond, 0)

    pos = jnp.where(count_ge(jnp.zeros((1, LANES), F32)) >= topk, 1, 0)
    sign = jnp.where(pos == 1, 0, np.int32(-2 ** 31))

    def bit_step(i, cur):
        cand = cur | lax.shift_left(jnp.int32(1), 30 - i)
        t = lax.bitcast_convert_type(cand | sign, F32)
        ok = jnp.where(count_ge(t) >= topk, 1, 0)
        finite = jnp.where(cand <= F32_INF_BITS, 1, 0)
        keep = jnp.where(pos == 1, ok, (1 - ok) * finite)
        return jnp.where(keep == 1, cand, cur)

    cur = lax.fori_loop(0, 31, bit_step, jnp.zeros((1, LANES), I32))
    mag = jnp.where(pos == 1, cur, cur + 1)
    tau = lax.bitcast_convert_type(mag | sign, F32)
    surplus_ties = jnp.max(count_ge(tau)) > topk

    @pl.when(jnp.logical_not(surplus_ties))
    def _no_surplus():
        def body(c, _):
            r0 = pl.multiple_of(c * kb, kb)
            blk = sc_ref[pl.ds(r0, kb), :]
            kidx = r0 + lax.broadcasted_iota(I32, (kb, LANES), 0)
            bias_ref[pl.ds(r0, kb), :] = jnp.where((blk >= tau) & (kidx <= qpos), 0.0, NEG_INF)
            return 0
        lax.fori_loop(0, nk, body, 0)

    @pl.when(surplus_ties)
    def _ties_by_index():
        c_gt = _count(sc_ref, nk, kb, lambda blk: blk > tau) + jnp.where(NEG_INF > tau, n_beyond, 0)
        need = (topk - c_gt).astype(F32)

        def body(c, carry):
            r0 = pl.multiple_of(c * kb, kb)
            blk = sc_ref[pl.ds(r0, kb), :]
            eq = blk == tau
            pref = jnp.dot(tri_ref[...], jnp.where(eq, 1.0, 0.0).astype(BF16),
                           preferred_element_type=F32) + carry
            kidx = r0 + lax.broadcasted_iota(I32, (kb, LANES), 0)
            sel = ((blk > tau) | (eq & (pref <= need))) & (kidx <= qpos)
            bias_ref[pl.ds(r0, kb), :] = jnp.where(sel, 0.0, NEG_INF)
            return pref[kb - 1:kb, :]

        lax.fori_loop(0, nk, body, jnp.zeros((1, LANES), F32))


def _tri(kb):
    r = np.arange(kb)
    return jnp.asarray((r[None, :] <= r[:, None]).astype(np.float32), dtype=BF16)


def _dsa_prompt_kernel(ikb_ref, akb_ref, avtb_ref, iqb_ref, iwt_ref, aqb_ref, ag_ref, tri_ref,
                       y_ref, sc_ref, bias_ref, lg_ref, *o_refs, kb, ka, t_total, topk):
    i = pl.program_id(1)
    q0 = i * Q_BLOCK
    nk = (q0 + Q_BLOCK + kb - 1) // kb
    qpos = q0 + lax.broadcasted_iota(I32, (1, LANES), 1)
    iq = iqb_ref[...]
    iw = iwt_ref[...]
    qcat = jnp.concatenate([iq[:, h * IDX_DIM:(h + 1) * IDX_DIM] for h in range(IDX_HEADS)], axis=0)

    def score_body(c, _):
        r0 = pl.multiple_of(c * kb, kb)
        keys = ikb_ref[pl.ds(r0, kb), :]
        s = lax.dot_general(keys, qcat, _NT, preferred_element_type=F32)
        parts = [jnp.maximum(s[:, h * LANES:(h + 1) * LANES], 0.0) * iw[h:h + 1, :]
                 for h in range(IDX_HEADS)]
        acc = (parts[0] + parts[1]) + (parts[2] + parts[3])
        kidx = r0 + lax.broadcasted_iota(I32, (kb, LANES), 0)
        sc_ref[pl.ds(r0, kb), :] = jnp.where(kidx <= qpos, acc, NEG_INF)
        return 0

    lax.fori_loop(0, nk, score_body, 0)
    _select_bias(sc_ref, bias_ref, tri_ref, nk, kb, t_total - nk * kb, qpos, topk)

    aq = aqb_ref[...]
    qhs = [aq[:, h * LANES:(h + 1) * LANES] for h in range(ATT_HEADS)]
    for o_ref in o_refs:
        o_ref[...] = jnp.zeros(o_ref.shape, F32)

    def att_body(c, carry):
        ms, ls = carry
        r0 = pl.multiple_of(c * ka, ka)
        bias = bias_ref[pl.ds(r0, ka), :]
        ms_new, ls_new = [], []
        mx = []
        for h in range(ATT_HEADS):
            kh = akb_ref[pl.ds(r0, ka), h * LANES:(h + 1) * LANES]
            logit = lax.dot_general(kh, qhs[h], _NT, preferred_element_type=F32) + bias
            lg_ref[h] = logit
            mx.append(jnp.max(logit, axis=0, keepdims=True))
        for h in range(ATT_HEADS):
            m_new = jnp.maximum(ms[h], mx[h])
            alpha = jnp.exp(ms[h] - m_new)
            p = jnp.exp(lg_ref[h] - m_new)
            ls_new.append(alpha * ls[h] + jnp.sum(p, axis=0, keepdims=True))
            ms_new.append(m_new)
            vt = avtb_ref[h * HEAD_DIM:(h + 1) * HEAD_DIM, pl.ds(r0, ka)]
            o_refs[h][...] = alpha * o_refs[h][...] + jnp.dot(vt, p.astype(BF16), preferred_element_type=F32)
        return tuple(ms_new), tuple(ls_new)

    m0 = (jnp.full((1, LANES), NEG_INF, F32),) * ATT_HEADS
    l0 = (jnp.zeros((1, LANES), F32),) * ATT_HEADS
    _, ls = lax.fori_loop(0, nk * (kb // ka), att_body, (m0, l0))
    ot = jnp.concatenate([o_refs[h][...] / ls[h] for h in range(ATT_HEADS)], axis=0)
    gate = ag_ref[...]
    y_ref[...] = (ot.T * (gate * jax.nn.sigmoid(gate))).astype(BF16)


def _dsa_prompt(bv, tv, ikb, akb, avtb, iqb, iwt, aqb, ag, kb=512, ka=512):
    topk = min(TOPK_MAX, tv // 4)
    kb = min(kb, tv)
    nq = tv // Q_BLOCK
    tri = _tri(kb)
    per_b = lambda w: pl.BlockSpec((tv, w), lambda b, i: (b, 0))
    qrow = lambda w: pl.BlockSpec((Q_BLOCK, w), lambda b, i: (b * nq + i, 0))
    return pl.pallas_call(
        functools.partial(_dsa_prompt_kernel, kb=kb, ka=min(ka, kb), t_total=tv, topk=topk),
        grid=(bv, nq),
        in_specs=[per_b(IDX_DIM), per_b(ATT_HEADS * LANES),
                  pl.BlockSpec((D_ATT, tv), lambda b, i: (0, b)),
                  qrow(IDX_HEADS * IDX_DIM),
                  pl.BlockSpec((SUBLANES, Q_BLOCK), lambda b, i: (0, b * nq + i)),
                  qrow(ATT_HEADS * LANES), qrow(D_ATT),
                  pl.BlockSpec(tri.shape, lambda b, i: (0, 0))],
        out_specs=qrow(D_ATT),
        out_shape=jax.ShapeDtypeStruct((bv * tv, D_ATT), BF16),
        scratch_shapes=[pltpu.VMEM((tv, LANES), F32), pltpu.VMEM((tv, LANES), F32)]
        + [pltpu.VMEM((ATT_HEADS, min(ka, kb), LANES), F32)]
        + [pltpu.VMEM((HEAD_DIM, LANES), F32)] * ATT_HEADS,
        compiler_params=_cparams(2),
        name="dsa_prompt",
    )(ikb, akb, avtb, iqb, iwt, aqb, ag, tri)


def _samp_score_kernel(pt_ref, *refs, npg, ts):
    del pt_ref
    pages = refs[:npg]
    qi_ref, w_ref, iknew_ref, sc_ref, scn_ref = refs[npg:]
    qi = qi_ref[...]
    w = w_ref[...]

    def score(keys):
        s = lax.dot_general(qi, keys.astype(BF16), _NT, preferred_element_type=F32)
        r = jnp.maximum(s, 0.0) * w
        out = r[0:ts]
        for h in range(1, IDX_HEADS):
            out = out + r[h * ts:(h + 1) * ts]
        return out

    for i in range(npg):
        sc_ref[:, i * PAGE_SIZE:(i + 1) * PAGE_SIZE] = score(pages[i][...])

    @pl.when(pl.program_id(1) == 0)
    def _new_keys():
        sn = score(iknew_ref[...])
        col = lax.broadcasted_iota(I32, (ts, PAGE_SIZE), 1)
        row = lax.broadcasted_iota(I32, (ts, PAGE_SIZE), 0)
        scn_ref[...] = jnp.where(col <= row, sn, NEG_INF)


def _samp_scores(page_table, cache_kidx, layer, qi, w, iknew, npg=8):
    bs, n_pages = page_table.shape
    ts = qi.shape[1] // IDX_HEADS
    page_spec = lambda i: pl.BlockSpec(
        (None, None, PAGE_SIZE, IDX_DIM), lambda b, j, pt, i=i: (layer, pt[b, j * npg + i], 0, 0))
    per_b = lambda a: pl.BlockSpec((None,) + a.shape[1:], lambda b, j, pt: (b, 0, 0))
    gs = pltpu.PrefetchScalarGridSpec(
        num_scalar_prefetch=1,
        grid=(bs, n_pages // npg),
        in_specs=[page_spec(i) for i in range(npg)] + [per_b(qi), per_b(w), per_b(iknew)],
        out_specs=(pl.BlockSpec((ts, npg * PAGE_SIZE), lambda b, j, pt: (b, j)),
                   pl.BlockSpec((ts, PAGE_SIZE), lambda b, j, pt: (b, 0))),
    )
    return pl.pallas_call(
        functools.partial(_samp_score_kernel, npg=npg, ts=ts),
        grid_spec=gs,
        out_shape=(jax.ShapeDtypeStruct((bs * ts, n_pages * PAGE_SIZE), F32),
                   jax.ShapeDtypeStruct((bs * ts, PAGE_SIZE), F32)),
        compiler_params=_cparams(2),
        name="sample_scores",
    )(page_table, *([cache_kidx] * npg), qi, w, iknew)


def _select_kernel(sc_ref, qpos_ref, tri_ref, bias_ref, *, kb, nk, topk):
    _select_bias(sc_ref, bias_ref, tri_ref, nk, kb, 0, qpos_ref[...], topk)


def _select(sc_t, qpos, topk, kb=128):
    nkeys, nq = sc_t.shape
    tri = _tri(kb)
    col = pl.BlockSpec((nkeys, LANES), lambda i: (0, i))
    return pl.pallas_call(
        functools.partial(_select_kernel, kb=kb, nk=nkeys // kb, topk=topk),
        grid=(nq // LANES,),
        in_specs=[col, pl.BlockSpec((1, LANES), lambda i: (0, i)),
                  pl.BlockSpec(tri.shape, lambda i: (0, 0))],
        out_specs=col,
        out_shape=jax.ShapeDtypeStruct((nkeys, nq), F32),
        compiler_params=_cparams(1),
        name="sample_select",
    )(sc_t, qpos, tri)


def _samp_att_kernel(pt_ref, *refs, npg, ts):
    del pt_ref
    kp = refs[:npg]
    vp = refs[npg:2 * npg]
    q_ref, bias_ref, biasn_ref, kn_ref, vn_ref, ag_ref, y_ref, m_ref, l_ref, acc_ref = refs[2 * npg:]
    j = pl.program_id(1)
    nj = pl.num_programs(1)

    @pl.when(j == 0)
    def _init():
        m_ref[...] = jnp.full(m_ref.shape, NEG_INF, F32)
        l_ref[...] = jnp.zeros(l_ref.shape, F32)
        acc_ref[...] = jnp.zeros(acc_ref.shape, F32)

    q = q_ref[...]

    def update(ks, vs, bias):
        logit = jnp.concatenate(
            [lax.dot_general(q, k.astype(BF16), _NT, preferred_element_type=F32) for k in ks], axis=1)
        logit = logit + jnp.concatenate([bias] * ATT_HEADS, axis=0)
        m_old = m_ref[...]
        m_new = jnp.maximum(m_old, jnp.max(logit, axis=1, keepdims=True))
        alpha = jnp.exp(m_old - m_new)
        p = jnp.exp(logit - m_new)
        l_ref[...] = alpha * l_ref[...] + jnp.sum(p, axis=1, keepdims=True)
        pb = p.astype(BF16)
        pv = jnp.dot(pb[:, 0:PAGE_SIZE], vs[0].astype(BF16), preferred_element_type=F32)
        for i in range(1, len(vs)):
            pv = pv + jnp.dot(pb[:, i * PAGE_SIZE:(i + 1) * PAGE_SIZE], vs[i].astype(BF16),
                              preferred_element_type=F32)
        acc_ref[...] = alpha * acc_ref[...] + pv
        m_ref[...] = m_new

    update([r[...] for r in kp], [r[...] for r in vp], bias_ref[...])

    @pl.when(j == nj - 1)
    def _fin():
        update([kn_ref[...]], [vn_ref[...]], biasn_ref[...])
        o = acc_ref[...] / l_ref[...]
        gate = ag_ref[...]
        gsil = gate * jax.nn.sigmoid(gate)
        for h in range(ATT_HEADS):
            sl = slice(h * HEAD_DIM, (h + 1) * HEAD_DIM)
            y_ref[:, sl] = (o[h * ts:(h + 1) * ts, sl] * gsil[:, sl]).astype(BF16)


def _samp_attention(page_table, cache_k, cache_v, layer, qbd, bias, biasn, knew, vnew, ag, npg=8):
    bs, n_pages = page_table.shape
    ts = qbd.shape[1] // ATT_HEADS
    page_spec = lambda i: pl.BlockSpec(
        (None, None, PAGE_SIZE, D_ATT), lambda b, j, pt, i=i: (layer, pt[b, j * npg + i], 0, 0))
    per_b = lambda a: pl.BlockSpec((None,) + a.shape[1:], lambda b, j, pt: (b, 0, 0))
    gs = pltpu.PrefetchScalarGridSpec(
        num_scalar_prefetch=1,
        grid=(bs, n_pages // npg),
        in_specs=[page_spec(i) for i in range(npg)] * 2
        + [per_b(qbd),
           pl.BlockSpec((ts, npg * PAGE_SIZE), lambda b, j, pt: (b, j)),
           pl.BlockSpec((ts, PAGE_SIZE), lambda b, j, pt: (b, 0)),
           per_b(knew), per_b(vnew),
           pl.BlockSpec((ts, D_ATT), lambda b, j, pt: (b, 0))],
        out_specs=pl.BlockSpec((ts, D_ATT), lambda b, j, pt: (b, 0)),
        scratch_shapes=[pltpu.VMEM((ATT_HEADS * ts, 1), F32), pltpu.VMEM((ATT_HEADS * ts, 1), F32),
                        pltpu.VMEM((ATT_HEADS * ts, D_ATT), F32)],
    )
    return pl.pallas_call(
        functools.partial(_samp_att_kernel, npg=npg, ts=ts),
        grid_spec=gs,
        out_shape=jax.ShapeDtypeStruct((bs * ts, D_ATT), BF16),
        compiler_params=_cparams(2),
        name="sample_attention",
    )(page_table, *([cache_k] * npg), *([cache_v] * npg), qbd, bias, biasn, knew, vnew, ag)


def _dsa_sample(page_table, cache_k4, cache_v4, cache_kidx, layer, ts, ak, av, ik, ag, aqb, iqb, iwt):
    bs, n_pages = page_table.shape
    past = n_pages * PAGE_SIZE
    n = bs * ts
    topk = min(TOPK_MAX, (past + ts) // 4)
    pad_rows = lambda a: jnp.pad(a.reshape(bs, ts, a.shape[-1]), ((0, 0), (0, PAGE_SIZE - ts), (0, 0)))
    qi = iqb.reshape(bs, ts, IDX_HEADS, IDX_DIM).transpose(0, 2, 1, 3).reshape(bs, IDX_HEADS * ts, IDX_DIM)
    w = iwt[:IDX_HEADS].reshape(IDX_HEADS, bs, ts).transpose(1, 0, 2).reshape(bs, IDX_HEADS * ts, 1)
    w = jnp.broadcast_to(w, (bs, IDX_HEADS * ts, LANES))
    sc, scn = _samp_scores(page_table, cache_kidx, layer, qi, w, pad_rows(ik))
    sc_t = jnp.concatenate([sc, scn], axis=1).T
    qpos = (past + jnp.arange(n, dtype=I32) % ts).reshape(1, n)
    bias_all = _select(sc_t, qpos, topk).T
    eye = jnp.eye(ATT_HEADS, dtype=BF16)
    q4 = aqb.reshape(bs, ts, ATT_HEADS, LANES)[..., :HEAD_DIM]
    qbd = (eye[None, :, None, :, None] * q4.transpose(0, 2, 1, 3)[:, :, :, None, :]).reshape(
        bs, ATT_HEADS * ts, D_ATT)
    return _samp_attention(page_table, cache_k4, cache_v4, layer, qbd, bias_all[:, :past],
                           bias_all[:, past:], pad_rows(ak), pad_rows(av), ag)


def _mixer_layer(x, pos, h0r, h0i, s0, attend, g, w_in_l, w_out_l, ssm_prm, ret_g, gf, final, tm, tl):
    bv, tv, _ = x.shape
    n = bv * tv
    tm = min(tm, n)
    wm = w_in_l[:, :C_IW].astype(BF16)
    wvt = w_in_l[:, C_AV:C_AG].T.astype(BF16)
    wwt = jnp.pad(w_in_l[:, C_IW:C_END].T, ((0, SUBLANES - IDX_HEADS), (0, 0))).astype(BF16)
    slots = lambda w: jnp.pad(w.reshape(D_MODEL, ATT_HEADS, HEAD_DIM),
                              ((0, 0), (0, 0), (0, LANES - HEAD_DIM))).reshape(D_MODEL, ATT_HEADS * LANES)
    wqp = slots(w_in_l[:, C_AQ:C_AK]).astype(BF16)
    wkp = slots(w_in_l[:, C_AK:C_AV]).astype(BF16)
    xf = x.reshape(n, D_MODEL)
    (u2, r4, ak, av, ik, ag, aqb, akb, avtb, iqb, ikb, iwt) = _inproj(
        xf, g.astype(F32).reshape(1, D_MODEL), wm, wvt, wwt, wqp, wkp, tm)
    y_ssm, hr, hi = _ssm(u2.reshape(bv, tv, -1), h0r, h0i, ssm_prm, tl)
    y_ret, s_new = _ret(r4.reshape(bv, tv, -1), s0, pos, ret_g)
    y_att = attend(ak, av, ik, ag, aqb, akb, avtb, iqb, ikb, iwt)
    y = _outproj(xf, y_ssm.reshape(n, D_SSM), y_ret.reshape(n, D_RET), y_att,
                 w_out_l.astype(BF16), gf.astype(F32).reshape(1, D_MODEL), tm, final)
    return y.reshape(bv, tv, D_MODEL), (ak.reshape(bv, tv, -1), av.reshape(bv, tv, -1),
                                        ik.reshape(bv, tv, -1), hr, hi, s_new)


def kernel(x_prompt, x_sample, cache_k, cache_v, cache_kidx, state_ssm_re, state_ssm_im, state_ret,
           page_table, norm_g, w_in, w_out, ssm_lambda_re, ssm_lambda_im, ssm_b_re, ssm_b_im,
           ssm_c_re, ssm_c_im, ssm_d, ssm_log_step, ssm_w_glu, ssm_b_glu, ret_norm_g, final_norm_g):
    bp, tp, _ = x_prompt.shape
    bs, ts, _ = x_sample.shape
    depth = w_in.shape[0]
    past = page_table.shape[1] * PAGE_SIZE
    assert ts == SUBLANES and tp % Q_BLOCK == 0
    pos_p = jnp.arange(tp)
    pos_s = past + jnp.arange(ts)
    cache_k4 = cache_k.reshape(cache_k.shape[:3] + (D_ATT,))
    cache_v4 = cache_v.reshape(cache_v.shape[:3] + (D_ATT,))
    zero_h = jnp.zeros((bp, 1, N_STATE), F32)
    zero_s = jnp.zeros((bp, RET_HEADS, HEAD_DIM, HEAD_DIM), F32)
    hp, hs = x_prompt, x_sample
    outs_p, outs_s = [], []
    for l in range(depth):
        final = l == depth - 1
        prm = _ssm_params(ssm_lambda_re[l], ssm_lambda_im[l], ssm_b_re[l], ssm_b_im[l], ssm_c_re[l],
                          ssm_c_im[l], ssm_d[l], ssm_log_step[l], ssm_w_glu[l], ssm_b_glu[l])

        def attend_p(ak, av, ik, ag, aqb, akb, avtb, iqb, ikb, iwt):
            return _dsa_prompt(bp, tp, ikb, akb, avtb, iqb, iwt, aqb, ag)

        hp, st_p = _mixer_layer(hp, pos_p, zero_h, zero_h, zero_s, attend_p, norm_g[l], w_in[l], w_out[l],
                                prm, ret_norm_g[l], final_norm_g, final, tm=256, tl=min(512, tp))

        def attend_s(ak, av, ik, ag, aqb, akb, avtb, iqb, ikb, iwt):
            return _dsa_sample(page_table, cache_k4, cache_v4, cache_kidx, l, ts, ak, av, ik, ag,
                               aqb, iqb, iwt)

        hs, st_s = _mixer_layer(hs, pos_s, state_ssm_re[l].reshape(bs, 1, N_STATE),
                                state_ssm_im[l].reshape(bs, 1, N_STATE), state_ret[l], attend_s,
                                norm_g[l], w_in[l], w_out[l], prm, ret_norm_g[l], final_norm_g, final,
                                tm=256, tl=ts)
        outs_p.append(st_p)
        outs_s.append(st_s)

    def pack(outs, b, t):
        k = jnp.stack([o[0] for o in outs]).reshape(depth, b, t, ATT_HEADS, HEAD_DIM)
        v = jnp.stack([o[1] for o in outs]).reshape(depth, b, t, ATT_HEADS, HEAD_DIM)
        ki = jnp.stack([o[2] for o in outs])
        hr = jnp.stack([o[3] for o in outs]).reshape(depth, b, SSM_GROUPS, SSM_STATE)
        hi = jnp.stack([o[4] for o in outs]).reshape(depth, b, SSM_GROUPS, SSM_STATE)
        s = jnp.stack([o[5] for o in outs])
        return k, v, ki, hr, hi, s

    return (hp, hs) + pack(outs_p, bp, tp) + pack(outs_s, bs, ts)
```

```python
import functools
import math

import jax
import jax.numpy as jnp
import numpy as np
from jax import lax
from jax.experimental import pallas as pl
from jax.experimental.pallas import tpu as pltpu

F32 = jnp.float32
BF16 = jnp.bfloat16
I32 = jnp.int32

D_MODEL = 1024
HEAD_DIM = 64
SSM_GROUP = 16
D_SSM = 256
SSM_GROUPS = D_SSM // SSM_GROUP
SSM_STATE = 64
N_STATE = SSM_GROUPS * SSM_STATE
D_RET = 384
RET_HEADS = D_RET // HEAD_DIM
RET_CHUNK = 128
ROPE_BASE = 10000.0
D_ATT = 384
ATT_HEADS = D_ATT // HEAD_DIM
IDX_HEADS = 4
IDX_DIM = 64
TOPK_MAX = 256
Q_BLOCK = 128
PAGE_SIZE = 128
NORM_EPS = 1e-6
NEG_INF = -1e30

_SIZES = (D_SSM, D_SSM, D_RET, D_RET, D_RET, D_RET, D_ATT, D_ATT, D_ATT, D_ATT,
          IDX_HEADS * IDX_DIM, IDX_DIM, IDX_HEADS)
_OFF = np.concatenate([[0], np.cumsum(_SIZES)]).tolist()
C_U, C_RET, C_AQ, C_AK, C_AV, C_AG, C_IQ, C_IK, C_IW, C_END = (
    _OFF[0], _OFF[2], _OFF[6], _OFF[7], _OFF[8], _OFF[9], _OFF[10], _OFF[11], _OFF[12], _OFF[13])

LANES = 128
SUBLANES = 8
F32_INF_BITS = 0x7F800000
VMEM_LIMIT = 56 * 1024 * 1024

_NT = (((1,), (1,)), ((), ()))


def _cparams(n_axes):
    return pltpu.CompilerParams(dimension_semantics=("arbitrary",) * n_axes,
                                vmem_limit_bytes=VMEM_LIMIT)


_T_AK, _T_AV, _T_IK, _T_IW, _T_END = 0, D_ATT, 2 * D_ATT, 2 * D_ATT + IDX_DIM, 2 * D_ATT + IDX_DIM + SUBLANES


def _inproj_kernel(x_ref, g_ref, wm_ref, wt_ref, wqp_ref, wkp_ref,
                   u2_ref, r4_ref, ag_ref, akt_ref, avt_ref, ikt_ref,
                   aqb_ref, akb_ref, avtb_ref, iqb_ref, ikb_ref, iwt_ref):
    x = x_ref[...]
    ms = jnp.mean(x * x, axis=-1, keepdims=True)
    xn = (x * lax.rsqrt(ms + NORM_EPS) * g_ref[...]).astype(BF16)

    def seg(a, b):
        return jnp.dot(xn, wm_ref[:, a:b], preferred_element_type=F32)

    u2_ref[...] = seg(C_U, C_RET)
    r4_ref[...] = seg(C_RET, C_AQ)
    aqb_ref[...] = (jnp.dot(xn, wqp_ref[...], preferred_element_type=F32) * HEAD_DIM ** -0.5).astype(BF16)
    akb_ref[...] = jnp.dot(xn, wkp_ref[...], preferred_element_type=F32).astype(BF16)
    ag_ref[...] = seg(C_AG, C_IQ)
    iqb_ref[...] = seg(C_IQ, C_IK).astype(BF16)
    ikb_ref[...] = seg(C_IK, C_IW).astype(BF16)
    zt = lax.dot_general(wt_ref[...], xn, _NT, preferred_element_type=F32)
    akt_ref[...] = zt[_T_AK:_T_AV]
    avt = zt[_T_AV:_T_IK]
    avt_ref[...] = avt
    avtb_ref[...] = avt.astype(BF16)
    ikt_ref[...] = zt[_T_IK:_T_IW]
    iwt_ref[...] = zt[_T_IW:_T_END] * (IDX_HEADS ** -0.5 * IDX_DIM ** -0.5)


def _inproj(x, bv, tv, g, wm, wt, wqp, wkp, tm):
    n = x.shape[0]
    nt = tv // tm
    row = lambda w: pl.BlockSpec((tm, w), lambda i: (i, 0))
    col = lambda h: pl.BlockSpec((None, h, tm), lambda i: (i // nt, 0, i % nt))
    full = lambda a: pl.BlockSpec(a.shape, lambda i: (0,) * a.ndim)
    out_shape = (
        jax.ShapeDtypeStruct((n, C_RET - C_U), F32),
        jax.ShapeDtypeStruct((n, C_AQ - C_RET), F32),
        jax.ShapeDtypeStruct((n, D_ATT), F32),
        jax.ShapeDtypeStruct((bv, D_ATT, tv), F32),
        jax.ShapeDtypeStruct((bv, D_ATT, tv), F32),
        jax.ShapeDtypeStruct((bv, IDX_DIM, tv), F32),
        jax.ShapeDtypeStruct((n, ATT_HEADS * LANES), BF16),
        jax.ShapeDtypeStruct((n, ATT_HEADS * LANES), BF16),
        jax.ShapeDtypeStruct((bv, D_ATT, tv), BF16),
        jax.ShapeDtypeStruct((n, IDX_HEADS * IDX_DIM), BF16),
        jax.ShapeDtypeStruct((n, IDX_DIM), BF16),
        jax.ShapeDtypeStruct((bv, SUBLANES, tv), F32),
    )
    out_specs = (
        row(C_RET - C_U), row(C_AQ - C_RET), row(D_ATT), col(D_ATT), col(D_ATT), col(IDX_DIM),
        row(ATT_HEADS * LANES), row(ATT_HEADS * LANES), col(D_ATT), row(IDX_HEADS * IDX_DIM), row(IDX_DIM),
        col(SUBLANES))
    return pl.pallas_call(
        _inproj_kernel,
        grid=(n // tm,),
        in_specs=[row(D_MODEL), full(g), full(wm), full(wt), full(wqp), full(wkp)],
        out_specs=out_specs,
        out_shape=out_shape,
        compiler_params=_cparams(1),
        name="inproj",
    )(x, g, wm, wt, wqp, wkp)


def _outproj_kernel(x_ref, ys_ref, yr_ref, ya_ref, wo_ref, gf_ref, o_ref, *, final):
    y = x_ref[...]
    y = y + jnp.dot(ys_ref[...], wo_ref[0:D_SSM, :], preferred_element_type=F32)
    y = y + jnp.dot(yr_ref[...], wo_ref[D_SSM:D_SSM + D_RET, :], preferred_element_type=F32)
    y = y + jnp.dot(ya_ref[...], wo_ref[D_SSM + D_RET:, :], preferred_element_type=F32)
    if final:
        ms = jnp.mean(y * y, axis=-1, keepdims=True)
        y = y * lax.rsqrt(ms + NORM_EPS) * gf_ref[...]
    o_ref[...] = y


def _outproj(x, ys, yr, ya, wo, gf, tm, final):
    n = x.shape[0]
    row = lambda w: pl.BlockSpec((tm, w), lambda i: (i, 0))
    full = lambda a: pl.BlockSpec(a.shape, lambda i: (0,) * a.ndim)
    return pl.pallas_call(
        functools.partial(_outproj_kernel, final=final),
        grid=(n // tm,),
        in_specs=[row(D_MODEL), row(D_SSM), row(D_RET), row(D_ATT), full(wo), full(gf)],
        out_specs=row(D_MODEL),
        out_shape=jax.ShapeDtypeStruct((n, D_MODEL), F32),
        compiler_params=_cparams(1),
        name="outproj",
    )(x, ys, yr, ya, wo, gf)


def _cmul(ar, ai, br, bi):
    return ar * br - ai * bi, ar * bi + ai * br


def _ssm_kernel(u2_ref, h0r_ref, h0i_ref, lr_ref, li_ref, ls_ref, btr_ref, bti_ref,
                ctr_ref, cti_ref, d_ref, wg_ref, bg_ref,
                y_ref, hr_out_ref, hi_out_ref,
                bbr_ref, bbi_ref, apr_ref, api_ref, cr_ref, ci_ref, hr_ref, hi_ref, *, tl):
    b = pl.program_id(0)
    c = pl.program_id(1)
    nc = pl.num_programs(1)

    @pl.when((b == 0) & (c == 0))
    def _prep():
        lr, li = lr_ref[...], li_ref[...]
        dt = jnp.exp(ls_ref[...])
        mag = jnp.exp(lr * dt)
        abr, abi = mag * jnp.cos(li * dt), mag * jnp.sin(li * dt)
        den = lr * lr + li * li
        nr, ni = abr - 1.0, abi
        fr = (nr * lr + ni * li) / den
        fi = (ni * lr - nr * li) / den
        bbr_ref[...] = (fr * btr_ref[...] - fi * bti_ref[...]).astype(BF16)
        bbi_ref[...] = (fr * bti_ref[...] + fi * btr_ref[...]).astype(BF16)
        pr, pi = abr, abi
        rows_r, rows_i = [pr], [pi]
        for _ in range(SUBLANES - 1):
            pr, pi = _cmul(pr, pi, abr, abi)
            rows_r.append(pr)
            rows_i.append(pi)
        apr_ref[...] = jnp.concatenate(rows_r, axis=0)
        api_ref[...] = jnp.concatenate(rows_i, axis=0)

    @pl.when(c == 0)
    def _init():
        cr_ref[...] = h0r_ref[...]
        ci_ref[...] = h0i_ref[...]

    u = u2_ref[:, 0:D_SSM]
    gate = u2_ref[:, D_SSM:2 * D_SSM]
    ub = u.astype(BF16)
    hr_ref[...] = jnp.dot(ub, bbr_ref[...], preferred_element_type=F32)
    hi_ref[...] = jnp.dot(ub, bbi_ref[...], preferred_element_type=F32)

    apr, api = apr_ref[...], api_ref[...]
    rowid = lax.broadcasted_iota(I32, (SUBLANES, N_STATE), 0)

    def blk(j, carry):
        cr, ci = carry
        r0 = pl.multiple_of(j * SUBLANES, SUBLANES)
        br = hr_ref[pl.ds(r0, SUBLANES), :]
        bi = hi_ref[pl.ds(r0, SUBLANES), :]
        for k in (1, 2, 4):
            keep = rowid >= k
            sr = jnp.where(keep, pltpu.roll(br, k, 0), 0.0)
            si = jnp.where(keep, pltpu.roll(bi, k, 0), 0.0)
            akr, aki = apr[k - 1:k, :], api[k - 1:k, :]
            br, bi = br + (akr * sr - aki * si), bi + (akr * si + aki * sr)
        hr = br + (apr * cr - api * ci)
        hi = bi + (apr * ci + api * cr)
        hr_ref[pl.ds(r0, SUBLANES), :] = hr
        hi_ref[pl.ds(r0, SUBLANES), :] = hi
        return hr[SUBLANES - 1:SUBLANES, :], hi[SUBLANES - 1:SUBLANES, :]

    cr, ci = lax.fori_loop(0, tl // SUBLANES, blk, (cr_ref[...], ci_ref[...]))
    cr_ref[...] = cr
    ci_ref[...] = ci

    @pl.when(c == nc - 1)
    def _fin():
        hr_out_ref[...] = cr
        hi_out_ref[...] = ci

    y = (jnp.dot(hr_ref[...].astype(BF16), ctr_ref[...], preferred_element_type=F32)
         - jnp.dot(hi_ref[...].astype(BF16), cti_ref[...], preferred_element_type=F32)
         + d_ref[...] * u)
    g = jax.nn.gelu(y)
    y = g * jax.nn.sigmoid(jnp.dot(g.astype(BF16), wg_ref[...], preferred_element_type=F32) + bg_ref[...])
    y_ref[...] = (y * (gate * jax.nn.sigmoid(gate))).astype(BF16)


def _ssm(u2, h0r, h0i, prm, tl):
    bv, tv, _ = u2.shape
    full = lambda a: pl.BlockSpec(a.shape, lambda b, c: (0,) * a.ndim)
    st = pl.BlockSpec((None, 1, N_STATE), lambda b, c: (b, 0, 0))
    return pl.pallas_call(
        functools.partial(_ssm_kernel, tl=tl),
        grid=(bv, tv // tl),
        in_specs=[pl.BlockSpec((None, tl, 2 * D_SSM), lambda b, c: (b, c, 0)), st, st]
        + [full(a) for a in prm],
        out_specs=(pl.BlockSpec((None, tl, D_SSM), lambda b, c: (b, c, 0)), st, st),
        out_shape=(jax.ShapeDtypeStruct((bv, tv, D_SSM), BF16),
                   jax.ShapeDtypeStruct((bv, 1, N_STATE), F32),
                   jax.ShapeDtypeStruct((bv, 1, N_STATE), F32)),
        scratch_shapes=[pltpu.VMEM((D_SSM, N_STATE), BF16), pltpu.VMEM((D_SSM, N_STATE), BF16),
                        pltpu.VMEM((SUBLANES, N_STATE), F32), pltpu.VMEM((SUBLANES, N_STATE), F32),
                        pltpu.VMEM((1, N_STATE), F32), pltpu.VMEM((1, N_STATE), F32),
                        pltpu.VMEM((tl, N_STATE), F32), pltpu.VMEM((tl, N_STATE), F32)],
        compiler_params=_cparams(2),
        name="ssm",
    )(u2, h0r, h0i, *prm)


def _ssm_params(lam_re, lam_im, b_re, b_im, c_re, c_im, d_skip, log_step, w_glu, b_glu):
    eye = jnp.eye(SSM_GROUPS, dtype=F32)

    def bdiag(m):
        g, r, c = m.shape
        return (eye[:, None, :, None] * m[:, :, None, :]).reshape(g * r, g * c)

    flat = lambda a: a.astype(F32).reshape(1, N_STATE)
    ls = jnp.broadcast_to(log_step.astype(F32)[:, None], (SSM_GROUPS, SSM_STATE))
    btr = bdiag(jnp.swapaxes(b_re.astype(F32), 1, 2))
    bti = bdiag(jnp.swapaxes(b_im.astype(F32), 1, 2))
    ctr = bdiag(jnp.swapaxes(c_re.astype(F32), 1, 2)).astype(BF16)
    cti = bdiag(jnp.swapaxes(c_im.astype(F32), 1, 2)).astype(BF16)
    return (flat(lam_re), flat(lam_im), flat(ls), btr, bti, ctr, cti,
            d_skip.astype(F32).reshape(1, D_SSM), w_glu.astype(BF16),
            b_glu.astype(F32).reshape(1, D_SSM))


def _swap_halves(x):
    lane = lax.broadcasted_iota(I32, (x.shape[0], LANES), 1)
    first = (lane % HEAD_DIM) < (HEAD_DIM // 2)
    tiles = []
    for t in range(x.shape[1] // LANES):
        xt = x[:, t * LANES:(t + 1) * LANES]
        up = pltpu.roll(xt, LANES - HEAD_DIM // 2, 1)
        dn = pltpu.roll(xt, HEAD_DIM // 2, 1)
        tiles.append(jnp.where(first, up, dn))
    return jnp.concatenate(tiles, axis=1)


def _ret_kernel(r4_ref, s0_ref, cos_ref, sin_ref, din_ref, dq_ref, dk_ref, ds_ref, g_ref,
                y_ref, s_out_ref, s_ref):
    c = pl.program_id(1)
    nc = pl.num_programs(1)

    @pl.when(c == 0)
    def _init():
        s_ref[...] = s0_ref[...]

    q = r4_ref[:, 0:D_RET]
    k = r4_ref[:, D_RET:2 * D_RET]
    v = r4_ref[:, 2 * D_RET:3 * D_RET]
    gate = r4_ref[:, 3 * D_RET:4 * D_RET]
    cos = jnp.concatenate([cos_ref[...]] * (D_RET // LANES), axis=1)
    sin = jnp.concatenate([sin_ref[...]] * (D_RET // LANES), axis=1)
    qr = q * cos + _swap_halves(q) * sin
    kr = (k * cos + _swap_halves(k) * sin) * HEAD_DIM ** -0.5
    qb = qr.astype(BF16)
    kb = kr.astype(BF16)
    vb = v.astype(BF16)
    kdt = (kr * dk_ref[...]).T.astype(BF16)
    dq = dq_ref[...]
    gsil = gate * jax.nn.sigmoid(gate) * g_ref[...]
    for h in range(RET_HEADS):
        sl = slice(h * HEAD_DIM, (h + 1) * HEAD_DIM)
        qh, kh, vh = qb[:, sl], kb[:, sl], vb[:, sl]
        s = s_ref[h]
        att = lax.dot_general(qh, kh, _NT, preferred_element_type=F32) * din_ref[h]
        inner = jnp.dot(att.astype(BF16), vh, preferred_element_type=F32)
        cross = jnp.dot(qh, s.astype(BF16), preferred_element_type=F32) * dq[:, sl]
        s_ref[h] = s * ds_ref[h] + jnp.dot(kdt[sl, :], vh, preferred_element_type=F32)
        o = inner + cross
        mu = jnp.mean(o, axis=-1, keepdims=True)
        var = jnp.mean(jnp.square(o - mu), axis=-1, keepdims=True)
        y_ref[:, sl] = ((o - mu) * lax.rsqrt(var + NORM_EPS) * gsil[:, sl]).astype(BF16)

    @pl.when(c == nc - 1)
    def _fin():
        s_out_ref[...] = s_ref[...]


def _ret_tables(pos, chunk):
    half = HEAD_DIM // 2
    inv = ROPE_BASE ** (-jnp.arange(half, dtype=F32) / half)
    ang = pos.astype(F32)[:, None] * inv[None, :]
    cos, sin = jnp.cos(ang), jnp.sin(ang)
    cos_t = jnp.concatenate([cos, cos] * (LANES // HEAD_DIM), axis=1)
    sin_t = jnp.concatenate([-sin, sin] * (LANES // HEAD_DIM), axis=1)
    log_g = jnp.log1p(-jnp.exp2(-5.0 - jnp.arange(RET_HEADS, dtype=F32)))
    j = jnp.arange(chunk, dtype=F32)
    rel = j[:, None] - j[None, :]
    din = jnp.where(rel[None] >= 0, jnp.exp(log_g[:, None, None] * jnp.maximum(rel, 0.0)[None]), 0.0)
    dq = jnp.repeat(jnp.exp(log_g[None, :] * (j[:, None] + 1.0)), HEAD_DIM, axis=1)
    dk = jnp.repeat(jnp.exp(log_g[None, :] * (chunk - 1.0 - j[:, None])), HEAD_DIM, axis=1)
    ds = jnp.broadcast_to(jnp.exp(log_g * chunk)[:, None, None], (RET_HEADS, HEAD_DIM, HEAD_DIM))
    return cos_t, sin_t, din, dq, dk, ds


def _ret(r4, s0, pos, norm_g):
    bv, tv, _ = r4.shape
    chunk = RET_CHUNK if tv % RET_CHUNK == 0 else tv
    cos_t, sin_t, din, dq, dk, ds = _ret_tables(pos, chunk)
    g = norm_g.astype(F32).reshape(1, D_RET)
    full = lambda a: pl.BlockSpec(a.shape, lambda b, c: (0,) * a.ndim)
    st = pl.BlockSpec((None, RET_HEADS, HEAD_DIM, HEAD_DIM), lambda b, c: (b, 0, 0, 0))
    tab = pl.BlockSpec((chunk, LANES), lambda b, c: (c, 0))
    return pl.pallas_call(
        _ret_kernel,
        grid=(bv, tv // chunk),
        in_specs=[pl.BlockSpec((None, chunk, 4 * D_RET), lambda b, c: (b, c, 0)), st, tab, tab,
                  full(din), full(dq), full(dk), full(ds), full(g)],
        out_specs=(pl.BlockSpec((None, chunk, D_RET), lambda b, c: (b, c, 0)), st),
        out_shape=(jax.ShapeDtypeStruct((bv, tv, D_RET), BF16),
                   jax.ShapeDtypeStruct((bv, RET_HEADS, HEAD_DIM, HEAD_DIM), F32)),
        scratch_shapes=[pltpu.VMEM((RET_HEADS, HEAD_DIM, HEAD_DIM), F32)],
        compiler_params=_cparams(2),
        name="retention",
    )(r4, s0, cos_t, sin_t, din, dq, dk, ds, g)


def _count(sc_ref, nk, kb, pred):
    n_acc = 4

    def body(c, cnts):
        r0 = pl.multiple_of(c * kb, kb)
        cnts = list(cnts)
        for j in range(kb // SUBLANES):
            blk = sc_ref[pl.ds(r0 + j * SUBLANES, SUBLANES), :]
            cnts[j % n_acc] = cnts[j % n_acc] + jnp.where(pred(blk), 1, 0)
        return tuple(cnts)
    zero = jnp.zeros((SUBLANES, LANES), I32)
    cnts = lax.fori_loop(0, nk, body, (zero,) * n_acc)
    return jnp.sum((cnts[0] + cnts[1]) + (cnts[2] + cnts[3]), axis=0, keepdims=True)


def _select_bias(sc_ref, bias_ref, tri_ref, nk, kb, n_beyond, qpos, topk):
    def count_ge(t):
        return _count(sc_ref, nk, kb, lambda blk: blk >= t) + jnp.where(t <= NEG_INF, n_beyond, 0)

    pos = jnp.where(count_ge(jnp.zeros((1, LANES), F32)) >= topk, 1, 0)
    sign = jnp.where(pos == 1, 0, np.int32(-2 ** 31))

    def bit_step(i, cur):
        cand = cur | lax.shift_left(jnp.int32(1), 30 - i)
        t = lax.bitcast_convert_type(cand | sign, F32)
        ok = jnp.where(count_ge(t) >= topk, 1, 0)
        finite = jnp.where(cand <= F32_INF_BITS, 1, 0)
        keep = jnp.where(pos == 1, ok, (1 - ok) * finite)
        return jnp.where(keep == 1, cand, cur)

    cur = lax.fori_loop(0, 31, bit_step, jnp.zeros((1, LANES), I32))
    mag = jnp.where(pos == 1, cur, cur + 1)
    tau = lax.bitcast_convert_type(mag | sign, F32)
    surplus_ties = jnp.max(count_ge(tau)) > topk

    @pl.when(jnp.logical_not(surplus_ties))
    def _no_surplus():
        def body(c, _):
            r0 = pl.multiple_of(c * kb, kb)
            blk = sc_ref[pl.ds(r0, kb), :]
            kidx = r0 + lax.broadcasted_iota(I32, (kb, LANES), 0)
            bias_ref[pl.ds(r0, kb), :] = jnp.where((blk >= tau) & (kidx <= qpos), 0.0, NEG_INF)
            return 0
        lax.fori_loop(0, nk, body, 0)

    @pl.when(surplus_ties)
    def _ties_by_index():
        c_gt = _count(sc_ref, nk, kb, lambda blk: blk > tau) + jnp.where(NEG_INF > tau, n_beyond, 0)
        need = (topk - c_gt).astype(F32)

        def body(c, carry):
            r0 = pl.multiple_of(c * kb, kb)
            blk = sc_ref[pl.ds(r0, kb), :]
            eq = blk == tau
            pref = jnp.dot(tri_ref[...], jnp.where(eq, 1.0, 0.0).astype(BF16),
                           preferred_element_type=F32) + carry
            kidx = r0 + lax.broadcasted_iota(I32, (kb, LANES), 0)
            sel = ((blk > tau) | (eq & (pref <= need))) & (kidx <= qpos)
            bias_ref[pl.ds(r0, kb), :] = jnp.where(sel, 0.0, NEG_INF)
            return pref[kb - 1:kb, :]

        lax.fori_loop(0, nk, body, jnp.zeros((1, LANES), F32))


def _tri(kb):
    r = np.arange(kb)
    return jnp.asarray((r[None, :] <= r[:, None]).astype(np.float32), dtype=BF16)


def _dsa_prompt_kernel(ikb_ref, akb_ref, avtb_ref, iqb_ref, iwt_ref, aqb_ref, ag_ref, tri_ref,
                       y_ref, sc_ref, bias_ref, lg_ref, *o_refs, kb, ka, t_total, topk):
    i = pl.program_id(1)
    q0 = i * Q_BLOCK
    nk = (q0 + Q_BLOCK + kb - 1) // kb
    qpos = q0 + lax.broadcasted_iota(I32, (1, LANES), 1)
    iq = iqb_ref[...]
    iw = iwt_ref[...]
    qcat = jnp.concatenate([iq[:, h * IDX_DIM:(h + 1) * IDX_DIM] for h in range(IDX_HEADS)], axis=0)

    def score_body(c, _):
        r0 = pl.multiple_of(c * kb, kb)
        keys = ikb_ref[pl.ds(r0, kb), :]
        s = lax.dot_general(keys, qcat, _NT, preferred_element_type=F32)
        parts = [jnp.maximum(s[:, h * LANES:(h + 1) * LANES], 0.0) * iw[h:h + 1, :]
                 for h in range(IDX_HEADS)]
        acc = (parts[0] + parts[1]) + (parts[2] + parts[3])
        kidx = r0 + lax.broadcasted_iota(I32, (kb, LANES), 0)
        sc_ref[pl.ds(r0, kb), :] = jnp.where(kidx <= qpos, acc, NEG_INF)
        return 0

    lax.fori_loop(0, nk, score_body, 0)
    _select_bias(sc_ref, bias_ref, tri_ref, nk, kb, t_total - nk * kb, qpos, topk)

    aq = aqb_ref[...]
    qhs = [aq[:, h * LANES:(h + 1) * LANES] for h in range(ATT_HEADS)]
    for o_ref in o_refs:
        o_ref[...] = jnp.zeros(o_ref.shape, F32)

    def att_body(c, carry):
        ms, ls = carry
        r0 = pl.multiple_of(c * ka, ka)
        bias = bias_ref[pl.ds(r0, ka), :]
        ms_new, ls_new = [], []
        mx = []
        for h in range(ATT_HEADS):
            kh = akb_ref[pl.ds(r0, ka), h * LANES:(h + 1) * LANES]
            logit = lax.dot_general(kh, qhs[h], _NT, preferred_element_type=F32) + bias
            lg_ref[h] = logit
            mx.append(jnp.max(logit, axis=0, keepdims=True))
        for h in range(ATT_HEADS):
            m_new = jnp.maximum(ms[h], mx[h])
            alpha = jnp.exp(ms[h] - m_new)
            p = jnp.exp(lg_ref[h] - m_new)
            ls_new.append(alpha * ls[h] + jnp.sum(p, axis=0, keepdims=True))
            ms_new.append(m_new)
            vt = avtb_ref[h * HEAD_DIM:(h + 1) * HEAD_DIM, pl.ds(r0, ka)]
            o_refs[h][...] = alpha * o_refs[h][...] + jnp.dot(vt, p.astype(BF16), preferred_element_type=F32)
        return tuple(ms_new), tuple(ls_new)

    m0 = (jnp.full((1, LANES), NEG_INF, F32),) * ATT_HEADS
    l0 = (jnp.zeros((1, LANES), F32),) * ATT_HEADS
    _, ls = lax.fori_loop(0, nk * (kb // ka), att_body, (m0, l0))
    ot = jnp.concatenate([o_refs[h][...] / ls[h] for h in range(ATT_HEADS)], axis=0)
    gate = ag_ref[...]
    y_ref[...] = (ot.T * (gate * jax.nn.sigmoid(gate))).astype(BF16)


def _dsa_prompt(bv, tv, ikb, akb, avtb, iqb, iwt, aqb, ag, kb=512, ka=512):
    topk = min(TOPK_MAX, tv // 4)
    kb = min(kb, tv)
    nq = tv // Q_BLOCK
    tri = _tri(kb)
    per_b = lambda w: pl.BlockSpec((tv, w), lambda b, i: (b, 0))
    qrow = lambda w: pl.BlockSpec((Q_BLOCK, w), lambda b, i: (b * nq + i, 0))
    return pl.pallas_call(
        functools.partial(_dsa_prompt_kernel, kb=kb, ka=min(ka, kb), t_total=tv, topk=topk),
        grid=(bv, nq),
        in_specs=[per_b(IDX_DIM), per_b(ATT_HEADS * LANES),
                  pl.BlockSpec((None, D_ATT, tv), lambda b, i: (b, 0, 0)),
                  qrow(IDX_HEADS * IDX_DIM),
                  pl.BlockSpec((None, SUBLANES, Q_BLOCK), lambda b, i: (b, 0, i)),
                  qrow(ATT_HEADS * LANES), qrow(D_ATT),
                  pl.BlockSpec(tri.shape, lambda b, i: (0, 0))],
        out_specs=qrow(D_ATT),
        out_shape=jax.ShapeDtypeStruct((bv * tv, D_ATT), BF16),
        scratch_shapes=[pltpu.VMEM((tv, LANES), F32), pltpu.VMEM((tv, LANES), F32)]
        + [pltpu.VMEM((ATT_HEADS, min(ka, kb), LANES), F32)]
        + [pltpu.VMEM((HEAD_DIM, LANES), F32)] * ATT_HEADS,
        compiler_params=_cparams(2),
        name="dsa_prompt",
    )(ikb, akb, avtb, iqb, iwt, aqb, ag, tri)


def _samp_score_kernel(pt_ref, *refs, npg, ts):
    del pt_ref
    pages = refs[:npg]
    qi_ref, w_ref, iknew_ref, sc_ref, scn_ref = refs[npg:]
    qi = qi_ref[...]
    w = w_ref[...]

    def score(keys_t):
        s = jnp.dot(qi, keys_t.astype(BF16), preferred_element_type=F32)
        r = jnp.maximum(s, 0.0) * w
        return (r[0:ts] + r[ts:2 * ts]) + (r[2 * ts:3 * ts] + r[3 * ts:4 * ts])

    for i in range(npg):
        sc_ref[:, i * PAGE_SIZE:(i + 1) * PAGE_SIZE] = score(pages[i][...])

    @pl.when(pl.program_id(1) == 0)
    def _new_keys():
        sn = score(iknew_ref[...])
        col = lax.broadcasted_iota(I32, (ts, PAGE_SIZE), 1)
        row = lax.broadcasted_iota(I32, (ts, PAGE_SIZE), 0)
        scn_ref[...] = jnp.where(col <= row, sn, NEG_INF)


def _samp_scores(page_table, kidx_t, layer, qi, w, iknew_t, npg):
    bs, n_pages = page_table.shape
    ts = qi.shape[1] // IDX_HEADS
    page_spec = lambda i: pl.BlockSpec(
        (None, None, IDX_DIM, PAGE_SIZE), lambda b, j, pt, i=i: (layer, pt[b, j * npg + i], 0, 0))
    per_b = lambda a: pl.BlockSpec((None,) + a.shape[1:], lambda b, j, pt: (b, 0, 0))
    gs = pltpu.PrefetchScalarGridSpec(
        num_scalar_prefetch=1,
        grid=(bs, n_pages // npg),
        in_specs=[page_spec(i) for i in range(npg)] + [per_b(qi), per_b(w), per_b(iknew_t)],
        out_specs=(pl.BlockSpec((ts, npg * PAGE_SIZE), lambda b, j, pt: (b, j)),
                   pl.BlockSpec((ts, PAGE_SIZE), lambda b, j, pt: (b, 0))),
    )
    return pl.pallas_call(
        functools.partial(_samp_score_kernel, npg=npg, ts=ts),
        grid_spec=gs,
        out_shape=(jax.ShapeDtypeStruct((bs * ts, n_pages * PAGE_SIZE), F32),
                   jax.ShapeDtypeStruct((bs * ts, PAGE_SIZE), F32)),
        compiler_params=_cparams(2),
        name="sample_scores",
    )(page_table, *([kidx_t] * npg), qi, w, iknew_t)


def _select_kernel(sc_ref, qpos_ref, tri_ref, bias_ref, *, kb, nk, topk):
    _select_bias(sc_ref, bias_ref, tri_ref, nk, kb, 0, qpos_ref[...], topk)


def _select(sc_t, qpos, topk, kb):
    nkeys, nq = sc_t.shape
    tri = _tri(kb)
    col = pl.BlockSpec((nkeys, LANES), lambda i: (0, i))
    return pl.pallas_call(
        functools.partial(_select_kernel, kb=kb, nk=nkeys // kb, topk=topk),
        grid=(nq // LANES,),
        in_specs=[col, pl.BlockSpec((1, LANES), lambda i: (0, i)),
                  pl.BlockSpec(tri.shape, lambda i: (0, 0))],
        out_specs=col,
        out_shape=jax.ShapeDtypeStruct((nkeys, nq), F32),
        compiler_params=_cparams(1),
        name="sample_select",
    )(sc_t, qpos, tri)


def _samp_att_kernel(pt_ref, *refs, npg, ts):
    del pt_ref
    kp = refs[:npg]
    vp = refs[npg:2 * npg]
    q_ref, bias_ref, biasn_ref, kn_ref, vn_ref, ag_ref, y_ref, m_ref, l_ref, acc_ref = refs[2 * npg:]
    j = pl.program_id(1)
    nj = pl.num_programs(1)

    @pl.when(j == 0)
    def _init():
        m_ref[...] = jnp.full(m_ref.shape, NEG_INF, F32)
        l_ref[...] = jnp.zeros(l_ref.shape, F32)
        acc_ref[...] = jnp.zeros(acc_ref.shape, F32)

    q = q_ref[...]

    def update(kts, vts, bias):
        logit = jnp.concatenate(
            [jnp.dot(q, kt.astype(BF16), preferred_element_type=F32) for kt in kts], axis=1)
        logit = logit + jnp.concatenate([bias] * ATT_HEADS, axis=0)
        m_old = m_ref[...]
        m_new = jnp.maximum(m_old, jnp.max(logit, axis=1, keepdims=True))
        alpha = jnp.exp(m_old - m_new)
        p = jnp.exp(logit - m_new)
        l_ref[...] = alpha * l_ref[...] + jnp.sum(p, axis=1, keepdims=True)
        pb = p.astype(BF16)
        pvs = [lax.dot_general(pb[:, i * PAGE_SIZE:(i + 1) * PAGE_SIZE], vts[i].astype(BF16), _NT,
                               preferred_element_type=F32) for i in range(len(vts))]
        while len(pvs) > 1:
            pvs = [pvs[i] + pvs[i + 1] for i in range(0, len(pvs) - 1, 2)] + pvs[len(pvs) - len(pvs) % 2:]
        acc_ref[...] = alpha * acc_ref[...] + pvs[0]
        m_ref[...] = m_new

    update([r[...] for r in kp], [r[...] for r in vp], bias_ref[...])

    @pl.when(j == nj - 1)
    def _fin():
        update([kn_ref[...]], [vn_ref[...]], biasn_ref[...])
        o = acc_ref[...] / l_ref[...]
        gate = ag_ref[...]
        gsil = gate * jax.nn.sigmoid(gate)
        for h in range(ATT_HEADS):
            sl = slice(h * HEAD_DIM, (h + 1) * HEAD_DIM)
            y_ref[:, sl] = (o[h * ts:(h + 1) * ts, sl] * gsil[:, sl]).astype(BF16)


def _samp_attention(page_table, k_t, v_t, layer, qbd, bias, biasn, knew_t, vnew_t, ag, npg):
    bs, n_pages = page_table.shape
    ts = qbd.shape[1] // ATT_HEADS
    page_spec = lambda i: pl.BlockSpec(
        (None, None, D_ATT, PAGE_SIZE), lambda b, j, pt, i=i: (layer, pt[b, j * npg + i], 0, 0))
    per_b = lambda a: pl.BlockSpec((None,) + a.shape[1:], lambda b, j, pt: (b, 0, 0))
    gs = pltpu.PrefetchScalarGridSpec(
        num_scalar_prefetch=1,
        grid=(bs, n_pages // npg),
        in_specs=[page_spec(i) for i in range(npg)] * 2
        + [per_b(qbd),
           pl.BlockSpec((ts, npg * PAGE_SIZE), lambda b, j, pt: (b, j)),
           pl.BlockSpec((ts, PAGE_SIZE), lambda b, j, pt: (b, 0)),
           per_b(knew_t), per_b(vnew_t),
           pl.BlockSpec((ts, D_ATT), lambda b, j, pt: (b, 0))],
        out_specs=pl.BlockSpec((ts, D_ATT), lambda b, j, pt: (b, 0)),
        scratch_shapes=[pltpu.VMEM((ATT_HEADS * ts, 1), F32), pltpu.VMEM((ATT_HEADS * ts, 1), F32),
                        pltpu.VMEM((ATT_HEADS * ts, D_ATT), F32)],
    )
    return pl.pallas_call(
        functools.partial(_samp_att_kernel, npg=npg, ts=ts),
        grid_spec=gs,
        out_shape=jax.ShapeDtypeStruct((bs * ts, D_ATT), BF16),
        compiler_params=_cparams(2),
        name="sample_attention",
    )(page_table, *([k_t] * npg), *([v_t] * npg), qbd, bias, biasn, knew_t, vnew_t, ag)


def _dsa_sample(page_table, k_t, v_t, kidx_t, layer, ts, akt, avt, ikt, ag, aqb, iqb, iwt):
    bs, n_pages = page_table.shape
    past = n_pages * PAGE_SIZE
    n = bs * ts
    topk = min(TOPK_MAX, (past + ts) // 4)
    npg = math.gcd(n_pages, 16)

    def new_keys_t(a):
        a = a.reshape(a.shape[0], bs, ts).transpose(1, 0, 2)
        return jnp.pad(a, ((0, 0), (0, 0), (0, PAGE_SIZE - ts)))

    qi = iqb.reshape(bs, ts, IDX_HEADS, IDX_DIM).transpose(0, 2, 1, 3).reshape(bs, IDX_HEADS * ts, IDX_DIM)
    w = iwt[:IDX_HEADS].reshape(IDX_HEADS, bs, ts).transpose(1, 0, 2).reshape(bs, IDX_HEADS * ts, 1)
    w = jnp.broadcast_to(w, (bs, IDX_HEADS * ts, LANES))
    sc, scn = _samp_scores(page_table, kidx_t, layer, qi, w, new_keys_t(ikt), npg)
    sc_t = jnp.concatenate([sc, scn], axis=1).T
    qpos = (past + jnp.arange(n, dtype=I32) % ts).reshape(1, n)
    bias_all = _select(sc_t, qpos, topk, PAGE_SIZE).T
    eye = jnp.eye(ATT_HEADS, dtype=BF16)
    q4 = aqb.reshape(bs, ts, ATT_HEADS, LANES)[..., :HEAD_DIM]
    qbd = (eye[None, :, None, :, None] * q4.transpose(0, 2, 1, 3)[:, :, :, None, :]).reshape(
        bs, ATT_HEADS * ts, D_ATT)
    return _samp_attention(page_table, k_t, v_t, layer, qbd, bias_all[:, :past], bias_all[:, past:],
                           new_keys_t(akt), new_keys_t(avt), ag, npg)


def _mixer_layer(x, pos, h0r, h0i, s0, attend, g, w_in_l, w_out_l, ssm_prm, ret_g, gf, final, tm, tl):
    bv, tv, _ = x.shape
    n = bv * tv
    wm = w_in_l[:, :C_IW].astype(BF16)
    wt = jnp.concatenate([w_in_l[:, C_AK:C_AG].T, w_in_l[:, C_IK:C_END].T,
                          jnp.zeros((SUBLANES - IDX_HEADS, D_MODEL), w_in_l.dtype)], axis=0).astype(BF16)
    slots = lambda w: jnp.pad(w.reshape(D_MODEL, ATT_HEADS, HEAD_DIM),
                              ((0, 0), (0, 0), (0, LANES - HEAD_DIM))).reshape(D_MODEL, ATT_HEADS * LANES)
    wqp = slots(w_in_l[:, C_AQ:C_AK]).astype(BF16)
    wkp = slots(w_in_l[:, C_AK:C_AV]).astype(BF16)
    xf = x.reshape(n, D_MODEL)
    pb, pt = (bv, tv) if tv % tm == 0 else (1, n)
    tm = min(tm, pt)
    (u2, r4, ag, akt, avt, ikt, aqb, akb, avtb, iqb, ikb, iwt) = _inproj(
        xf, pb, pt, g.astype(F32).reshape(1, D_MODEL), wm, wt, wqp, wkp, tm)
    y_ssm, hr, hi = _ssm(u2.reshape(bv, tv, -1), h0r, h0i, ssm_prm, tl)
    y_ret, s_new = _ret(r4.reshape(bv, tv, -1), s0, pos, ret_g)
    y_att = attend(ag, akt, avt, ikt, aqb, akb, avtb, iqb, ikb, iwt)
    y = _outproj(xf, y_ssm.reshape(n, D_SSM), y_ret.reshape(n, D_RET), y_att,
                 w_out_l.astype(BF16), gf.astype(F32).reshape(1, D_MODEL), tm, final)
    tok = lambda a: a.reshape(pb, a.shape[1], -1, tv).transpose(0, 2, 3, 1).reshape(bv, tv, a.shape[1])
    return y.reshape(bv, tv, D_MODEL), (tok(akt), tok(avt), tok(ikt), hr, hi, s_new)


def kernel(x_prompt, x_sample, cache_k, cache_v, cache_kidx, state_ssm_re, state_ssm_im, state_ret,
           page_table, norm_g, w_in, w_out, ssm_lambda_re, ssm_lambda_im, ssm_b_re, ssm_b_im,
           ssm_c_re, ssm_c_im, ssm_d, ssm_log_step, ssm_w_glu, ssm_b_glu, ret_norm_g, final_norm_g):
    bp, tp, _ = x_prompt.shape
    bs, ts, _ = x_sample.shape
    depth = w_in.shape[0]
    past = page_table.shape[1] * PAGE_SIZE
    assert ts == SUBLANES and tp % Q_BLOCK == 0
    pos_p = jnp.arange(tp)
    pos_s = past + jnp.arange(ts)
    k_t = jnp.transpose(cache_k, (0, 1, 3, 4, 2)).reshape(cache_k.shape[:2] + (D_ATT, PAGE_SIZE))
    v_t = jnp.transpose(cache_v, (0, 1, 3, 4, 2)).reshape(cache_v.shape[:2] + (D_ATT, PAGE_SIZE))
    kidx_t = jnp.transpose(cache_kidx, (0, 1, 3, 2))
    zero_h = jnp.zeros((bp, 1, N_STATE), F32)
    zero_s = jnp.zeros((bp, RET_HEADS, HEAD_DIM, HEAD_DIM), F32)
    hp, hs = x_prompt, x_sample
    outs_p, outs_s = [], []
    for l in range(depth):
        final = l == depth - 1
        prm = _ssm_params(ssm_lambda_re[l], ssm_lambda_im[l], ssm_b_re[l], ssm_b_im[l], ssm_c_re[l],
                          ssm_c_im[l], ssm_d[l], ssm_log_step[l], ssm_w_glu[l], ssm_b_glu[l])

        def attend_p(ag, akt, avt, ikt, aqb, akb, avtb, iqb, ikb, iwt):
            return _dsa_prompt(bp, tp, ikb, akb, avtb, iqb, iwt, aqb, ag)

        hp, st_p = _mixer_layer(hp, pos_p, zero_h, zero_h, zero_s, attend_p, norm_g[l], w_in[l], w_out[l],
                                prm, ret_norm_g[l], final_norm_g, final, tm=256, tl=min(512, tp))

        def attend_s(ag, akt, avt, ikt, aqb, akb, avtb, iqb, ikb, iwt):
            return _dsa_sample(page_table, k_t, v_t, kidx_t, l, ts, akt[0], avt[0], ikt[0], ag,
                               aqb, iqb, iwt[0])

        hs, st_s = _mixer_layer(hs, pos_s, state_ssm_re[l].reshape(bs, 1, N_STATE),
                                state_ssm_im[l].reshape(bs, 1, N_STATE), state_ret[l], attend_s,
                                norm_g[l], w_in[l], w_out[l], prm, ret_norm_g[l], final_norm_g, final,
                                tm=256, tl=ts)
        outs_p.append(st_p)
        outs_s.append(st_s)

    def pack(outs, b, t):
        k = jnp.stack([o[0] for o in outs]).reshape(depth, b, t, ATT_HEADS, HEAD_DIM)
        v = jnp.stack([o[1] for o in outs]).reshape(depth, b, t, ATT_HEADS, HEAD_DIM)
        ki = jnp.stack([o[2] for o in outs])
        hr = jnp.stack([o[3] for o in outs]).reshape(depth, b, SSM_GROUPS, SSM_STATE)
        hi = jnp.stack([o[4] for o in outs]).reshape(depth, b, SSM_GROUPS, SSM_STATE)
        s = jnp.stack([o[5] for o in outs])
        return k, v, ki, hr, hi, s

    return (hp, hs) + pack(outs_p, bp, tp) + pack(outs_s, bs, ts)
```

```python
import functools
import math

import jax
import jax.numpy as jnp
import numpy as np
from jax import lax
from jax.experimental import pallas as pl
from jax.experimental.pallas import tpu as pltpu

F32 = jnp.float32
BF16 = jnp.bfloat16
I32 = jnp.int32

D_MODEL = 1024
HEAD_DIM = 64
SSM_GROUP = 16
D_SSM = 256
SSM_GROUPS = D_SSM // SSM_GROUP
SSM_STATE = 64
N_STATE = SSM_GROUPS * SSM_STATE
D_RET = 384
RET_HEADS = D_RET // HEAD_DIM
RET_CHUNK = 128
ROPE_BASE = 10000.0
D_ATT = 384
ATT_HEADS = D_ATT // HEAD_DIM
IDX_HEADS = 4
IDX_DIM = 64
TOPK_MAX = 256
Q_BLOCK = 128
PAGE_SIZE = 128
NORM_EPS = 1e-6
NEG_INF = -1e30

_SIZES = (D_SSM, D_SSM, D_RET, D_RET, D_RET, D_RET, D_ATT, D_ATT, D_ATT, D_ATT,
          IDX_HEADS * IDX_DIM, IDX_DIM, IDX_HEADS)
_OFF = np.concatenate([[0], np.cumsum(_SIZES)]).tolist()
C_U, C_RET, C_AQ, C_AK, C_AV, C_AG, C_IQ, C_IK, C_IW, C_END = (
    _OFF[0], _OFF[2], _OFF[6], _OFF[7], _OFF[8], _OFF[9], _OFF[10], _OFF[11], _OFF[12], _OFF[13])

LANES = 128
SUBLANES = 8
F32_INF_BITS = 0x7F800000
LOG2_E = 1.4426950408889634
V_ROWS = HEAD_DIM + 16
VMEM_LIMIT = 56 * 1024 * 1024

_NT = (((1,), (1,)), ((), ()))


def _cparams(n_axes):
    return pltpu.CompilerParams(dimension_semantics=("arbitrary",) * n_axes,
                                vmem_limit_bytes=VMEM_LIMIT)


_T_AK, _T_AV, _T_IK, _T_IW, _T_END = 0, D_ATT, 2 * D_ATT, 2 * D_ATT + IDX_DIM, 2 * D_ATT + IDX_DIM + SUBLANES


def _inproj_kernel(x_ref, g_ref, wm_ref, wt_ref, wqp_ref, wkp_ref,
                   u2_ref, r4_ref, ag_ref, akt_ref, avt_ref, ikt_ref,
                   aqb_ref, akb_ref, avtb_ref, iqb_ref, ikb_ref, iwt_ref):
    x = x_ref[...]
    ms = jnp.mean(x * x, axis=-1, keepdims=True)
    xn = (x * lax.rsqrt(ms + NORM_EPS) * g_ref[...]).astype(BF16)

    def seg(a, b):
        return jnp.dot(xn, wm_ref[:, a:b], preferred_element_type=F32)

    u2_ref[...] = seg(C_U, C_RET)
    r4_ref[...] = seg(C_RET, C_AQ)
    aqb_ref[...] = (jnp.dot(xn, wqp_ref[...], preferred_element_type=F32)
                    * (LOG2_E * HEAD_DIM ** -0.5)).astype(BF16)
    akb_ref[...] = jnp.dot(xn, wkp_ref[...], preferred_element_type=F32).astype(BF16)
    ag_ref[...] = seg(C_AG, C_IQ)
    iqb_ref[...] = seg(C_IQ, C_IK).astype(BF16)
    ikb_ref[...] = seg(C_IK, C_IW).astype(BF16)
    zt = lax.dot_general(wt_ref[...], xn, _NT, preferred_element_type=F32)
    akt_ref[...] = zt[_T_AK:_T_AV]
    avt = zt[_T_AV:_T_IK]
    avt_ref[...] = avt
    ones = jnp.ones((V_ROWS - HEAD_DIM, avt.shape[1]), BF16)
    for h in range(ATT_HEADS):
        avtb_ref[h * V_ROWS:h * V_ROWS + HEAD_DIM, :] = avt[h * HEAD_DIM:(h + 1) * HEAD_DIM].astype(BF16)
        avtb_ref[h * V_ROWS + HEAD_DIM:(h + 1) * V_ROWS, :] = ones
    ikt_ref[...] = zt[_T_IK:_T_IW]
    iwt_ref[...] = zt[_T_IW:_T_END] * (IDX_HEADS ** -0.5 * IDX_DIM ** -0.5)


def _inproj(x, bv, tv, g, wm, wt, wqp, wkp, tm):
    n = x.shape[0]
    nt = tv // tm
    row = lambda w: pl.BlockSpec((tm, w), lambda i: (i, 0))
    col = lambda h: pl.BlockSpec((None, h, tm), lambda i: (i // nt, 0, i % nt))
    full = lambda a: pl.BlockSpec(a.shape, lambda i: (0,) * a.ndim)
    out_shape = (
        jax.ShapeDtypeStruct((n, C_RET - C_U), F32),
        jax.ShapeDtypeStruct((n, C_AQ - C_RET), F32),
        jax.ShapeDtypeStruct((n, D_ATT), F32),
        jax.ShapeDtypeStruct((bv, D_ATT, tv), F32),
        jax.ShapeDtypeStruct((bv, D_ATT, tv), F32),
        jax.ShapeDtypeStruct((bv, IDX_DIM, tv), F32),
        jax.ShapeDtypeStruct((n, ATT_HEADS * LANES), BF16),
        jax.ShapeDtypeStruct((n, ATT_HEADS * LANES), BF16),
        jax.ShapeDtypeStruct((bv, ATT_HEADS * V_ROWS, tv), BF16),
        jax.ShapeDtypeStruct((n, IDX_HEADS * IDX_DIM), BF16),
        jax.ShapeDtypeStruct((n, IDX_DIM), BF16),
        jax.ShapeDtypeStruct((bv, SUBLANES, tv), F32),
    )
    out_specs = (
        row(C_RET - C_U), row(C_AQ - C_RET), row(D_ATT), col(D_ATT), col(D_ATT), col(IDX_DIM),
        row(ATT_HEADS * LANES), row(ATT_HEADS * LANES), col(ATT_HEADS * V_ROWS), row(IDX_HEADS * IDX_DIM),
        row(IDX_DIM), col(SUBLANES))
    return pl.pallas_call(
        _inproj_kernel,
        grid=(n // tm,),
        in_specs=[row(D_MODEL), full(g), full(wm), full(wt), full(wqp), full(wkp)],
        out_specs=out_specs,
        out_shape=out_shape,
        compiler_params=_cparams(1),
        name="inproj",
    )(x, g, wm, wt, wqp, wkp)


def _outproj_kernel(x_ref, ys_ref, yr_ref, ya_ref, wo_ref, gf_ref, o_ref, *, final):
    y = x_ref[...]
    y = y + jnp.dot(ys_ref[...], wo_ref[0:D_SSM, :], preferred_element_type=F32)
    y = y + jnp.dot(yr_ref[...], wo_ref[D_SSM:D_SSM + D_RET, :], preferred_element_type=F32)
    y = y + jnp.dot(ya_ref[...], wo_ref[D_SSM + D_RET:, :], preferred_element_type=F32)
    if final:
        ms = jnp.mean(y * y, axis=-1, keepdims=True)
        y = y * lax.rsqrt(ms + NORM_EPS) * gf_ref[...]
    o_ref[...] = y


def _outproj(x, ys, yr, ya, wo, gf, tm, final):
    n = x.shape[0]
    row = lambda w: pl.BlockSpec((tm, w), lambda i: (i, 0))
    full = lambda a: pl.BlockSpec(a.shape, lambda i: (0,) * a.ndim)
    return pl.pallas_call(
        functools.partial(_outproj_kernel, final=final),
        grid=(n // tm,),
        in_specs=[row(D_MODEL), row(D_SSM), row(D_RET), row(D_ATT), full(wo), full(gf)],
        out_specs=row(D_MODEL),
        out_shape=jax.ShapeDtypeStruct((n, D_MODEL), F32),
        compiler_params=_cparams(1),
        name="outproj",
    )(x, ys, yr, ya, wo, gf)


def _cmul(ar, ai, br, bi):
    return ar * br - ai * bi, ar * bi + ai * br


def _ssm_kernel(u2_ref, h0r_ref, h0i_ref, lr_ref, li_ref, ls_ref, btr_ref, bti_ref,
                ctr_ref, cti_ref, d_ref, wg_ref, bg_ref,
                y_ref, hr_out_ref, hi_out_ref,
                bbr_ref, bbi_ref, apr_ref, api_ref, cr_ref, ci_ref, hr_ref, hi_ref, *, tl):
    b = pl.program_id(0)
    c = pl.program_id(1)
    nc = pl.num_programs(1)

    @pl.when((b == 0) & (c == 0))
    def _prep():
        lr, li = lr_ref[...], li_ref[...]
        dt = jnp.exp(ls_ref[...])
        mag = jnp.exp(lr * dt)
        abr, abi = mag * jnp.cos(li * dt), mag * jnp.sin(li * dt)
        den = lr * lr + li * li
        nr, ni = abr - 1.0, abi
        fr = (nr * lr + ni * li) / den
        fi = (ni * lr - nr * li) / den
        bbr_ref[...] = (fr * btr_ref[...] - fi * bti_ref[...]).astype(BF16)
        bbi_ref[...] = (fr * bti_ref[...] + fi * btr_ref[...]).astype(BF16)
        pr, pi = abr, abi
        rows_r, rows_i = [pr], [pi]
        for _ in range(SUBLANES - 1):
            pr, pi = _cmul(pr, pi, abr, abi)
            rows_r.append(pr)
            rows_i.append(pi)
        apr_ref[...] = jnp.concatenate(rows_r, axis=0)
        api_ref[...] = jnp.concatenate(rows_i, axis=0)

    @pl.when(c == 0)
    def _init():
        cr_ref[...] = h0r_ref[...]
        ci_ref[...] = h0i_ref[...]

    u = u2_ref[:, 0:D_SSM]
    gate = u2_ref[:, D_SSM:2 * D_SSM]
    ub = u.astype(BF16)
    hr_ref[...] = jnp.dot(ub, bbr_ref[...], preferred_element_type=F32)
    hi_ref[...] = jnp.dot(ub, bbi_ref[...], preferred_element_type=F32)

    apr, api = apr_ref[...], api_ref[...]
    rowid = lax.broadcasted_iota(I32, (SUBLANES, N_STATE), 0)

    def blk(j, carry):
        cr, ci = carry
        r0 = pl.multiple_of(j * SUBLANES, SUBLANES)
        br = hr_ref[pl.ds(r0, SUBLANES), :]
        bi = hi_ref[pl.ds(r0, SUBLANES), :]
        for k in (1, 2, 4):
            keep = rowid >= k
            sr = jnp.where(keep, pltpu.roll(br, k, 0), 0.0)
            si = jnp.where(keep, pltpu.roll(bi, k, 0), 0.0)
            akr, aki = apr[k - 1:k, :], api[k - 1:k, :]
            br, bi = br + (akr * sr - aki * si), bi + (akr * si + aki * sr)
        hr = br + (apr * cr - api * ci)
        hi = bi + (apr * ci + api * cr)
        hr_ref[pl.ds(r0, SUBLANES), :] = hr
        hi_ref[pl.ds(r0, SUBLANES), :] = hi
        return hr[SUBLANES - 1:SUBLANES, :], hi[SUBLANES - 1:SUBLANES, :]

    cr, ci = lax.fori_loop(0, tl // SUBLANES, blk, (cr_ref[...], ci_ref[...]))
    cr_ref[...] = cr
    ci_ref[...] = ci

    @pl.when(c == nc - 1)
    def _fin():
        hr_out_ref[...] = cr
        hi_out_ref[...] = ci

    y = (jnp.dot(hr_ref[...].astype(BF16), ctr_ref[...], preferred_element_type=F32)
         - jnp.dot(hi_ref[...].astype(BF16), cti_ref[...], preferred_element_type=F32)
         + d_ref[...] * u)
    g = jax.nn.gelu(y)
    y = g * jax.nn.sigmoid(jnp.dot(g.astype(BF16), wg_ref[...], preferred_element_type=F32) + bg_ref[...])
    y_ref[...] = (y * (gate * jax.nn.sigmoid(gate))).astype(BF16)


def _ssm(u2, h0r, h0i, prm, tl):
    bv, tv, _ = u2.shape
    full = lambda a: pl.BlockSpec(a.shape, lambda b, c: (0,) * a.ndim)
    st = pl.BlockSpec((None, 1, N_STATE), lambda b, c: (b, 0, 0))
    return pl.pallas_call(
        functools.partial(_ssm_kernel, tl=tl),
        grid=(bv, tv // tl),
        in_specs=[pl.BlockSpec((None, tl, 2 * D_SSM), lambda b, c: (b, c, 0)), st, st]
        + [full(a) for a in prm],
        out_specs=(pl.BlockSpec((None, tl, D_SSM), lambda b, c: (b, c, 0)), st, st),
        out_shape=(jax.ShapeDtypeStruct((bv, tv, D_SSM), BF16),
                   jax.ShapeDtypeStruct((bv, 1, N_STATE), F32),
                   jax.ShapeDtypeStruct((bv, 1, N_STATE), F32)),
        scratch_shapes=[pltpu.VMEM((D_SSM, N_STATE), BF16), pltpu.VMEM((D_SSM, N_STATE), BF16),
                        pltpu.VMEM((SUBLANES, N_STATE), F32), pltpu.VMEM((SUBLANES, N_STATE), F32),
                        pltpu.VMEM((1, N_STATE), F32), pltpu.VMEM((1, N_STATE), F32),
                        pltpu.VMEM((tl, N_STATE), F32), pltpu.VMEM((tl, N_STATE), F32)],
        compiler_params=_cparams(2),
        name="ssm",
    )(u2, h0r, h0i, *prm)


def _ssm_params(lam_re, lam_im, b_re, b_im, c_re, c_im, d_skip, log_step, w_glu, b_glu):
    eye = jnp.eye(SSM_GROUPS, dtype=F32)

    def bdiag(m):
        g, r, c = m.shape
        return (eye[:, None, :, None] * m[:, :, None, :]).reshape(g * r, g * c)

    flat = lambda a: a.astype(F32).reshape(1, N_STATE)
    ls = jnp.broadcast_to(log_step.astype(F32)[:, None], (SSM_GROUPS, SSM_STATE))
    btr = bdiag(jnp.swapaxes(b_re.astype(F32), 1, 2))
    bti = bdiag(jnp.swapaxes(b_im.astype(F32), 1, 2))
    ctr = bdiag(jnp.swapaxes(c_re.astype(F32), 1, 2)).astype(BF16)
    cti = bdiag(jnp.swapaxes(c_im.astype(F32), 1, 2)).astype(BF16)
    return (flat(lam_re), flat(lam_im), flat(ls), btr, bti, ctr, cti,
            d_skip.astype(F32).reshape(1, D_SSM), w_glu.astype(BF16),
            b_glu.astype(F32).reshape(1, D_SSM))


def _swap_halves(x):
    lane = lax.broadcasted_iota(I32, (x.shape[0], LANES), 1)
    first = (lane % HEAD_DIM) < (HEAD_DIM // 2)
    tiles = []
    for t in range(x.shape[1] // LANES):
        xt = x[:, t * LANES:(t + 1) * LANES]
        up = pltpu.roll(xt, LANES - HEAD_DIM // 2, 1)
        dn = pltpu.roll(xt, HEAD_DIM // 2, 1)
        tiles.append(jnp.where(first, up, dn))
    return jnp.concatenate(tiles, axis=1)


def _ret_kernel(r4_ref, s0_ref, cos_ref, sin_ref, din_ref, dq_ref, dk_ref, ds_ref, g_ref,
                y_ref, s_out_ref, s_ref):
    c = pl.program_id(1)
    nc = pl.num_programs(1)

    @pl.when(c == 0)
    def _init():
        s_ref[...] = s0_ref[...]

    q = r4_ref[:, 0:D_RET]
    k = r4_ref[:, D_RET:2 * D_RET]
    v = r4_ref[:, 2 * D_RET:3 * D_RET]
    gate = r4_ref[:, 3 * D_RET:4 * D_RET]
    cos = jnp.concatenate([cos_ref[...]] * (D_RET // LANES), axis=1)
    sin = jnp.concatenate([sin_ref[...]] * (D_RET // LANES), axis=1)
    qr = q * cos + _swap_halves(q) * sin
    kr = (k * cos + _swap_halves(k) * sin) * HEAD_DIM ** -0.5
    qb = qr.astype(BF16)
    kb = kr.astype(BF16)
    vb = v.astype(BF16)
    kdt = (kr * dk_ref[...]).T.astype(BF16)
    dq = dq_ref[...]
    gsil = gate * jax.nn.sigmoid(gate) * g_ref[...]
    sls = [slice(h * HEAD_DIM, (h + 1) * HEAD_DIM) for h in range(RET_HEADS)]
    atts, crosses = [], []
    for h, sl in enumerate(sls):
        qh, kh, vh = qb[:, sl], kb[:, sl], vb[:, sl]
        s = s_ref[h]
        atts.append((lax.dot_general(qh, kh, _NT, preferred_element_type=F32) * din_ref[h]).astype(BF16))
        crosses.append(jnp.dot(qh, s.astype(BF16), preferred_element_type=F32) * dq[:, sl])
        s_ref[h] = s * ds_ref[h] + jnp.dot(kdt[sl, :], vh, preferred_element_type=F32)
    for h, sl in enumerate(sls):
        o = jnp.dot(atts[h], vb[:, sl], preferred_element_type=F32) + crosses[h]
        mu = jnp.mean(o, axis=-1, keepdims=True)
        var = jnp.mean(jnp.square(o - mu), axis=-1, keepdims=True)
        y_ref[:, sl] = ((o - mu) * lax.rsqrt(var + NORM_EPS) * gsil[:, sl]).astype(BF16)

    @pl.when(c == nc - 1)
    def _fin():
        s_out_ref[...] = s_ref[...]


def _ret_tables(pos, chunk):
    half = HEAD_DIM // 2
    inv = ROPE_BASE ** (-jnp.arange(half, dtype=F32) / half)
    ang = pos.astype(F32)[:, None] * inv[None, :]
    cos, sin = jnp.cos(ang), jnp.sin(ang)
    cos_t = jnp.concatenate([cos, cos] * (LANES // HEAD_DIM), axis=1)
    sin_t = jnp.concatenate([-sin, sin] * (LANES // HEAD_DIM), axis=1)
    log_g = jnp.log1p(-jnp.exp2(-5.0 - jnp.arange(RET_HEADS, dtype=F32)))
    j = jnp.arange(chunk, dtype=F32)
    rel = j[:, None] - j[None, :]
    din = jnp.where(rel[None] >= 0, jnp.exp(log_g[:, None, None] * jnp.maximum(rel, 0.0)[None]), 0.0)
    dq = jnp.repeat(jnp.exp(log_g[None, :] * (j[:, None] + 1.0)), HEAD_DIM, axis=1)
    dk = jnp.repeat(jnp.exp(log_g[None, :] * (chunk - 1.0 - j[:, None])), HEAD_DIM, axis=1)
    ds = jnp.broadcast_to(jnp.exp(log_g * chunk)[:, None, None], (RET_HEADS, HEAD_DIM, HEAD_DIM))
    return cos_t, sin_t, din, dq, dk, ds


def _ret(r4, s0, pos, norm_g):
    bv, tv, _ = r4.shape
    chunk = RET_CHUNK if tv % RET_CHUNK == 0 else tv
    cos_t, sin_t, din, dq, dk, ds = _ret_tables(pos, chunk)
    g = norm_g.astype(F32).reshape(1, D_RET)
    full = lambda a: pl.BlockSpec(a.shape, lambda b, c: (0,) * a.ndim)
    st = pl.BlockSpec((None, RET_HEADS, HEAD_DIM, HEAD_DIM), lambda b, c: (b, 0, 0, 0))
    tab = pl.BlockSpec((chunk, LANES), lambda b, c: (c, 0))
    return pl.pallas_call(
        _ret_kernel,
        grid=(bv, tv // chunk),
        in_specs=[pl.BlockSpec((None, chunk, 4 * D_RET), lambda b, c: (b, c, 0)), st, tab, tab,
                  full(din), full(dq), full(dk), full(ds), full(g)],
        out_specs=(pl.BlockSpec((None, chunk, D_RET), lambda b, c: (b, c, 0)), st),
        out_shape=(jax.ShapeDtypeStruct((bv, tv, D_RET), BF16),
                   jax.ShapeDtypeStruct((bv, RET_HEADS, HEAD_DIM, HEAD_DIM), F32)),
        scratch_shapes=[pltpu.VMEM((RET_HEADS, HEAD_DIM, HEAD_DIM), F32)],
        compiler_params=_cparams(2),
        name="retention",
    )(r4, s0, cos_t, sin_t, din, dq, dk, ds, g)


def _count(sc_ref, nk, kb, pred):
    n_acc = 4

    def body(c, cnts):
        r0 = pl.multiple_of(c * kb, kb)
        cnts = list(cnts)
        for j in range(kb // SUBLANES):
            blk = sc_ref[pl.ds(r0 + j * SUBLANES, SUBLANES), :]
            cnts[j % n_acc] = cnts[j % n_acc] + jnp.where(pred(blk), 1, 0)
        return tuple(cnts)
    zero = jnp.zeros((SUBLANES, LANES), I32)
    cnts = lax.fori_loop(0, nk, body, (zero,) * n_acc)
    return jnp.sum((cnts[0] + cnts[1]) + (cnts[2] + cnts[3]), axis=0, keepdims=True)


_ORD_FLIP = 0x7FFFFFFF
_ORD_POS_INF = F32_INF_BITS
_ORD_NEG_INF = -F32_INF_BITS - 1


def _unord(o):
    return lax.bitcast_convert_type(jnp.where(o >= 0, o, o ^ _ORD_FLIP), F32)


def _select_bias(sc_ref, bias_ref, tri_ref, nk, kb, n_beyond, qpos, topk):
    def count_ge(t):
        return _count(sc_ref, nk, kb, lambda blk: blk >= t) + jnp.where(t <= NEG_INF, n_beyond, 0)

    pos = jnp.where(count_ge(jnp.zeros((1, LANES), F32)) >= topk, 1, 0)
    lo = jnp.where(pos == 1, 0, _ORD_NEG_INF)
    hi = jnp.where(pos == 1, _ORD_POS_INF + 1, 0)
    n_steps = 31

    def step(_, carry):
        lo, hi, c_lo = carry
        mid = lo + lax.shift_right_arithmetic(hi - lo, 1)
        c = count_ge(_unord(mid))
        ok = c >= topk
        return jnp.where(ok, mid, lo), jnp.where(ok, hi, mid), jnp.where(ok, c, c_lo)

    lo, _, c_lo = lax.fori_loop(0, n_steps, step, (lo, hi, jnp.full((1, LANES), topk + 1, I32)))
    tau = _unord(lo)
    surplus_ties = jnp.max(c_lo) > topk

    @pl.when(jnp.logical_not(surplus_ties))
    def _no_surplus():
        def body(c, _):
            r0 = pl.multiple_of(c * kb, kb)
            blk = sc_ref[pl.ds(r0, kb), :]
            kidx = r0 + lax.broadcasted_iota(I32, (kb, LANES), 0)
            bias_ref[pl.ds(r0, kb), :] = jnp.where((blk >= tau) & (kidx <= qpos), 0.0, NEG_INF)
            return 0
        lax.fori_loop(0, nk, body, 0)

    @pl.when(surplus_ties)
    def _ties_by_index():
        c_gt = _count(sc_ref, nk, kb, lambda blk: blk > tau) + jnp.where(NEG_INF > tau, n_beyond, 0)
        need = (topk - c_gt).astype(F32)

        def body(c, carry):
            r0 = pl.multiple_of(c * kb, kb)
            blk = sc_ref[pl.ds(r0, kb), :]
            eq = blk == tau
            pref = jnp.dot(tri_ref[...], jnp.where(eq, 1.0, 0.0).astype(BF16),
                           preferred_element_type=F32) + carry
            kidx = r0 + lax.broadcasted_iota(I32, (kb, LANES), 0)
            sel = ((blk > tau) | (eq & (pref <= need))) & (kidx <= qpos)
            bias_ref[pl.ds(r0, kb), :] = jnp.where(sel, 0.0, NEG_INF)
            return pref[kb - 1:kb, :]

        lax.fori_loop(0, nk, body, jnp.zeros((1, LANES), F32))


def _tri(kb):
    r = np.arange(kb)
    return jnp.asarray((r[None, :] <= r[:, None]).astype(np.float32), dtype=BF16)


def _dsa_prompt_kernel(ikb_ref, akb_ref, avtb_ref, iqb_ref, iwt_ref, aqb_ref, ag_ref, tri_ref,
                       y_ref, sc_ref, bias_ref, lg_ref, *o_refs, kb, ka, t_total, topk):
    i = pl.program_id(1)
    q0 = i * Q_BLOCK
    nk = (q0 + Q_BLOCK + kb - 1) // kb
    qpos = q0 + lax.broadcasted_iota(I32, (1, LANES), 1)
    iq = iqb_ref[...]
    iw = iwt_ref[...]
    qcat = jnp.concatenate([iq[:, h * IDX_DIM:(h + 1) * IDX_DIM] for h in range(IDX_HEADS)], axis=0)

    def score_body(c, _):
        r0 = pl.multiple_of(c * kb, kb)
        keys = ikb_ref[pl.ds(r0, kb), :]
        s = lax.dot_general(keys, qcat, _NT, preferred_element_type=F32)
        parts = [jnp.maximum(s[:, h * LANES:(h + 1) * LANES], 0.0) * iw[h:h + 1, :]
                 for h in range(IDX_HEADS)]
        acc = (parts[0] + parts[1]) + (parts[2] + parts[3])
        kidx = r0 + lax.broadcasted_iota(I32, (kb, LANES), 0)
        sc_ref[pl.ds(r0, kb), :] = jnp.where(kidx <= qpos, acc, NEG_INF)
        return 0

    lax.fori_loop(0, nk, score_body, 0)
    _select_bias(sc_ref, bias_ref, tri_ref, nk, kb, t_total - nk * kb, qpos, topk)

    aq = aqb_ref[...]
    qhs = [aq[:, h * LANES:(h + 1) * LANES] for h in range(ATT_HEADS)]
    for o_ref in o_refs:
        o_ref[...] = jnp.zeros(o_ref.shape, F32)

    def att_body(c, ms):
        r0 = pl.multiple_of(c * ka, ka)
        bias = bias_ref[pl.ds(r0, ka), :]
        ms_new = []
        mx = []
        for h in range(ATT_HEADS):
            kh = akb_ref[pl.ds(r0, ka), h * LANES:(h + 1) * LANES]
            logit = lax.dot_general(kh, qhs[h], _NT, preferred_element_type=F32) + bias
            lg_ref[h] = logit
            mx.append(jnp.max(logit, axis=0, keepdims=True))
        for h in range(ATT_HEADS):
            m_new = jnp.maximum(ms[h], mx[h])
            alpha = jnp.exp2(ms[h] - m_new)
            p = jnp.exp2(lg_ref[h] - m_new)
            ms_new.append(m_new)
            vt = avtb_ref[h * V_ROWS:(h + 1) * V_ROWS, pl.ds(r0, ka)]
            o_refs[h][...] = alpha * o_refs[h][...] + jnp.dot(vt, p.astype(BF16), preferred_element_type=F32)
        return tuple(ms_new)

    m0 = (jnp.full((1, LANES), NEG_INF, F32),) * ATT_HEADS
    lax.fori_loop(0, nk * (kb // ka), att_body, m0)
    ot = jnp.concatenate([o_refs[h][0:HEAD_DIM] / o_refs[h][HEAD_DIM:HEAD_DIM + 1]
                          for h in range(ATT_HEADS)], axis=0)
    gate = ag_ref[...]
    y_ref[...] = (ot.T * (gate * jax.nn.sigmoid(gate))).astype(BF16)


def _dsa_prompt(bv, tv, ikb, akb, avtb, iqb, iwt, aqb, ag, kb=512, ka=512):
    topk = min(TOPK_MAX, tv // 4)
    kb = min(kb, tv)
    nq = tv // Q_BLOCK
    tri = _tri(kb)
    per_b = lambda w: pl.BlockSpec((tv, w), lambda b, i: (b, 0))
    qrow = lambda w: pl.BlockSpec((Q_BLOCK, w), lambda b, i: (b * nq + i, 0))
    return pl.pallas_call(
        functools.partial(_dsa_prompt_kernel, kb=kb, ka=min(ka, kb), t_total=tv, topk=topk),
        grid=(bv, nq),
        in_specs=[per_b(IDX_DIM), per_b(ATT_HEADS * LANES),
                  pl.BlockSpec((None, ATT_HEADS * V_ROWS, tv), lambda b, i: (b, 0, 0)),
                  qrow(IDX_HEADS * IDX_DIM),
                  pl.BlockSpec((None, SUBLANES, Q_BLOCK), lambda b, i: (b, 0, i)),
                  qrow(ATT_HEADS * LANES), qrow(D_ATT),
                  pl.BlockSpec(tri.shape, lambda b, i: (0, 0))],
        out_specs=qrow(D_ATT),
        out_shape=jax.ShapeDtypeStruct((bv * tv, D_ATT), BF16),
        scratch_shapes=[pltpu.VMEM((tv, LANES), F32), pltpu.VMEM((tv, LANES), F32)]
        + [pltpu.VMEM((ATT_HEADS, min(ka, kb), LANES), F32)]
        + [pltpu.VMEM((V_ROWS, LANES), F32)] * ATT_HEADS,
        compiler_params=_cparams(2),
        name="dsa_prompt",
    )(ikb, akb, avtb, iqb, iwt, aqb, ag, tri)


def _samp_score_kernel(pt_ref, *refs, npg, ts):
    del pt_ref
    pages = refs[:npg]
    qi_ref, w_ref, iknew_ref, sc_ref, scn_ref = refs[npg:]
    qi = qi_ref[...]
    w = w_ref[...]

    def score(keys_t):
        s = jnp.dot(qi, keys_t.astype(BF16), preferred_element_type=F32)
        r = jnp.maximum(s, 0.0) * w
        return (r[0:ts] + r[ts:2 * ts]) + (r[2 * ts:3 * ts] + r[3 * ts:4 * ts])

    for i in range(npg):
        sc_ref[:, i * PAGE_SIZE:(i + 1) * PAGE_SIZE] = score(pages[i][...])

    @pl.when(pl.program_id(1) == 0)
    def _new_keys():
        sn = score(iknew_ref[...])
        col = lax.broadcasted_iota(I32, (ts, PAGE_SIZE), 1)
        row = lax.broadcasted_iota(I32, (ts, PAGE_SIZE), 0)
        scn_ref[...] = jnp.where(col <= row, sn, NEG_INF)


def _samp_scores(page_table, kidx_t, layer, qi, w, iknew_t, npg):
    bs, n_pages = page_table.shape
    ts = qi.shape[1] // IDX_HEADS
    page_spec = lambda i: pl.BlockSpec(
        (None, None, IDX_DIM, PAGE_SIZE), lambda b, j, pt, i=i: (layer, pt[b, j * npg + i], 0, 0))
    per_b = lambda a: pl.BlockSpec((None,) + a.shape[1:], lambda b, j, pt: (b, 0, 0))
    gs = pltpu.PrefetchScalarGridSpec(
        num_scalar_prefetch=1,
        grid=(bs, n_pages // npg),
        in_specs=[page_spec(i) for i in range(npg)] + [per_b(qi), per_b(w), per_b(iknew_t)],
        out_specs=(pl.BlockSpec((ts, npg * PAGE_SIZE), lambda b, j, pt: (b, j)),
                   pl.BlockSpec((ts, PAGE_SIZE), lambda b, j, pt: (b, 0))),
    )
    return pl.pallas_call(
        functools.partial(_samp_score_kernel, npg=npg, ts=ts),
        grid_spec=gs,
        out_shape=(jax.ShapeDtypeStruct((bs * ts, n_pages * PAGE_SIZE), F32),
                   jax.ShapeDtypeStruct((bs * ts, PAGE_SIZE), F32)),
        compiler_params=_cparams(2),
        name="sample_scores",
    )(page_table, *([kidx_t] * npg), qi, w, iknew_t)


def _select_kernel(sc_ref, qpos_ref, tri_ref, bias_ref, *, kb, nk, topk):
    _select_bias(sc_ref, bias_ref, tri_ref, nk, kb, 0, qpos_ref[...], topk)


def _select(sc_t, qpos, topk, kb):
    nkeys, nq = sc_t.shape
    tri = _tri(kb)
    col = pl.BlockSpec((nkeys, LANES), lambda i: (0, i))
    return pl.pallas_call(
        functools.partial(_select_kernel, kb=kb, nk=nkeys // kb, topk=topk),
        grid=(nq // LANES,),
        in_specs=[col, pl.BlockSpec((1, LANES), lambda i: (0, i)),
                  pl.BlockSpec(tri.shape, lambda i: (0, 0))],
        out_specs=col,
        out_shape=jax.ShapeDtypeStruct((nkeys, nq), F32),
        compiler_params=_cparams(1),
        name="sample_select",
    )(sc_t, qpos, tri)


def _samp_att_kernel(pt_ref, *refs, npg, ts):
    del pt_ref
    kp = refs[:npg]
    vp = refs[npg:2 * npg]
    q_ref, bias_ref, biasn_ref, kn_ref, vn_ref, ag_ref, y_ref, m_ref, l_ref, acc_ref = refs[2 * npg:]
    j = pl.program_id(1)
    nj = pl.num_programs(1)

    @pl.when(j == 0)
    def _init():
        m_ref[...] = jnp.full(m_ref.shape, NEG_INF, F32)
        l_ref[...] = jnp.zeros(l_ref.shape, F32)
        acc_ref[...] = jnp.zeros(acc_ref.shape, F32)

    q = q_ref[...]

    def update(kts, vts, bias):
        logit = jnp.concatenate(
            [jnp.dot(q, kt.astype(BF16), preferred_element_type=F32) for kt in kts], axis=1)
        logit = logit + jnp.concatenate([bias] * ATT_HEADS, axis=0)
        m_old = m_ref[...]
        m_new = jnp.maximum(m_old, jnp.max(logit, axis=1, keepdims=True))
        alpha = jnp.exp2(m_old - m_new)
        p = jnp.exp2(logit - m_new)
        l_ref[...] = alpha * l_ref[...] + jnp.sum(p, axis=1, keepdims=True)
        pb = p.astype(BF16)
        pvs = [lax.dot_general(pb[:, i * PAGE_SIZE:(i + 1) * PAGE_SIZE], vts[i].astype(BF16), _NT,
                               preferred_element_type=F32) for i in range(len(vts))]
        while len(pvs) > 1:
            pvs = [pvs[i] + pvs[i + 1] for i in range(0, len(pvs) - 1, 2)] + pvs[len(pvs) - len(pvs) % 2:]
        acc_ref[...] = alpha * acc_ref[...] + pvs[0]
        m_ref[...] = m_new

    update([r[...] for r in kp], [r[...] for r in vp], bias_ref[...])

    @pl.when(j == nj - 1)
    def _fin():
        update([kn_ref[...]], [vn_ref[...]], biasn_ref[...])
        o = acc_ref[...] / l_ref[...]
        gate = ag_ref[...]
        gsil = gate * jax.nn.sigmoid(gate)
        for h in range(ATT_HEADS):
            sl = slice(h * HEAD_DIM, (h + 1) * HEAD_DIM)
            y_ref[:, sl] = (o[h * ts:(h + 1) * ts, sl] * gsil[:, sl]).astype(BF16)


def _samp_attention(page_table, k_t, v_t, layer, qbd, bias, biasn, knew_t, vnew_t, ag, npg):
    bs, n_pages = page_table.shape
    ts = qbd.shape[1] // ATT_HEADS
    page_spec = lambda i: pl.BlockSpec(
        (None, None, D_ATT, PAGE_SIZE), lambda b, j, pt, i=i: (layer, pt[b, j * npg + i], 0, 0))
    per_b = lambda a: pl.BlockSpec((None,) + a.shape[1:], lambda b, j, pt: (b, 0, 0))
    gs = pltpu.PrefetchScalarGridSpec(
        num_scalar_prefetch=1,
        grid=(bs, n_pages // npg),
        in_specs=[page_spec(i) for i in range(npg)] * 2
        + [per_b(qbd),
           pl.BlockSpec((ts, npg * PAGE_SIZE), lambda b, j, pt: (b, j)),
           pl.BlockSpec((ts, PAGE_SIZE), lambda b, j, pt: (b, 0)),
           per_b(knew_t), per_b(vnew_t),
           pl.BlockSpec((ts, D_ATT), lambda b, j, pt: (b, 0))],
        out_specs=pl.BlockSpec((ts, D_ATT), lambda b, j, pt: (b, 0)),
        scratch_shapes=[pltpu.VMEM((ATT_HEADS * ts, 1), F32), pltpu.VMEM((ATT_HEADS * ts, 1), F32),
                        pltpu.VMEM((ATT_HEADS * ts, D_ATT), F32)],
    )
    return pl.pallas_call(
        functools.partial(_samp_att_kernel, npg=npg, ts=ts),
        grid_spec=gs,
        out_shape=jax.ShapeDtypeStruct((bs * ts, D_ATT), BF16),
        compiler_params=_cparams(2),
        name="sample_attention",
    )(page_table, *([k_t] * npg), *([v_t] * npg), qbd, bias, biasn, knew_t, vnew_t, ag)


def _dsa_sample(page_table, k_t, v_t, kidx_t, layer, ts, akt, avt, ikt, ag, aqb, iqb, iwt):
    bs, n_pages = page_table.shape
    past = n_pages * PAGE_SIZE
    n = bs * ts
    topk = min(TOPK_MAX, (past + ts) // 4)
    npg = math.gcd(n_pages, 16)

    def new_keys_t(a):
        a = a.reshape(a.shape[0], bs, ts).transpose(1, 0, 2)
        return jnp.pad(a, ((0, 0), (0, 0), (0, PAGE_SIZE - ts)))

    qi = iqb.reshape(bs, ts, IDX_HEADS, IDX_DIM).transpose(0, 2, 1, 3).reshape(bs, IDX_HEADS * ts, IDX_DIM)
    w = iwt[:IDX_HEADS].reshape(IDX_HEADS, bs, ts).transpose(1, 0, 2).reshape(bs, IDX_HEADS * ts, 1)
    w = jnp.broadcast_to(w, (bs, IDX_HEADS * ts, LANES))
    sc, scn = _samp_scores(page_table, kidx_t, layer, qi, w, new_keys_t(ikt), npg)
    sc_t = jnp.concatenate([sc, scn], axis=1).T
    qpos = (past + jnp.arange(n, dtype=I32) % ts).reshape(1, n)
    bias_all = _select(sc_t, qpos, topk, PAGE_SIZE).T
    eye = jnp.eye(ATT_HEADS, dtype=BF16)
    q4 = aqb.reshape(bs, ts, ATT_HEADS, LANES)[..., :HEAD_DIM]
    qbd = (eye[None, :, None, :, None] * q4.transpose(0, 2, 1, 3)[:, :, :, None, :]).reshape(
        bs, ATT_HEADS * ts, D_ATT)
    return _samp_attention(page_table, k_t, v_t, layer, qbd, bias_all[:, :past], bias_all[:, past:],
                           new_keys_t(akt), new_keys_t(avt), ag, npg)


def _mixer_layer(x, pos, h0r, h0i, s0, attend, g, w_in_l, w_out_l, ssm_prm, ret_g, gf, final, tm, tl):
    bv, tv, _ = x.shape
    n = bv * tv
    wm = w_in_l[:, :C_IW].astype(BF16)
    wt = jnp.concatenate([w_in_l[:, C_AK:C_AG].T, w_in_l[:, C_IK:C_END].T,
                          jnp.zeros((SUBLANES - IDX_HEADS, D_MODEL), w_in_l.dtype)], axis=0).astype(BF16)
    slots = lambda w: jnp.pad(w.reshape(D_MODEL, ATT_HEADS, HEAD_DIM),
                              ((0, 0), (0, 0), (0, LANES - HEAD_DIM))).reshape(D_MODEL, ATT_HEADS * LANES)
    wqp = slots(w_in_l[:, C_AQ:C_AK]).astype(BF16)
    wkp = slots(w_in_l[:, C_AK:C_AV]).astype(BF16)
    xf = x.reshape(n, D_MODEL)
    pb, pt = (bv, tv) if tv % tm == 0 else (1, n)
    tm = min(tm, pt)
    (u2, r4, ag, akt, avt, ikt, aqb, akb, avtb, iqb, ikb, iwt) = _inproj(
        xf, pb, pt, g.astype(F32).reshape(1, D_MODEL), wm, wt, wqp, wkp, tm)
    y_ssm, hr, hi = _ssm(u2.reshape(bv, tv, -1), h0r, h0i, ssm_prm, tl)
    y_ret, s_new = _ret(r4.reshape(bv, tv, -1), s0, pos, ret_g)
    y_att = attend(ag, akt, avt, ikt, aqb, akb, avtb, iqb, ikb, iwt)
    y = _outproj(xf, y_ssm.reshape(n, D_SSM), y_ret.reshape(n, D_RET), y_att,
                 w_out_l.astype(BF16), gf.astype(F32).reshape(1, D_MODEL), tm, final)
    tok = lambda a: a.reshape(pb, a.shape[1], -1, tv).transpose(0, 2, 3, 1).reshape(bv, tv, a.shape[1])
    return y.reshape(bv, tv, D_MODEL), (tok(akt), tok(avt), tok(ikt), hr, hi, s_new)


def kernel(x_prompt, x_sample, cache_k, cache_v, cache_kidx, state_ssm_re, state_ssm_im, state_ret,
           page_table, norm_g, w_in, w_out, ssm_lambda_re, ssm_lambda_im, ssm_b_re, ssm_b_im,
           ssm_c_re, ssm_c_im, ssm_d, ssm_log_step, ssm_w_glu, ssm_b_glu, ret_norm_g, final_norm_g):
    bp, tp, _ = x_prompt.shape
    bs, ts, _ = x_sample.shape
    depth = w_in.shape[0]
    past = page_table.shape[1] * PAGE_SIZE
    assert ts == SUBLANES and tp % Q_BLOCK == 0
    pos_p = jnp.arange(tp)
    pos_s = past + jnp.arange(ts)
    k_t = jnp.transpose(cache_k, (0, 1, 3, 4, 2)).reshape(cache_k.shape[:2] + (D_ATT, PAGE_SIZE))
    v_t = jnp.transpose(cache_v, (0, 1, 3, 4, 2)).reshape(cache_v.shape[:2] + (D_ATT, PAGE_SIZE))
    kidx_t = jnp.transpose(cache_kidx, (0, 1, 3, 2))
    zero_h = jnp.zeros((bp, 1, N_STATE), F32)
    zero_s = jnp.zeros((bp, RET_HEADS, HEAD_DIM, HEAD_DIM), F32)
    hp, hs = x_prompt, x_sample
    outs_p, outs_s = [], []
    for l in range(depth):
        final = l == depth - 1
        prm = _ssm_params(ssm_lambda_re[l], ssm_lambda_im[l], ssm_b_re[l], ssm_b_im[l], ssm_c_re[l],
                          ssm_c_im[l], ssm_d[l], ssm_log_step[l], ssm_w_glu[l], ssm_b_glu[l])

        def attend_p(ag, akt, avt, ikt, aqb, akb, avtb, iqb, ikb, iwt):
            return _dsa_prompt(bp, tp, ikb, akb, avtb, iqb, iwt, aqb, ag)

        hp, st_p = _mixer_layer(hp, pos_p, zero_h, zero_h, zero_s, attend_p, norm_g[l], w_in[l], w_out[l],
                                prm, ret_norm_g[l], final_norm_g, final, tm=256, tl=min(512, tp))

        def attend_s(ag, akt, avt, ikt, aqb, akb, avtb, iqb, ikb, iwt):
            return _dsa_sample(page_table, k_t, v_t, kidx_t, l, ts, akt[0], avt[0], ikt[0], ag,
                               aqb, iqb, iwt[0])

        hs, st_s = _mixer_layer(hs, pos_s, state_ssm_re[l].reshape(bs, 1, N_STATE),
                                state_ssm_im[l].reshape(bs, 1, N_STATE), state_ret[l], attend_s,
                                norm_g[l], w_in[l], w_out[l], prm, ret_norm_g[l], final_norm_g, final,
                                tm=256, tl=ts)
        outs_p.append(st_p)
        outs_s.append(st_s)

    def pack(outs, b, t):
        k = jnp.stack([o[0] for o in outs]).reshape(depth, b, t, ATT_HEADS, HEAD_DIM)
        v = jnp.stack([o[1] for o in outs]).reshape(depth, b, t, ATT_HEADS, HEAD_DIM)
        ki = jnp.stack([o[2] for o in outs])
        hr = jnp.stack([o[3] for o in outs]).reshape(depth, b, SSM_GROUPS, SSM_STATE)
        hi = jnp.stack([o[4] for o in outs]).reshape(depth, b, SSM_GROUPS, SSM_STATE)
        s = jnp.stack([o[5] for o in outs])
        return k, v, ki, hr, hi, s

    return (hp, hs) + pack(outs_p, bp, tp) + pack(outs_s, bs, ts)
```

```python
import functools
import math

import jax
import jax.numpy as jnp
import numpy as np
from jax import lax
from jax.experimental import pallas as pl
from jax.experimental.pallas import tpu as pltpu

F32 = jnp.float32
BF16 = jnp.bfloat16
I32 = jnp.int32

D_MODEL = 1024
HEAD_DIM = 64
SSM_GROUP = 16
D_SSM = 256
SSM_GROUPS = D_SSM // SSM_GROUP
SSM_STATE = 64
N_STATE = SSM_GROUPS * SSM_STATE
D_RET = 384
RET_HEADS = D_RET // HEAD_DIM
RET_CHUNK = 128
ROPE_BASE = 10000.0
D_ATT = 384
ATT_HEADS = D_ATT // HEAD_DIM
IDX_HEADS = 4
IDX_DIM = 64
TOPK_MAX = 256
Q_BLOCK = 128
PAGE_SIZE = 128
NORM_EPS = 1e-6
NEG_INF = -1e30

_SIZES = (D_SSM, D_SSM, D_RET, D_RET, D_RET, D_RET, D_ATT, D_ATT, D_ATT, D_ATT,
          IDX_HEADS * IDX_DIM, IDX_DIM, IDX_HEADS)
_OFF = np.concatenate([[0], np.cumsum(_SIZES)]).tolist()
C_U, C_RET, C_AQ, C_AK, C_AV, C_AG, C_IQ, C_IK, C_IW, C_END = (
    _OFF[0], _OFF[2], _OFF[6], _OFF[7], _OFF[8], _OFF[9], _OFF[10], _OFF[11], _OFF[12], _OFF[13])

LANES = 128
SUBLANES = 8
F32_INF_BITS = 0x7F800000
LOG2_E = 1.4426950408889634
V_ROWS = HEAD_DIM + 16
VMEM_LIMIT = 56 * 1024 * 1024

_NT = (((1,), (1,)), ((), ()))


def _cparams(n_axes):
    return pltpu.CompilerParams(dimension_semantics=("arbitrary",) * n_axes,
                                vmem_limit_bytes=VMEM_LIMIT)


_T_AK, _T_AV, _T_IK, _T_IW, _T_END = 0, D_ATT, 2 * D_ATT, 2 * D_ATT + IDX_DIM, 2 * D_ATT + IDX_DIM + SUBLANES


def _inproj_kernel(x_ref, g_ref, wm_ref, wt_ref,
                   u2_ref, r4_ref, ag_ref, akt_ref, avt_ref, ikt_ref,
                   aqb_ref, akb_ref, avtb_ref, iqb_ref, ikb_ref, iwt_ref):
    x = x_ref[...]
    ms = jnp.mean(x * x, axis=-1, keepdims=True)
    xn = (x * lax.rsqrt(ms + NORM_EPS) * g_ref[...]).astype(BF16)

    def seg(a, b):
        return jnp.dot(xn, wm_ref[:, a:b], preferred_element_type=F32)

    u2_ref[...] = seg(C_U, C_RET)
    r4_ref[...] = seg(C_RET, C_AQ)
    aqb_ref[...] = (seg(C_AQ, C_AK) * (LOG2_E * HEAD_DIM ** -0.5)).astype(BF16)
    akb_ref[...] = seg(C_AK, C_AV).astype(BF16)
    ag_ref[...] = seg(C_AG, C_IQ)
    iqb_ref[...] = seg(C_IQ, C_IK).astype(BF16)
    ikb_ref[...] = seg(C_IK, C_IW).astype(BF16)
    zt = lax.dot_general(wt_ref[...], xn, _NT, preferred_element_type=F32)
    akt_ref[...] = zt[_T_AK:_T_AV]
    avt = zt[_T_AV:_T_IK]
    avt_ref[...] = avt
    ones = jnp.ones((V_ROWS - HEAD_DIM, avt.shape[1]), BF16)
    for h in range(ATT_HEADS):
        avtb_ref[h * V_ROWS:h * V_ROWS + HEAD_DIM, :] = avt[h * HEAD_DIM:(h + 1) * HEAD_DIM].astype(BF16)
        avtb_ref[h * V_ROWS + HEAD_DIM:(h + 1) * V_ROWS, :] = ones
    ikt_ref[...] = zt[_T_IK:_T_IW]
    iwt_ref[...] = zt[_T_IW:_T_END] * (IDX_HEADS ** -0.5 * IDX_DIM ** -0.5)


def _inproj(x, bv, tv, g, wm, wt, tm):
    n = x.shape[0]
    nt = tv // tm
    row = lambda w: pl.BlockSpec((tm, w), lambda i: (i, 0))
    col = lambda h: pl.BlockSpec((None, h, tm), lambda i: (i // nt, 0, i % nt))
    full = lambda a: pl.BlockSpec(a.shape, lambda i: (0,) * a.ndim)
    out_shape = (
        jax.ShapeDtypeStruct((n, C_RET - C_U), F32),
        jax.ShapeDtypeStruct((n, C_AQ - C_RET), F32),
        jax.ShapeDtypeStruct((n, D_ATT), F32),
        jax.ShapeDtypeStruct((bv, D_ATT, tv), F32),
        jax.ShapeDtypeStruct((bv, D_ATT, tv), F32),
        jax.ShapeDtypeStruct((bv, IDX_DIM, tv), F32),
        jax.ShapeDtypeStruct((n, D_ATT), BF16),
        jax.ShapeDtypeStruct((n, D_ATT), BF16),
        jax.ShapeDtypeStruct((bv, ATT_HEADS * V_ROWS, tv), BF16),
        jax.ShapeDtypeStruct((n, IDX_HEADS * IDX_DIM), BF16),
        jax.ShapeDtypeStruct((n, IDX_DIM), BF16),
        jax.ShapeDtypeStruct((bv, SUBLANES, tv), F32),
    )
    out_specs = (
        row(C_RET - C_U), row(C_AQ - C_RET), row(D_ATT), col(D_ATT), col(D_ATT), col(IDX_DIM),
        row(D_ATT), row(D_ATT), col(ATT_HEADS * V_ROWS), row(IDX_HEADS * IDX_DIM),
        row(IDX_DIM), col(SUBLANES))
    return pl.pallas_call(
        _inproj_kernel,
        grid=(n // tm,),
        in_specs=[row(D_MODEL), full(g), full(wm), full(wt)],
        out_specs=out_specs,
        out_shape=out_shape,
        compiler_params=_cparams(1),
        name="inproj",
    )(x, g, wm, wt)


def _outproj_kernel(x_ref, ys_ref, yr_ref, ya_ref, wo_ref, gf_ref, o_ref, *, final):
    y = x_ref[...]
    y = y + jnp.dot(ys_ref[...], wo_ref[0:D_SSM, :], preferred_element_type=F32)
    y = y + jnp.dot(yr_ref[...], wo_ref[D_SSM:D_SSM + D_RET, :], preferred_element_type=F32)
    y = y + jnp.dot(ya_ref[...], wo_ref[D_SSM + D_RET:, :], preferred_element_type=F32)
    if final:
        ms = jnp.mean(y * y, axis=-1, keepdims=True)
        y = y * lax.rsqrt(ms + NORM_EPS) * gf_ref[...]
    o_ref[...] = y


def _outproj(x, ys, yr, ya, wo, gf, tm, final):
    n = x.shape[0]
    row = lambda w: pl.BlockSpec((tm, w), lambda i: (i, 0))
    full = lambda a: pl.BlockSpec(a.shape, lambda i: (0,) * a.ndim)
    return pl.pallas_call(
        functools.partial(_outproj_kernel, final=final),
        grid=(n // tm,),
        in_specs=[row(D_MODEL), row(D_SSM), row(D_RET), row(D_ATT), full(wo), full(gf)],
        out_specs=row(D_MODEL),
        out_shape=jax.ShapeDtypeStruct((n, D_MODEL), F32),
        compiler_params=_cparams(1),
        name="outproj",
    )(x, ys, yr, ya, wo, gf)


def _cmul(ar, ai, br, bi):
    return ar * br - ai * bi, ar * bi + ai * br


def _ssm_kernel(u2_ref, h0r_ref, h0i_ref, lr_ref, li_ref, ls_ref, btr_ref, bti_ref,
                ctr_ref, cti_ref, d_ref, wg_ref, bg_ref,
                y_ref, hr_out_ref, hi_out_ref,
                bbr_ref, bbi_ref, apr_ref, api_ref, cr_ref, ci_ref, hr_ref, hi_ref, *, tl):
    b = pl.program_id(0)
    c = pl.program_id(1)
    nc = pl.num_programs(1)

    @pl.when((b == 0) & (c == 0))
    def _prep():
        lr, li = lr_ref[...], li_ref[...]
        dt = jnp.exp(ls_ref[...])
        mag = jnp.exp(lr * dt)
        abr, abi = mag * jnp.cos(li * dt), mag * jnp.sin(li * dt)
        den = lr * lr + li * li
        nr, ni = abr - 1.0, abi
        fr = (nr * lr + ni * li) / den
        fi = (ni * lr - nr * li) / den
        bbr_ref[...] = (fr * btr_ref[...] - fi * bti_ref[...]).astype(BF16)
        bbi_ref[...] = (fr * bti_ref[...] + fi * btr_ref[...]).astype(BF16)
        pr, pi = abr, abi
        rows_r, rows_i = [pr], [pi]
        for _ in range(SUBLANES - 1):
            pr, pi = _cmul(pr, pi, abr, abi)
            rows_r.append(pr)
            rows_i.append(pi)
        apr_ref[...] = jnp.concatenate(rows_r, axis=0)
        api_ref[...] = jnp.concatenate(rows_i, axis=0)

    @pl.when(c == 0)
    def _init():
        cr_ref[...] = h0r_ref[...]
        ci_ref[...] = h0i_ref[...]

    u = u2_ref[:, 0:D_SSM]
    gate = u2_ref[:, D_SSM:2 * D_SSM]
    ub = u.astype(BF16)
    hr_ref[...] = jnp.dot(ub, bbr_ref[...], preferred_element_type=F32)
    hi_ref[...] = jnp.dot(ub, bbi_ref[...], preferred_element_type=F32)

    apr, api = apr_ref[...], api_ref[...]
    rowid = lax.broadcasted_iota(I32, (SUBLANES, N_STATE), 0)

    def blk(j, carry):
        cr, ci = carry
        r0 = pl.multiple_of(j * SUBLANES, SUBLANES)
        br = hr_ref[pl.ds(r0, SUBLANES), :]
        bi = hi_ref[pl.ds(r0, SUBLANES), :]
        for k in (1, 2, 4):
            keep = rowid >= k
            sr = jnp.where(keep, pltpu.roll(br, k, 0), 0.0)
            si = jnp.where(keep, pltpu.roll(bi, k, 0), 0.0)
            akr, aki = apr[k - 1:k, :], api[k - 1:k, :]
            br, bi = br + (akr * sr - aki * si), bi + (akr * si + aki * sr)
        hr = br + (apr * cr - api * ci)
        hi = bi + (apr * ci + api * cr)
        hr_ref[pl.ds(r0, SUBLANES), :] = hr
        hi_ref[pl.ds(r0, SUBLANES), :] = hi
        return hr[SUBLANES - 1:SUBLANES, :], hi[SUBLANES - 1:SUBLANES, :]

    cr, ci = lax.fori_loop(0, tl // SUBLANES, blk, (cr_ref[...], ci_ref[...]))
    cr_ref[...] = cr
    ci_ref[...] = ci

    @pl.when(c == nc - 1)
    def _fin():
        hr_out_ref[...] = cr
        hi_out_ref[...] = ci

    y = (jnp.dot(hr_ref[...].astype(BF16), ctr_ref[...], preferred_element_type=F32)
         - jnp.dot(hi_ref[...].astype(BF16), cti_ref[...], preferred_element_type=F32)
         + d_ref[...] * u)
    g = jax.nn.gelu(y)
    y = g * jax.nn.sigmoid(jnp.dot(g.astype(BF16), wg_ref[...], preferred_element_type=F32) + bg_ref[...])
    y_ref[...] = (y * (gate * jax.nn.sigmoid(gate))).astype(BF16)


def _ssm(u2, h0r, h0i, prm, tl):
    bv, tv, _ = u2.shape
    full = lambda a: pl.BlockSpec(a.shape, lambda b, c: (0,) * a.ndim)
    st = pl.BlockSpec((None, 1, N_STATE), lambda b, c: (b, 0, 0))
    return pl.pallas_call(
        functools.partial(_ssm_kernel, tl=tl),
        grid=(bv, tv // tl),
        in_specs=[pl.BlockSpec((None, tl, 2 * D_SSM), lambda b, c: (b, c, 0)), st, st]
        + [full(a) for a in prm],
        out_specs=(pl.BlockSpec((None, tl, D_SSM), lambda b, c: (b, c, 0)), st, st),
        out_shape=(jax.ShapeDtypeStruct((bv, tv, D_SSM), BF16),
                   jax.ShapeDtypeStruct((bv, 1, N_STATE), F32),
                   jax.ShapeDtypeStruct((bv, 1, N_STATE), F32)),
        scratch_shapes=[pltpu.VMEM((D_SSM, N_STATE), BF16), pltpu.VMEM((D_SSM, N_STATE), BF16),
                        pltpu.VMEM((SUBLANES, N_STATE), F32), pltpu.VMEM((SUBLANES, N_STATE), F32),
                        pltpu.VMEM((1, N_STATE), F32), pltpu.VMEM((1, N_STATE), F32),
                        pltpu.VMEM((tl, N_STATE), F32), pltpu.VMEM((tl, N_STATE), F32)],
        compiler_params=_cparams(2),
        name="ssm",
    )(u2, h0r, h0i, *prm)


def _ssm_params(lam_re, lam_im, b_re, b_im, c_re, c_im, d_skip, log_step, w_glu, b_glu):
    eye = jnp.eye(SSM_GROUPS, dtype=F32)

    def bdiag(m):
        g, r, c = m.shape
        return (eye[:, None, :, None] * m[:, :, None, :]).reshape(g * r, g * c)

    flat = lambda a: a.astype(F32).reshape(1, N_STATE)
    ls = jnp.broadcast_to(log_step.astype(F32)[:, None], (SSM_GROUPS, SSM_STATE))
    btr = bdiag(jnp.swapaxes(b_re.astype(F32), 1, 2))
    bti = bdiag(jnp.swapaxes(b_im.astype(F32), 1, 2))
    ctr = bdiag(jnp.swapaxes(c_re.astype(F32), 1, 2)).astype(BF16)
    cti = bdiag(jnp.swapaxes(c_im.astype(F32), 1, 2)).astype(BF16)
    return (flat(lam_re), flat(lam_im), flat(ls), btr, bti, ctr, cti,
            d_skip.astype(F32).reshape(1, D_SSM), w_glu.astype(BF16),
            b_glu.astype(F32).reshape(1, D_SSM))


def _swap_halves(x):
    lane = lax.broadcasted_iota(I32, (x.shape[0], LANES), 1)
    first = (lane % HEAD_DIM) < (HEAD_DIM // 2)
    tiles = []
    for t in range(x.shape[1] // LANES):
        xt = x[:, t * LANES:(t + 1) * LANES]
        up = pltpu.roll(xt, LANES - HEAD_DIM // 2, 1)
        dn = pltpu.roll(xt, HEAD_DIM // 2, 1)
        tiles.append(jnp.where(first, up, dn))
    return jnp.concatenate(tiles, axis=1)


def _ret_kernel(r4_ref, s0_ref, cos_ref, sin_ref, din_ref, dq_ref, dk_ref, ds_ref, g_ref,
                y_ref, s_out_ref, s_ref):
    c = pl.program_id(1)
    nc = pl.num_programs(1)

    @pl.when(c == 0)
    def _init():
        s_ref[...] = s0_ref[...]

    q = r4_ref[:, 0:D_RET]
    k = r4_ref[:, D_RET:2 * D_RET]
    v = r4_ref[:, 2 * D_RET:3 * D_RET]
    gate = r4_ref[:, 3 * D_RET:4 * D_RET]
    cos = jnp.concatenate([cos_ref[...]] * (D_RET // LANES), axis=1)
    sin = jnp.concatenate([sin_ref[...]] * (D_RET // LANES), axis=1)
    qr = q * cos + _swap_halves(q) * sin
    kr = (k * cos + _swap_halves(k) * sin) * HEAD_DIM ** -0.5
    qb = qr.astype(BF16)
    kb = kr.astype(BF16)
    vb = v.astype(BF16)
    kdt = (kr * dk_ref[...]).T.astype(BF16)
    dq = dq_ref[...]
    gsil = gate * jax.nn.sigmoid(gate) * g_ref[...]
    sls = [slice(h * HEAD_DIM, (h + 1) * HEAD_DIM) for h in range(RET_HEADS)]
    atts, crosses = [], []
    for h, sl in enumerate(sls):
        qh, kh, vh = qb[:, sl], kb[:, sl], vb[:, sl]
        s = s_ref[h]
        atts.append((lax.dot_general(qh, kh, _NT, preferred_element_type=F32) * din_ref[h]).astype(BF16))
        crosses.append(jnp.dot(qh, s.astype(BF16), preferred_element_type=F32) * dq[:, sl])
        s_ref[h] = s * ds_ref[h] + jnp.dot(kdt[sl, :], vh, preferred_element_type=F32)
    for h, sl in enumerate(sls):
        o = jnp.dot(atts[h], vb[:, sl], preferred_element_type=F32) + crosses[h]
        mu = jnp.mean(o, axis=-1, keepdims=True)
        var = jnp.mean(jnp.square(o - mu), axis=-1, keepdims=True)
        y_ref[:, sl] = ((o - mu) * lax.rsqrt(var + NORM_EPS) * gsil[:, sl]).astype(BF16)

    @pl.when(c == nc - 1)
    def _fin():
        s_out_ref[...] = s_ref[...]


def _ret_tables(pos, chunk):
    half = HEAD_DIM // 2
    inv = ROPE_BASE ** (-jnp.arange(half, dtype=F32) / half)
    ang = pos.astype(F32)[:, None] * inv[None, :]
    cos, sin = jnp.cos(ang), jnp.sin(ang)
    cos_t = jnp.concatenate([cos, cos] * (LANES // HEAD_DIM), axis=1)
    sin_t = jnp.concatenate([-sin, sin] * (LANES // HEAD_DIM), axis=1)
    log_g = jnp.log1p(-jnp.exp2(-5.0 - jnp.arange(RET_HEADS, dtype=F32)))
    j = jnp.arange(chunk, dtype=F32)
    rel = j[:, None] - j[None, :]
    din = jnp.where(rel[None] >= 0, jnp.exp(log_g[:, None, None] * jnp.maximum(rel, 0.0)[None]), 0.0)
    dq = jnp.repeat(jnp.exp(log_g[None, :] * (j[:, None] + 1.0)), HEAD_DIM, axis=1)
    dk = jnp.repeat(jnp.exp(log_g[None, :] * (chunk - 1.0 - j[:, None])), HEAD_DIM, axis=1)
    ds = jnp.broadcast_to(jnp.exp(log_g * chunk)[:, None, None], (RET_HEADS, HEAD_DIM, HEAD_DIM))
    return cos_t, sin_t, din, dq, dk, ds


def _ret(r4, s0, pos, norm_g):
    bv, tv, _ = r4.shape
    chunk = RET_CHUNK if tv % RET_CHUNK == 0 else tv
    cos_t, sin_t, din, dq, dk, ds = _ret_tables(pos, chunk)
    g = norm_g.astype(F32).reshape(1, D_RET)
    full = lambda a: pl.BlockSpec(a.shape, lambda b, c: (0,) * a.ndim)
    st = pl.BlockSpec((None, RET_HEADS, HEAD_DIM, HEAD_DIM), lambda b, c: (b, 0, 0, 0))
    tab = pl.BlockSpec((chunk, LANES), lambda b, c: (c, 0))
    return pl.pallas_call(
        _ret_kernel,
        grid=(bv, tv // chunk),
        in_specs=[pl.BlockSpec((None, chunk, 4 * D_RET), lambda b, c: (b, c, 0)), st, tab, tab,
                  full(din), full(dq), full(dk), full(ds), full(g)],
        out_specs=(pl.BlockSpec((None, chunk, D_RET), lambda b, c: (b, c, 0)), st),
        out_shape=(jax.ShapeDtypeStruct((bv, tv, D_RET), BF16),
                   jax.ShapeDtypeStruct((bv, RET_HEADS, HEAD_DIM, HEAD_DIM), F32)),
        scratch_shapes=[pltpu.VMEM((RET_HEADS, HEAD_DIM, HEAD_DIM), F32)],
        compiler_params=_cparams(2),
        name="retention",
    )(r4, s0, cos_t, sin_t, din, dq, dk, ds, g)


def _count(sc_ref, nk, kb, pred):
    n_acc = 4

    def body(c, cnts):
        r0 = pl.multiple_of(c * kb, kb)
        cnts = list(cnts)
        for j in range(kb // SUBLANES):
            blk = sc_ref[pl.ds(r0 + j * SUBLANES, SUBLANES), :]
            cnts[j % n_acc] = cnts[j % n_acc] + jnp.where(pred(blk), 1, 0)
        return tuple(cnts)
    zero = jnp.zeros((SUBLANES, LANES), I32)
    cnts = lax.fori_loop(0, nk, body, (zero,) * n_acc)
    return jnp.sum((cnts[0] + cnts[1]) + (cnts[2] + cnts[3]), axis=0, keepdims=True)


_ORD_FLIP = 0x7FFFFFFF
_ORD_POS_INF = F32_INF_BITS
_ORD_NEG_INF = -F32_INF_BITS - 1


def _unord(o):
    return lax.bitcast_convert_type(jnp.where(o >= 0, o, o ^ _ORD_FLIP), F32)


def _select_bias(sc_ref, bias_ref, tri_ref, nk, kb, n_beyond, qpos, topk):
    def count_ge(t):
        return _count(sc_ref, nk, kb, lambda blk: blk >= t) + jnp.where(t <= NEG_INF, n_beyond, 0)

    c_zero = count_ge(jnp.zeros((1, LANES), F32))
    pos = jnp.where(c_zero >= topk, 1, 0)
    lo = jnp.where(pos == 1, 0, _ORD_NEG_INF)
    hi = jnp.where(pos == 1, _ORD_POS_INF + 1, 0)
    c_lo = jnp.where(pos == 1, c_zero, nk * kb + n_beyond)
    n_steps = 31

    def step(_, carry):
        lo, hi, c_lo = carry
        mid = lo + lax.shift_right_arithmetic(hi - lo, 1)
        c = count_ge(_unord(mid))
        ok = c >= topk
        return jnp.where(ok, mid, lo), jnp.where(ok, hi, mid), jnp.where(ok, c, c_lo)

    lo, _, c_lo = lax.fori_loop(0, n_steps, step, (lo, hi, c_lo))
    tau = _unord(lo)
    surplus = c_lo - topk
    surplus_ties = jnp.max(surplus) > 0

    @pl.when(jnp.logical_not(surplus_ties))
    def _no_surplus():
        def body(c, _):
            r0 = pl.multiple_of(c * kb, kb)
            blk = sc_ref[pl.ds(r0, kb), :]
            kidx = r0 + lax.broadcasted_iota(I32, (kb, LANES), 0)
            bias_ref[pl.ds(r0, kb), :] = jnp.where((blk >= tau) & (kidx <= qpos), 0.0, NEG_INF)
            return 0
        lax.fori_loop(0, nk, body, 0)

    @pl.when(surplus_ties)
    def _ties_by_index():
        surplus_f = surplus.astype(F32)
        above = jnp.where(tau == NEG_INF, n_beyond, 0).astype(F32)

        def body(c, above):
            r0 = pl.multiple_of((nk - 1 - c) * kb, kb)
            for b in reversed(range(kb // TIE_BLOCK)):
                rb = r0 + b * TIE_BLOCK
                blk = sc_ref[pl.ds(rb, TIE_BLOCK), :]
                eq = blk == tau
                eq01 = jnp.where(eq, 1.0, 0.0)
                at_or_above = jnp.dot(tri_ref[...], eq01.astype(BF16), preferred_element_type=F32) + above
                kidx = rb + lax.broadcasted_iota(I32, (TIE_BLOCK, LANES), 0)
                sel = ((blk > tau) | (eq & (at_or_above > surplus_f))) & (kidx <= qpos)
                bias_ref[pl.ds(rb, TIE_BLOCK), :] = jnp.where(sel, 0.0, NEG_INF)
                above = above + jnp.sum(eq01, axis=0, keepdims=True)
            return above

        lax.fori_loop(0, nk, body, above)


TIE_BLOCK = 128


def _tri():
    r = np.arange(TIE_BLOCK)
    return jnp.asarray((r[None, :] >= r[:, None]).astype(np.float32), dtype=BF16)


def _dsa_prompt_kernel(ikb_ref, akb_ref, avtb_ref, iqb_ref, iwt_ref, aqb_ref, ag_ref, tri_ref,
                       y_ref, sc_ref, bias_ref, lg_ref, *o_refs, kb, ka, t_total, topk):
    i = pl.program_id(1)
    q0 = i * Q_BLOCK
    nk = (q0 + Q_BLOCK + kb - 1) // kb
    qpos = q0 + lax.broadcasted_iota(I32, (1, LANES), 1)
    iq = iqb_ref[...]
    iw = iwt_ref[...]
    qcat = jnp.concatenate([iq[:, h * IDX_DIM:(h + 1) * IDX_DIM] for h in range(IDX_HEADS)], axis=0)

    def score_body(c, _):
        r0 = pl.multiple_of(c * kb, kb)
        keys = ikb_ref[pl.ds(r0, kb), :]
        s = lax.dot_general(keys, qcat, _NT, preferred_element_type=F32)
        parts = [jnp.maximum(s[:, h * LANES:(h + 1) * LANES], 0.0) * iw[h:h + 1, :]
                 for h in range(IDX_HEADS)]
        acc = (parts[0] + parts[1]) + (parts[2] + parts[3])
        kidx = r0 + lax.broadcasted_iota(I32, (kb, LANES), 0)
        sc_ref[pl.ds(r0, kb), :] = jnp.where(kidx <= qpos, acc, NEG_INF)
        return 0

    lax.fori_loop(0, nk, score_body, 0)
    _select_bias(sc_ref, bias_ref, tri_ref, nk, kb, t_total - nk * kb, qpos, topk)

    aq = aqb_ref[...]
    lane = lax.broadcasted_iota(I32, (Q_BLOCK, LANES), 1)
    qpairs = []
    for j in range(ATT_HEADS // 2):
        qp = aq[:, j * LANES:(j + 1) * LANES]
        zero = jnp.zeros_like(qp)
        qpairs.append(jnp.concatenate([jnp.where(lane < HEAD_DIM, qp, zero),
                                       jnp.where(lane >= HEAD_DIM, qp, zero)], axis=0))
    for o_ref in o_refs:
        o_ref[...] = jnp.zeros(o_ref.shape, F32)

    def att_body(c, ms):
        r0 = pl.multiple_of(c * ka, ka)
        bias = bias_ref[pl.ds(r0, ka), :]
        ms_new = []
        mx = []
        for j in range(ATT_HEADS // 2):
            kpair = akb_ref[pl.ds(r0, ka), j * LANES:(j + 1) * LANES]
            both = lax.dot_general(kpair, qpairs[j], _NT, preferred_element_type=F32)
            for i in range(2):
                logit = both[:, i * LANES:(i + 1) * LANES] + bias
                lg_ref[2 * j + i] = logit
                mx.append(jnp.max(logit, axis=0, keepdims=True))
        for h in range(ATT_HEADS):
            m_new = jnp.maximum(ms[h], mx[h])
            alpha = jnp.exp2(ms[h] - m_new)
            p = jnp.exp2(lg_ref[h] - m_new)
            ms_new.append(m_new)
            vt = avtb_ref[h * V_ROWS:(h + 1) * V_ROWS, pl.ds(r0, ka)]
            o_refs[h][...] = alpha * o_refs[h][...] + jnp.dot(vt, p.astype(BF16), preferred_element_type=F32)
        return tuple(ms_new)

    m0 = (jnp.full((1, LANES), NEG_INF, F32),) * ATT_HEADS
    lax.fori_loop(0, nk * (kb // ka), att_body, m0)
    ot = jnp.concatenate([o_refs[h][0:HEAD_DIM] / o_refs[h][HEAD_DIM:HEAD_DIM + 1]
                          for h in range(ATT_HEADS)], axis=0)
    gate = ag_ref[...]
    y_ref[...] = (ot.T * (gate * jax.nn.sigmoid(gate))).astype(BF16)


def _dsa_prompt(bv, tv, ikb, akb, avtb, iqb, iwt, aqb, ag, kb=512, ka=512):
    assert ATT_HEADS % 2 == 0 and 2 * HEAD_DIM == LANES
    topk = min(TOPK_MAX, tv // 4)
    kb = min(kb, tv)
    nq = tv // Q_BLOCK
    tri = _tri()
    per_b = lambda w: pl.BlockSpec((tv, w), lambda b, i: (b, 0))
    qrow = lambda w: pl.BlockSpec((Q_BLOCK, w), lambda b, i: (b * nq + i, 0))
    return pl.pallas_call(
        functools.partial(_dsa_prompt_kernel, kb=kb, ka=min(ka, kb), t_total=tv, topk=topk),
        grid=(bv, nq),
        in_specs=[per_b(IDX_DIM), per_b(D_ATT),
                  pl.BlockSpec((None, ATT_HEADS * V_ROWS, tv), lambda b, i: (b, 0, 0)),
                  qrow(IDX_HEADS * IDX_DIM),
                  pl.BlockSpec((None, SUBLANES, Q_BLOCK), lambda b, i: (b, 0, i)),
                  qrow(D_ATT), qrow(D_ATT),
                  pl.BlockSpec(tri.shape, lambda b, i: (0, 0))],
        out_specs=qrow(D_ATT),
        out_shape=jax.ShapeDtypeStruct((bv * tv, D_ATT), BF16),
        scratch_shapes=[pltpu.VMEM((tv, LANES), F32), pltpu.VMEM((tv, LANES), F32)]
        + [pltpu.VMEM((ATT_HEADS, min(ka, kb), LANES), F32)]
        + [pltpu.VMEM((V_ROWS, LANES), F32)] * ATT_HEADS,
        compiler_params=_cparams(2),
        name="dsa_prompt",
    )(ikb, akb, avtb, iqb, iwt, aqb, ag, tri)


def _samp_score_kernel(pt_ref, *refs, npg, ts):
    del pt_ref
    pages = refs[:npg]
    qi_ref, w_ref, iknew_ref, sc_ref, scn_ref = refs[npg:]
    qi = qi_ref[...]
    w = w_ref[...]

    def score(keys_t):
        s = jnp.dot(qi, keys_t.astype(BF16), preferred_element_type=F32)
        r = jnp.maximum(s, 0.0) * w
        return (r[0:ts] + r[ts:2 * ts]) + (r[2 * ts:3 * ts] + r[3 * ts:4 * ts])

    for i in range(npg):
        sc_ref[:, i * PAGE_SIZE:(i + 1) * PAGE_SIZE] = score(pages[i][...])

    @pl.when(pl.program_id(1) == 0)
    def _new_keys():
        sn = score(iknew_ref[...])
        col = lax.broadcasted_iota(I32, (ts, PAGE_SIZE), 1)
        row = lax.broadcasted_iota(I32, (ts, PAGE_SIZE), 0)
        scn_ref[...] = jnp.where(col <= row, sn, NEG_INF)


def _samp_scores(page_table, kidx_t, layer, qi, w, iknew_t, npg):
    bs, n_pages = page_table.shape
    ts = qi.shape[1] // IDX_HEADS
    page_spec = lambda i: pl.BlockSpec(
        (None, None, IDX_DIM, PAGE_SIZE), lambda b, j, pt, i=i: (layer, pt[b, j * npg + i], 0, 0))
    per_b = lambda a: pl.BlockSpec((None,) + a.shape[1:], lambda b, j, pt: (b, 0, 0))
    gs = pltpu.PrefetchScalarGridSpec(
        num_scalar_prefetch=1,
        grid=(bs, n_pages // npg),
        in_specs=[page_spec(i) for i in range(npg)] + [per_b(qi), per_b(w), per_b(iknew_t)],
        out_specs=(pl.BlockSpec((ts, npg * PAGE_SIZE), lambda b, j, pt: (b, j)),
                   pl.BlockSpec((ts, PAGE_SIZE), lambda b, j, pt: (b, 0))),
    )
    return pl.pallas_call(
        functools.partial(_samp_score_kernel, npg=npg, ts=ts),
        grid_spec=gs,
        out_shape=(jax.ShapeDtypeStruct((bs * ts, n_pages * PAGE_SIZE), F32),
                   jax.ShapeDtypeStruct((bs * ts, PAGE_SIZE), F32)),
        compiler_params=_cparams(2),
        name="sample_scores",
    )(page_table, *([kidx_t] * npg), qi, w, iknew_t)


def _select_kernel(sc_ref, qpos_ref, tri_ref, bias_ref, *, kb, nk, topk):
    _select_bias(sc_ref, bias_ref, tri_ref, nk, kb, 0, qpos_ref[...], topk)


def _select(sc_t, qpos, topk, kb):
    nkeys, nq = sc_t.shape
    tri = _tri()
    col = pl.BlockSpec((nkeys, LANES), lambda i: (0, i))
    return pl.pallas_call(
        functools.partial(_select_kernel, kb=kb, nk=nkeys // kb, topk=topk),
        grid=(nq // LANES,),
        in_specs=[col, pl.BlockSpec((1, LANES), lambda i: (0, i)),
                  pl.BlockSpec(tri.shape, lambda i: (0, 0))],
        out_specs=col,
        out_shape=jax.ShapeDtypeStruct((nkeys, nq), F32),
        compiler_params=_cparams(1),
        name="sample_select",
    )(sc_t, qpos, tri)


def _samp_att_kernel(pt_ref, *refs, npg, ts):
    del pt_ref
    kp = refs[:npg]
    vp = refs[npg:2 * npg]
    q_ref, bias_ref, biasn_ref, kn_ref, vn_ref, ag_ref, y_ref, m_ref, l_ref, acc_ref = refs[2 * npg:]
    j = pl.program_id(1)
    nj = pl.num_programs(1)

    @pl.when(j == 0)
    def _init():
        m_ref[...] = jnp.full(m_ref.shape, NEG_INF, F32)
        l_ref[...] = jnp.zeros(l_ref.shape, F32)
        acc_ref[...] = jnp.zeros(acc_ref.shape, F32)

    q = q_ref[...]

    def update(kts, vts, bias):
        logit = jnp.concatenate(
            [jnp.dot(q, kt.astype(BF16), preferred_element_type=F32) for kt in kts], axis=1)
        logit = logit + jnp.concatenate([bias] * ATT_HEADS, axis=0)
        m_old = m_ref[...]
        m_new = jnp.maximum(m_old, jnp.max(logit, axis=1, keepdims=True))
        alpha = jnp.exp2(m_old - m_new)
        p = jnp.exp2(logit - m_new)
        l_ref[...] = alpha * l_ref[...] + jnp.sum(p, axis=1, keepdims=True)
        pb = p.astype(BF16)
        pvs = [lax.dot_general(pb[:, i * PAGE_SIZE:(i + 1) * PAGE_SIZE], vts[i].astype(BF16), _NT,
                               preferred_element_type=F32) for i in range(len(vts))]
        while len(pvs) > 1:
            pvs = [pvs[i] + pvs[i + 1] for i in range(0, len(pvs) - 1, 2)] + pvs[len(pvs) - len(pvs) % 2:]
        acc_ref[...] = alpha * acc_ref[...] + pvs[0]
        m_ref[...] = m_new

    update([r[...] for r in kp], [r[...] for r in vp], bias_ref[...])

    @pl.when(j == nj - 1)
    def _fin():
        update([kn_ref[...]], [vn_ref[...]], biasn_ref[...])
        o = acc_ref[...] / l_ref[...]
        gate = ag_ref[...]
        gsil = gate * jax.nn.sigmoid(gate)
        for h in range(ATT_HEADS):
            sl = slice(h * HEAD_DIM, (h + 1) * HEAD_DIM)
            y_ref[:, sl] = (o[h * ts:(h + 1) * ts, sl] * gsil[:, sl]).astype(BF16)


def _samp_attention(page_table, k_t, v_t, layer, qbd, bias, biasn, knew_t, vnew_t, ag, npg):
    bs, n_pages = page_table.shape
    ts = qbd.shape[1] // ATT_HEADS
    page_spec = lambda i: pl.BlockSpec(
        (None, None, D_ATT, PAGE_SIZE), lambda b, j, pt, i=i: (layer, pt[b, j * npg + i], 0, 0))
    per_b = lambda a: pl.BlockSpec((None,) + a.shape[1:], lambda b, j, pt: (b, 0, 0))
    gs = pltpu.PrefetchScalarGridSpec(
        num_scalar_prefetch=1,
        grid=(bs, n_pages // npg),
        in_specs=[page_spec(i) for i in range(npg)] * 2
        + [per_b(qbd),
           pl.BlockSpec((ts, npg * PAGE_SIZE), lambda b, j, pt: (b, j)),
           pl.BlockSpec((ts, PAGE_SIZE), lambda b, j, pt: (b, 0)),
           per_b(knew_t), per_b(vnew_t),
           pl.BlockSpec((ts, D_ATT), lambda b, j, pt: (b, 0))],
        out_specs=pl.BlockSpec((ts, D_ATT), lambda b, j, pt: (b, 0)),
        scratch_shapes=[pltpu.VMEM((ATT_HEADS * ts, 1), F32), pltpu.VMEM((ATT_HEADS * ts, 1), F32),
                        pltpu.VMEM((ATT_HEADS * ts, D_ATT), F32)],
    )
    return pl.pallas_call(
        functools.partial(_samp_att_kernel, npg=npg, ts=ts),
        grid_spec=gs,
        out_shape=jax.ShapeDtypeStruct((bs * ts, D_ATT), BF16),
        compiler_params=_cparams(2),
        name="sample_attention",
    )(page_table, *([k_t] * npg), *([v_t] * npg), qbd, bias, biasn, knew_t, vnew_t, ag)


def _dsa_sample(page_table, k_t, v_t, kidx_t, layer, ts, akt, avt, ikt, ag, aqb, iqb, iwt):
    bs, n_pages = page_table.shape
    past = n_pages * PAGE_SIZE
    n = bs * ts
    topk = min(TOPK_MAX, (past + ts) // 4)
    npg = math.gcd(n_pages, 16)

    def new_keys_t(a):
        a = a.reshape(a.shape[0], bs, ts).transpose(1, 0, 2)
        return jnp.pad(a, ((0, 0), (0, 0), (0, PAGE_SIZE - ts)))

    qi = iqb.reshape(bs, ts, IDX_HEADS, IDX_DIM).transpose(0, 2, 1, 3).reshape(bs, IDX_HEADS * ts, IDX_DIM)
    w = iwt[:IDX_HEADS].reshape(IDX_HEADS, bs, ts).transpose(1, 0, 2).reshape(bs, IDX_HEADS * ts, 1)
    w = jnp.broadcast_to(w, (bs, IDX_HEADS * ts, LANES))
    sc, scn = _samp_scores(page_table, kidx_t, layer, qi, w, new_keys_t(ikt), npg)
    sc_t = jnp.concatenate([sc, scn], axis=1).T
    qpos = (past + jnp.arange(n, dtype=I32) % ts).reshape(1, n)
    bias_all = _select(sc_t, qpos, topk, PAGE_SIZE).T
    eye = jnp.eye(ATT_HEADS, dtype=BF16)
    q4 = aqb.reshape(bs, ts, ATT_HEADS, HEAD_DIM)
    qbd = (eye[None, :, None, :, None] * q4.transpose(0, 2, 1, 3)[:, :, :, None, :]).reshape(
        bs, ATT_HEADS * ts, D_ATT)
    return _samp_attention(page_table, k_t, v_t, layer, qbd, bias_all[:, :past], bias_all[:, past:],
                           new_keys_t(akt), new_keys_t(avt), ag, npg)


def _mixer_layer(x, pos, h0r, h0i, s0, attend, g, w_in_l, w_out_l, ssm_prm, ret_g, gf, final, tm, tl):
    bv, tv, _ = x.shape
    n = bv * tv
    wm = w_in_l[:, :C_IW].astype(BF16)
    wt = jnp.concatenate([w_in_l[:, C_AK:C_AG].T, w_in_l[:, C_IK:C_END].T,
                          jnp.zeros((SUBLANES - IDX_HEADS, D_MODEL), w_in_l.dtype)], axis=0).astype(BF16)
    xf = x.reshape(n, D_MODEL)
    pb, pt = (bv, tv) if tv % tm == 0 else (1, n)
    tm = min(tm, pt)
    (u2, r4, ag, akt, avt, ikt, aqb, akb, avtb, iqb, ikb, iwt) = _inproj(
        xf, pb, pt, g.astype(F32).reshape(1, D_MODEL), wm, wt, tm)
    y_ssm, hr, hi = _ssm(u2.reshape(bv, tv, -1), h0r, h0i, ssm_prm, tl)
    y_ret, s_new = _ret(r4.reshape(bv, tv, -1), s0, pos, ret_g)
    y_att = attend(ag, akt, avt, ikt, aqb, akb, avtb, iqb, ikb, iwt)
    y = _outproj(xf, y_ssm.reshape(n, D_SSM), y_ret.reshape(n, D_RET), y_att,
                 w_out_l.astype(BF16), gf.astype(F32).reshape(1, D_MODEL), tm, final)
    tok = lambda a: a.reshape(pb, a.shape[1], -1, tv).transpose(0, 2, 3, 1).reshape(bv, tv, a.shape[1])
    return y.reshape(bv, tv, D_MODEL), (tok(akt), tok(avt), tok(ikt), hr, hi, s_new)


def kernel(x_prompt, x_sample, cache_k, cache_v, cache_kidx, state_ssm_re, state_ssm_im, state_ret,
           page_table, norm_g, w_in, w_out, ssm_lambda_re, ssm_lambda_im, ssm_b_re, ssm_b_im,
           ssm_c_re, ssm_c_im, ssm_d, ssm_log_step, ssm_w_glu, ssm_b_glu, ret_norm_g, final_norm_g):
    bp, tp, _ = x_prompt.shape
    bs, ts, _ = x_sample.shape
    depth = w_in.shape[0]
    past = page_table.shape[1] * PAGE_SIZE
    assert ts == SUBLANES and tp % Q_BLOCK == 0
    pos_p = jnp.arange(tp)
    pos_s = past + jnp.arange(ts)
    k_t = jnp.transpose(cache_k, (0, 1, 3, 4, 2)).reshape(cache_k.shape[:2] + (D_ATT, PAGE_SIZE))
    v_t = jnp.transpose(cache_v, (0, 1, 3, 4, 2)).reshape(cache_v.shape[:2] + (D_ATT, PAGE_SIZE))
    kidx_t = jnp.transpose(cache_kidx, (0, 1, 3, 2))
    zero_h = jnp.zeros((bp, 1, N_STATE), F32)
    zero_s = jnp.zeros((bp, RET_HEADS, HEAD_DIM, HEAD_DIM), F32)
    hp, hs = x_prompt, x_sample
    outs_p, outs_s = [], []
    for l in range(depth):
        final = l == depth - 1
        prm = _ssm_params(ssm_lambda_re[l], ssm_lambda_im[l], ssm_b_re[l], ssm_b_im[l], ssm_c_re[l],
                          ssm_c_im[l], ssm_d[l], ssm_log_step[l], ssm_w_glu[l], ssm_b_glu[l])

        def attend_p(ag, akt, avt, ikt, aqb, akb, avtb, iqb, ikb, iwt):
            return _dsa_prompt(bp, tp, ikb, akb, avtb, iqb, iwt, aqb, ag)

        hp, st_p = _mixer_layer(hp, pos_p, zero_h, zero_h, zero_s, attend_p, norm_g[l], w_in[l], w_out[l],
                                prm, ret_norm_g[l], final_norm_g, final, tm=256, tl=min(512, tp))

        def attend_s(ag, akt, avt, ikt, aqb, akb, avtb, iqb, ikb, iwt):
            return _dsa_sample(page_table, k_t, v_t, kidx_t, l, ts, akt[0], avt[0], ikt[0], ag,
                               aqb, iqb, iwt[0])

        hs, st_s = _mixer_layer(hs, pos_s, state_ssm_re[l].reshape(bs, 1, N_STATE),
                                state_ssm_im[l].reshape(bs, 1, N_STATE), state_ret[l], attend_s,
                                norm_g[l], w_in[l], w_out[l], prm, ret_norm_g[l], final_norm_g, final,
                                tm=256, tl=ts)
        outs_p.append(st_p)
        outs_s.append(st_s)

    def pack(outs, b, t):
        k = jnp.stack([o[0] for o in outs]).reshape(depth, b, t, ATT_HEADS, HEAD_DIM)
        v = jnp.stack([o[1] for o in outs]).reshape(depth, b, t, ATT_HEADS, HEAD_DIM)
        ki = jnp.stack([o[2] for o in outs])
        hr = jnp.stack([o[3] for o in outs]).reshape(depth, b, SSM_GROUPS, SSM_STATE)
        hi = jnp.stack([o[4] for o in outs]).reshape(depth, b, SSM_GROUPS, SSM_STATE)
        s = jnp.stack([o[5] for o in outs])
        return k, v, ki, hr, hi, s

    return (hp, hs) + pack(outs_p, bp, tp) + pack(outs_s, bs, ts)
```

```python
import functools
import math

import jax
import jax.numpy as jnp
import numpy as np
from jax import lax
from jax.experimental import pallas as pl
from jax.experimental.pallas import tpu as pltpu

F32 = jnp.float32
BF16 = jnp.bfloat16
I32 = jnp.int32

D_MODEL = 1024
HEAD_DIM = 64
SSM_GROUP = 16
D_SSM = 256
SSM_GROUPS = D_SSM // SSM_GROUP
SSM_STATE = 64
N_STATE = SSM_GROUPS * SSM_STATE
D_RET = 384
RET_HEADS = D_RET // HEAD_DIM
RET_CHUNK = 128
ROPE_BASE = 10000.0
D_ATT = 384
ATT_HEADS = D_ATT // HEAD_DIM
IDX_HEADS = 4
IDX_DIM = 64
TOPK_MAX = 256
Q_BLOCK = 128
PAGE_SIZE = 128
NORM_EPS = 1e-6
NEG_INF = -1e30

_SIZES = (D_SSM, D_SSM, D_RET, D_RET, D_RET, D_RET, D_ATT, D_ATT, D_ATT, D_ATT,
          IDX_HEADS * IDX_DIM, IDX_DIM, IDX_HEADS)
_OFF = np.concatenate([[0], np.cumsum(_SIZES)]).tolist()
C_U, C_RET, C_AQ, C_AK, C_AV, C_AG, C_IQ, C_IK, C_IW, C_END = (
    _OFF[0], _OFF[2], _OFF[6], _OFF[7], _OFF[8], _OFF[9], _OFF[10], _OFF[11], _OFF[12], _OFF[13])

LANES = 128
SUBLANES = 8
F32_INF_BITS = 0x7F800000
LOG2_E = 1.4426950408889634
V_ROWS = HEAD_DIM + 16
VMEM_LIMIT = 56 * 1024 * 1024

_NT = (((1,), (1,)), ((), ()))


def _cparams(n_axes):
    return pltpu.CompilerParams(dimension_semantics=("arbitrary",) * n_axes,
                                vmem_limit_bytes=VMEM_LIMIT)


_T_AK, _T_AV, _T_IK, _T_IW, _T_END = 0, D_ATT, 2 * D_ATT, 2 * D_ATT + IDX_DIM, 2 * D_ATT + IDX_DIM + SUBLANES


def _inproj_kernel(x_ref, g_ref, wm_ref, wt_ref,
                   u2_ref, r4_ref, ag_ref, akt_ref, avt_ref, ikt_ref,
                   aqb_ref, akb_ref, avtb_ref, iqb_ref, ikb_ref, iwt_ref):
    x = x_ref[...]
    ms = jnp.mean(x * x, axis=-1, keepdims=True)
    xn = (x * lax.rsqrt(ms + NORM_EPS) * g_ref[...]).astype(BF16)

    def seg(a, b):
        return jnp.dot(xn, wm_ref[:, a:b], preferred_element_type=F32)

    u2_ref[...] = seg(C_U, C_RET)
    r4_ref[...] = seg(C_RET, C_AQ)
    aqb_ref[...] = (seg(C_AQ, C_AK) * (LOG2_E * HEAD_DIM ** -0.5)).astype(BF16)
    akb_ref[...] = seg(C_AK, C_AV).astype(BF16)
    ag_ref[...] = seg(C_AG, C_IQ)
    iqb_ref[...] = seg(C_IQ, C_IK).astype(BF16)
    ikb_ref[...] = seg(C_IK, C_IW).astype(BF16)
    zt = lax.dot_general(wt_ref[...], xn, _NT, preferred_element_type=F32)
    akt_ref[...] = zt[_T_AK:_T_AV]
    avt = zt[_T_AV:_T_IK]
    avt_ref[...] = avt
    ones = jnp.ones((V_ROWS - HEAD_DIM, avt.shape[1]), BF16)
    for h in range(ATT_HEADS):
        avtb_ref[h * V_ROWS:h * V_ROWS + HEAD_DIM, :] = avt[h * HEAD_DIM:(h + 1) * HEAD_DIM].astype(BF16)
        avtb_ref[h * V_ROWS + HEAD_DIM:(h + 1) * V_ROWS, :] = ones
    ikt_ref[...] = zt[_T_IK:_T_IW]
    iwt_ref[...] = zt[_T_IW:_T_END] * (IDX_HEADS ** -0.5 * IDX_DIM ** -0.5)


def _inproj(x, bv, tv, g, wm, wt, tm):
    n = x.shape[0]
    nt = tv // tm
    row = lambda w: pl.BlockSpec((tm, w), lambda i: (i, 0))
    col = lambda h: pl.BlockSpec((None, h, tm), lambda i: (i // nt, 0, i % nt))
    full = lambda a: pl.BlockSpec(a.shape, lambda i: (0,) * a.ndim)
    out_shape = (
        jax.ShapeDtypeStruct((n, C_RET - C_U), F32),
        jax.ShapeDtypeStruct((n, C_AQ - C_RET), F32),
        jax.ShapeDtypeStruct((n, D_ATT), F32),
        jax.ShapeDtypeStruct((bv, D_ATT, tv), F32),
        jax.ShapeDtypeStruct((bv, D_ATT, tv), F32),
        jax.ShapeDtypeStruct((bv, IDX_DIM, tv), F32),
        jax.ShapeDtypeStruct((n, D_ATT), BF16),
        jax.ShapeDtypeStruct((n, D_ATT), BF16),
        jax.ShapeDtypeStruct((bv, ATT_HEADS * V_ROWS, tv), BF16),
        jax.ShapeDtypeStruct((n, IDX_HEADS * IDX_DIM), BF16),
        jax.ShapeDtypeStruct((n, IDX_DIM), BF16),
        jax.ShapeDtypeStruct((bv, SUBLANES, tv), F32),
    )
    out_specs = (
        row(C_RET - C_U), row(C_AQ - C_RET), row(D_ATT), col(D_ATT), col(D_ATT), col(IDX_DIM),
        row(D_ATT), row(D_ATT), col(ATT_HEADS * V_ROWS), row(IDX_HEADS * IDX_DIM),
        row(IDX_DIM), col(SUBLANES))
    return pl.pallas_call(
        _inproj_kernel,
        grid=(n // tm,),
        in_specs=[row(D_MODEL), full(g), full(wm), full(wt)],
        out_specs=out_specs,
        out_shape=out_shape,
        compiler_params=_cparams(1),
        name="inproj",
    )(x, g, wm, wt)


def _outproj_kernel(x_ref, ys_ref, yr_ref, ya_ref, wo_ref, gf_ref, o_ref, *, final):
    y = x_ref[...]
    y = y + jnp.dot(ys_ref[...], wo_ref[0:D_SSM, :], preferred_element_type=F32)
    y = y + jnp.dot(yr_ref[...], wo_ref[D_SSM:D_SSM + D_RET, :], preferred_element_type=F32)
    y = y + jnp.dot(ya_ref[...], wo_ref[D_SSM + D_RET:, :], preferred_element_type=F32)
    if final:
        ms = jnp.mean(y * y, axis=-1, keepdims=True)
        y = y * lax.rsqrt(ms + NORM_EPS) * gf_ref[...]
    o_ref[...] = y


def _outproj(x, ys, yr, ya, wo, gf, tm, final):
    n = x.shape[0]
    row = lambda w: pl.BlockSpec((tm, w), lambda i: (i, 0))
    full = lambda a: pl.BlockSpec(a.shape, lambda i: (0,) * a.ndim)
    return pl.pallas_call(
        functools.partial(_outproj_kernel, final=final),
        grid=(n // tm,),
        in_specs=[row(D_MODEL), row(D_SSM), row(D_RET), row(D_ATT), full(wo), full(gf)],
        out_specs=row(D_MODEL),
        out_shape=jax.ShapeDtypeStruct((n, D_MODEL), F32),
        compiler_params=_cparams(1),
        name="outproj",
    )(x, ys, yr, ya, wo, gf)


def _cmul(ar, ai, br, bi):
    return ar * br - ai * bi, ar * bi + ai * br


def _ssm_kernel(u2_ref, h0r_ref, h0i_ref, lr_ref, li_ref, ls_ref, btr_ref, bti_ref,
                ctr_ref, cti_ref, d_ref, wg_ref, bg_ref,
                y_ref, hr_out_ref, hi_out_ref,
                bbr_ref, bbi_ref, apr_ref, api_ref, cr_ref, ci_ref, hr_ref, hi_ref, *, tl):
    b = pl.program_id(0)
    c = pl.program_id(1)
    nc = pl.num_programs(1)

    @pl.when((b == 0) & (c == 0))
    def _prep():
        lr, li = lr_ref[...], li_ref[...]
        dt = jnp.exp(ls_ref[...])
        mag = jnp.exp(lr * dt)
        abr, abi = mag * jnp.cos(li * dt), mag * jnp.sin(li * dt)
        den = lr * lr + li * li
        nr, ni = abr - 1.0, abi
        fr = (nr * lr + ni * li) / den
        fi = (ni * lr - nr * li) / den
        bbr_ref[...] = (fr * btr_ref[...] - fi * bti_ref[...]).astype(BF16)
        bbi_ref[...] = (fr * bti_ref[...] + fi * btr_ref[...]).astype(BF16)
        pr, pi = abr, abi
        rows_r, rows_i = [pr], [pi]
        for _ in range(SUBLANES - 1):
            pr, pi = _cmul(pr, pi, abr, abi)
            rows_r.append(pr)
            rows_i.append(pi)
        apr_ref[...] = jnp.concatenate(rows_r, axis=0)
        api_ref[...] = jnp.concatenate(rows_i, axis=0)

    @pl.when(c == 0)
    def _init():
        cr_ref[...] = h0r_ref[...]
        ci_ref[...] = h0i_ref[...]

    u = u2_ref[:, 0:D_SSM]
    gate = u2_ref[:, D_SSM:2 * D_SSM]
    ub = u.astype(BF16)
    hr_ref[...] = jnp.dot(ub, bbr_ref[...], preferred_element_type=F32)
    hi_ref[...] = jnp.dot(ub, bbi_ref[...], preferred_element_type=F32)

    apr, api = apr_ref[...], api_ref[...]
    rowid = lax.broadcasted_iota(I32, (SUBLANES, N_STATE), 0)

    def blk(j, carry):
        cr, ci = carry
        r0 = pl.multiple_of(j * SUBLANES, SUBLANES)
        br = hr_ref[pl.ds(r0, SUBLANES), :]
        bi = hi_ref[pl.ds(r0, SUBLANES), :]
        for k in (1, 2, 4):
            keep = rowid >= k
            sr = jnp.where(keep, pltpu.roll(br, k, 0), 0.0)
            si = jnp.where(keep, pltpu.roll(bi, k, 0), 0.0)
            akr, aki = apr[k - 1:k, :], api[k - 1:k, :]
            br, bi = br + (akr * sr - aki * si), bi + (akr * si + aki * sr)
        hr = br + (apr * cr - api * ci)
        hi = bi + (apr * ci + api * cr)
        hr_ref[pl.ds(r0, SUBLANES), :] = hr
        hi_ref[pl.ds(r0, SUBLANES), :] = hi
        return hr[SUBLANES - 1:SUBLANES, :], hi[SUBLANES - 1:SUBLANES, :]

    cr, ci = lax.fori_loop(0, tl // SUBLANES, blk, (cr_ref[...], ci_ref[...]))
    cr_ref[...] = cr
    ci_ref[...] = ci

    @pl.when(c == nc - 1)
    def _fin():
        hr_out_ref[...] = cr
        hi_out_ref[...] = ci

    y = (jnp.dot(hr_ref[...].astype(BF16), ctr_ref[...], preferred_element_type=F32)
         - jnp.dot(hi_ref[...].astype(BF16), cti_ref[...], preferred_element_type=F32)
         + d_ref[...] * u)
    g = jax.nn.gelu(y)
    y = g * jax.nn.sigmoid(jnp.dot(g.astype(BF16), wg_ref[...], preferred_element_type=F32) + bg_ref[...])
    y_ref[...] = (y * (gate * jax.nn.sigmoid(gate))).astype(BF16)


def _ssm(u2, h0r, h0i, prm, tl):
    bv, tv, _ = u2.shape
    full = lambda a: pl.BlockSpec(a.shape, lambda b, c: (0,) * a.ndim)
    st = pl.BlockSpec((None, 1, N_STATE), lambda b, c: (b, 0, 0))
    return pl.pallas_call(
        functools.partial(_ssm_kernel, tl=tl),
        grid=(bv, tv // tl),
        in_specs=[pl.BlockSpec((None, tl, 2 * D_SSM), lambda b, c: (b, c, 0)), st, st]
        + [full(a) for a in prm],
        out_specs=(pl.BlockSpec((None, tl, D_SSM), lambda b, c: (b, c, 0)), st, st),
        out_shape=(jax.ShapeDtypeStruct((bv, tv, D_SSM), BF16),
                   jax.ShapeDtypeStruct((bv, 1, N_STATE), F32),
                   jax.ShapeDtypeStruct((bv, 1, N_STATE), F32)),
        scratch_shapes=[pltpu.VMEM((D_SSM, N_STATE), BF16), pltpu.VMEM((D_SSM, N_STATE), BF16),
                        pltpu.VMEM((SUBLANES, N_STATE), F32), pltpu.VMEM((SUBLANES, N_STATE), F32),
                        pltpu.VMEM((1, N_STATE), F32), pltpu.VMEM((1, N_STATE), F32),
                        pltpu.VMEM((tl, N_STATE), F32), pltpu.VMEM((tl, N_STATE), F32)],
        compiler_params=_cparams(2),
        name="ssm",
    )(u2, h0r, h0i, *prm)


def _ssm_params(lam_re, lam_im, b_re, b_im, c_re, c_im, d_skip, log_step, w_glu, b_glu):
    eye = jnp.eye(SSM_GROUPS, dtype=F32)

    def bdiag(m):
        g, r, c = m.shape
        return (eye[:, None, :, None] * m[:, :, None, :]).reshape(g * r, g * c)

    flat = lambda a: a.astype(F32).reshape(1, N_STATE)
    ls = jnp.broadcast_to(log_step.astype(F32)[:, None], (SSM_GROUPS, SSM_STATE))
    btr = bdiag(jnp.swapaxes(b_re.astype(F32), 1, 2))
    bti = bdiag(jnp.swapaxes(b_im.astype(F32), 1, 2))
    ctr = bdiag(jnp.swapaxes(c_re.astype(F32), 1, 2)).astype(BF16)
    cti = bdiag(jnp.swapaxes(c_im.astype(F32), 1, 2)).astype(BF16)
    return (flat(lam_re), flat(lam_im), flat(ls), btr, bti, ctr, cti,
            d_skip.astype(F32).reshape(1, D_SSM), w_glu.astype(BF16),
            b_glu.astype(F32).reshape(1, D_SSM))


def _swap_halves(x):
    lane = lax.broadcasted_iota(I32, (x.shape[0], LANES), 1)
    first = (lane % HEAD_DIM) < (HEAD_DIM // 2)
    tiles = []
    for t in range(x.shape[1] // LANES):
        xt = x[:, t * LANES:(t + 1) * LANES]
        up = pltpu.roll(xt, LANES - HEAD_DIM // 2, 1)
        dn = pltpu.roll(xt, HEAD_DIM // 2, 1)
        tiles.append(jnp.where(first, up, dn))
    return jnp.concatenate(tiles, axis=1)


def _ret_kernel(r4_ref, s0_ref, cos_ref, sin_ref, din_ref, dq_ref, dk_ref, ds_ref, g_ref,
                y_ref, s_out_ref, s_ref):
    c = pl.program_id(1)
    nc = pl.num_programs(1)

    @pl.when(c == 0)
    def _init():
        s_ref[...] = s0_ref[...]

    q = r4_ref[:, 0:D_RET]
    k = r4_ref[:, D_RET:2 * D_RET]
    v = r4_ref[:, 2 * D_RET:3 * D_RET]
    gate = r4_ref[:, 3 * D_RET:4 * D_RET]
    cos = jnp.concatenate([cos_ref[...]] * (D_RET // LANES), axis=1)
    sin = jnp.concatenate([sin_ref[...]] * (D_RET // LANES), axis=1)
    qr = q * cos + _swap_halves(q) * sin
    kr = (k * cos + _swap_halves(k) * sin) * HEAD_DIM ** -0.5
    qb = qr.astype(BF16)
    kb = kr.astype(BF16)
    vb = v.astype(BF16)
    kdt = (kr * dk_ref[...]).T.astype(BF16)
    dq = dq_ref[...]
    gsil = gate * jax.nn.sigmoid(gate) * g_ref[...]
    sls = [slice(h * HEAD_DIM, (h + 1) * HEAD_DIM) for h in range(RET_HEADS)]
    atts, crosses = [], []
    for h, sl in enumerate(sls):
        qh, kh, vh = qb[:, sl], kb[:, sl], vb[:, sl]
        s = s_ref[h]
        atts.append((lax.dot_general(qh, kh, _NT, preferred_element_type=F32) * din_ref[h]).astype(BF16))
        crosses.append(jnp.dot(qh, s.astype(BF16), preferred_element_type=F32) * dq[:, sl])
        s_ref[h] = s * ds_ref[h] + jnp.dot(kdt[sl, :], vh, preferred_element_type=F32)
    for h, sl in enumerate(sls):
        o = jnp.dot(atts[h], vb[:, sl], preferred_element_type=F32) + crosses[h]
        mu = jnp.mean(o, axis=-1, keepdims=True)
        var = jnp.mean(jnp.square(o - mu), axis=-1, keepdims=True)
        y_ref[:, sl] = ((o - mu) * lax.rsqrt(var + NORM_EPS) * gsil[:, sl]).astype(BF16)

    @pl.when(c == nc - 1)
    def _fin():
        s_out_ref[...] = s_ref[...]


def _ret_tables(pos, chunk):
    half = HEAD_DIM // 2
    inv = ROPE_BASE ** (-jnp.arange(half, dtype=F32) / half)
    ang = pos.astype(F32)[:, None] * inv[None, :]
    cos, sin = jnp.cos(ang), jnp.sin(ang)
    cos_t = jnp.concatenate([cos, cos] * (LANES // HEAD_DIM), axis=1)
    sin_t = jnp.concatenate([-sin, sin] * (LANES // HEAD_DIM), axis=1)
    log_g = jnp.log1p(-jnp.exp2(-5.0 - jnp.arange(RET_HEADS, dtype=F32)))
    j = jnp.arange(chunk, dtype=F32)
    rel = j[:, None] - j[None, :]
    din = jnp.where(rel[None] >= 0, jnp.exp(log_g[:, None, None] * jnp.maximum(rel, 0.0)[None]), 0.0)
    dq = jnp.repeat(jnp.exp(log_g[None, :] * (j[:, None] + 1.0)), HEAD_DIM, axis=1)
    dk = jnp.repeat(jnp.exp(log_g[None, :] * (chunk - 1.0 - j[:, None])), HEAD_DIM, axis=1)
    ds = jnp.broadcast_to(jnp.exp(log_g * chunk)[:, None, None], (RET_HEADS, HEAD_DIM, HEAD_DIM))
    return cos_t, sin_t, din, dq, dk, ds


def _ret(r4, s0, pos, norm_g):
    bv, tv, _ = r4.shape
    chunk = RET_CHUNK if tv % RET_CHUNK == 0 else tv
    cos_t, sin_t, din, dq, dk, ds = _ret_tables(pos, chunk)
    g = norm_g.astype(F32).reshape(1, D_RET)
    full = lambda a: pl.BlockSpec(a.shape, lambda b, c: (0,) * a.ndim)
    st = pl.BlockSpec((None, RET_HEADS, HEAD_DIM, HEAD_DIM), lambda b, c: (b, 0, 0, 0))
    tab = pl.BlockSpec((chunk, LANES), lambda b, c: (c, 0))
    return pl.pallas_call(
        _ret_kernel,
        grid=(bv, tv // chunk),
        in_specs=[pl.BlockSpec((None, chunk, 4 * D_RET), lambda b, c: (b, c, 0)), st, tab, tab,
                  full(din), full(dq), full(dk), full(ds), full(g)],
        out_specs=(pl.BlockSpec((None, chunk, D_RET), lambda b, c: (b, c, 0)), st),
        out_shape=(jax.ShapeDtypeStruct((bv, tv, D_RET), BF16),
                   jax.ShapeDtypeStruct((bv, RET_HEADS, HEAD_DIM, HEAD_DIM), F32)),
        scratch_shapes=[pltpu.VMEM((RET_HEADS, HEAD_DIM, HEAD_DIM), F32)],
        compiler_params=_cparams(2),
        name="retention",
    )(r4, s0, cos_t, sin_t, din, dq, dk, ds, g)


def _count(sc_ref, nk, kb, pred):
    n_acc = 4

    def body(c, cnts):
        r0 = pl.multiple_of(c * kb, kb)
        cnts = list(cnts)
        for j in range(kb // SUBLANES):
            blk = sc_ref[pl.ds(r0 + j * SUBLANES, SUBLANES), :]
            cnts[j % n_acc] = cnts[j % n_acc] + jnp.where(pred(blk), 1, 0)
        return tuple(cnts)
    zero = jnp.zeros((SUBLANES, LANES), I32)
    cnts = lax.fori_loop(0, nk, body, (zero,) * n_acc)
    return jnp.sum((cnts[0] + cnts[1]) + (cnts[2] + cnts[3]), axis=0, keepdims=True)


_ORD_FLIP = 0x7FFFFFFF
_ORD_POS_INF = F32_INF_BITS
_ORD_NEG_INF = -F32_INF_BITS - 1


def _unord(o):
    return lax.bitcast_convert_type(jnp.where(o >= 0, o, o ^ _ORD_FLIP), F32)


def _select_bias(sc_ref, bias_ref, tri_ref, nk, kb, n_beyond, qpos, topk):
    def count_ge(t):
        return _count(sc_ref, nk, kb, lambda blk: blk >= t) + jnp.where(t <= NEG_INF, n_beyond, 0)

    c_zero = count_ge(jnp.zeros((1, LANES), F32))
    pos = jnp.where(c_zero >= topk, 1, 0)
    lo = jnp.where(pos == 1, 0, _ORD_NEG_INF)
    hi = jnp.where(pos == 1, _ORD_POS_INF + 1, 0)
    c_lo = jnp.where(pos == 1, c_zero, nk * kb + n_beyond)
    n_steps = 31

    def step(_, carry):
        lo, hi, c_lo = carry
        mid = lo + lax.shift_right_arithmetic(hi - lo, 1)
        c = count_ge(_unord(mid))
        ok = c >= topk
        return jnp.where(ok, mid, lo), jnp.where(ok, hi, mid), jnp.where(ok, c, c_lo)

    lo, _, c_lo = lax.fori_loop(0, n_steps, step, (lo, hi, c_lo))
    tau = _unord(lo)
    surplus = c_lo - topk
    surplus_ties = jnp.max(surplus) > 0

    @pl.when(jnp.logical_not(surplus_ties))
    def _no_surplus():
        def body(c, _):
            r0 = pl.multiple_of(c * kb, kb)
            blk = sc_ref[pl.ds(r0, kb), :]
            kidx = r0 + lax.broadcasted_iota(I32, (kb, LANES), 0)
            bias_ref[pl.ds(r0, kb), :] = jnp.where((blk >= tau) & (kidx <= qpos), 0.0, NEG_INF)
            return 0
        lax.fori_loop(0, nk, body, 0)

    @pl.when(surplus_ties)
    def _ties_by_index():
        surplus_f = surplus.astype(F32)
        above = jnp.where(tau == NEG_INF, n_beyond, 0).astype(F32)

        def body(c, above):
            r0 = pl.multiple_of((nk - 1 - c) * kb, kb)
            for b in reversed(range(kb // TIE_BLOCK)):
                rb = r0 + b * TIE_BLOCK
                blk = sc_ref[pl.ds(rb, TIE_BLOCK), :]
                eq = blk == tau
                eq01 = jnp.where(eq, 1.0, 0.0)
                at_or_above = jnp.dot(tri_ref[...], eq01.astype(BF16), preferred_element_type=F32) + above
                kidx = rb + lax.broadcasted_iota(I32, (TIE_BLOCK, LANES), 0)
                sel = ((blk > tau) | (eq & (at_or_above > surplus_f))) & (kidx <= qpos)
                bias_ref[pl.ds(rb, TIE_BLOCK), :] = jnp.where(sel, 0.0, NEG_INF)
                above = above + jnp.sum(eq01, axis=0, keepdims=True)
            return above

        lax.fori_loop(0, nk, body, above)


TIE_BLOCK = 128


def _tri():
    r = np.arange(TIE_BLOCK)
    return jnp.asarray((r[None, :] >= r[:, None]).astype(np.float32), dtype=BF16)


def _dsa_prompt_kernel(ikb_ref, akb_ref, avtb_ref, iqb_ref, iwt_ref, aqb_ref, ag_ref, tri_ref,
                       y_ref, sc_ref, bias_ref, lga_ref, lg_ref, *o_refs, kb, t_total, topk):
    i = pl.program_id(1)
    q0 = i * Q_BLOCK
    nk = (q0 + Q_BLOCK + kb - 1) // kb
    qpos = q0 + lax.broadcasted_iota(I32, (1, LANES), 1)
    iq = iqb_ref[...]
    iw = iwt_ref[...]
    qcat = jnp.concatenate([iq[:, h * IDX_DIM:(h + 1) * IDX_DIM] for h in range(IDX_HEADS)], axis=0)

    def score_body(c, _):
        r0 = pl.multiple_of(c * kb, kb)
        keys = ikb_ref[pl.ds(r0, kb), :]
        s = lax.dot_general(keys, qcat, _NT, preferred_element_type=F32)
        parts = [jnp.maximum(s[:, h * LANES:(h + 1) * LANES], 0.0) * iw[h:h + 1, :]
                 for h in range(IDX_HEADS)]
        acc = (parts[0] + parts[1]) + (parts[2] + parts[3])
        kidx = r0 + lax.broadcasted_iota(I32, (kb, LANES), 0)
        sc_ref[pl.ds(r0, kb), :] = jnp.where(kidx <= qpos, acc, NEG_INF)
        return 0

    lax.fori_loop(0, nk, score_body, 0)
    _select_bias(sc_ref, bias_ref, tri_ref, nk, kb, t_total - nk * kb, qpos, topk)

    aq = aqb_ref[...]
    lane = lax.broadcasted_iota(I32, (Q_BLOCK, LANES), 1)
    qpairs = []
    for j in range(ATT_HEADS // 2):
        qp = aq[:, j * LANES:(j + 1) * LANES]
        zero = jnp.zeros_like(qp)
        qpairs.append(jnp.concatenate([jnp.where(lane < HEAD_DIM, qp, zero),
                                       jnp.where(lane >= HEAD_DIM, qp, zero)], axis=0))
    for o_ref in o_refs:
        o_ref[...] = jnp.zeros(o_ref.shape, F32)

    half = kb // 2

    def qk(r0, lg):
        bias = bias_ref[pl.ds(r0, half), :]
        mx = []
        for j in range(ATT_HEADS // 2):
            kpair = akb_ref[pl.ds(r0, half), j * LANES:(j + 1) * LANES]
            both = lax.dot_general(kpair, qpairs[j], _NT, preferred_element_type=F32)
            for i in range(2):
                logit = both[:, i * LANES:(i + 1) * LANES] + bias
                lg[2 * j + i] = logit
                mx.append(jnp.max(logit, axis=0, keepdims=True))
        return tuple(mx)

    def pv(r0, lg, ms, mx):
        ms_new = []
        for h in range(ATT_HEADS):
            m_new = jnp.maximum(ms[h], mx[h])
            alpha = jnp.exp2(ms[h] - m_new)
            p = jnp.exp2(lg[h] - m_new)
            ms_new.append(m_new)
            vt = avtb_ref[h * V_ROWS:(h + 1) * V_ROWS, pl.ds(r0, half)]
            o_refs[h][...] = alpha * o_refs[h][...] + jnp.dot(vt, p.astype(BF16), preferred_element_type=F32)
        return tuple(ms_new)

    def att_body(c, carry):
        ms, mx_a = carry
        r0 = pl.multiple_of(c * kb, kb)
        mx_b = qk(r0 + half, lg_ref)
        ms = pv(r0, lga_ref, ms, mx_a)
        r_next = pl.multiple_of(jnp.minimum(c + 1, nk - 1) * kb, kb)
        mx_a = qk(r_next, lga_ref)
        ms = pv(r0 + half, lg_ref, ms, mx_b)
        return ms, mx_a

    m0 = (jnp.full((1, LANES), NEG_INF, F32),) * ATT_HEADS
    lax.fori_loop(0, nk, att_body, (m0, qk(0, lga_ref)))
    ot = jnp.concatenate([o_refs[h][0:HEAD_DIM] / o_refs[h][HEAD_DIM:HEAD_DIM + 1]
                          for h in range(ATT_HEADS)], axis=0)
    gate = ag_ref[...]
    y_ref[...] = (ot.T * (gate * jax.nn.sigmoid(gate))).astype(BF16)


def _dsa_prompt(bv, tv, ikb, akb, avtb, iqb, iwt, aqb, ag, kb=512):
    assert ATT_HEADS % 2 == 0 and 2 * HEAD_DIM == LANES
    topk = min(TOPK_MAX, tv // 4)
    kb = min(kb, tv)
    nq = tv // Q_BLOCK
    tri = _tri()
    per_b = lambda w: pl.BlockSpec((tv, w), lambda b, i: (b, 0))
    qrow = lambda w: pl.BlockSpec((Q_BLOCK, w), lambda b, i: (b * nq + i, 0))
    return pl.pallas_call(
        functools.partial(_dsa_prompt_kernel, kb=kb, t_total=tv, topk=topk),
        grid=(bv, nq),
        in_specs=[per_b(IDX_DIM), per_b(D_ATT),
                  pl.BlockSpec((None, ATT_HEADS * V_ROWS, tv), lambda b, i: (b, 0, 0)),
                  qrow(IDX_HEADS * IDX_DIM),
                  pl.BlockSpec((None, SUBLANES, Q_BLOCK), lambda b, i: (b, 0, i)),
                  qrow(D_ATT), qrow(D_ATT),
                  pl.BlockSpec(tri.shape, lambda b, i: (0, 0))],
        out_specs=qrow(D_ATT),
        out_shape=jax.ShapeDtypeStruct((bv * tv, D_ATT), BF16),
        scratch_shapes=[pltpu.VMEM((tv, LANES), F32), pltpu.VMEM((tv, LANES), F32)]
        + [pltpu.VMEM((ATT_HEADS, kb // 2, LANES), F32)] * 2
        + [pltpu.VMEM((V_ROWS, LANES), F32)] * ATT_HEADS,
        compiler_params=_cparams(2),
        name="dsa_prompt",
    )(ikb, akb, avtb, iqb, iwt, aqb, ag, tri)


def _samp_score_kernel(pt_ref, *refs, npg, ts):
    del pt_ref
    pages = refs[:npg]
    qi_ref, w_ref, iknew_ref, sc_ref, scn_ref = refs[npg:]
    qi = qi_ref[...]
    w = w_ref[...]

    def score(keys_t):
        s = jnp.dot(qi, keys_t.astype(BF16), preferred_element_type=F32)
        r = jnp.maximum(s, 0.0) * w
        return (r[0:ts] + r[ts:2 * ts]) + (r[2 * ts:3 * ts] + r[3 * ts:4 * ts])

    for i in range(npg):
        sc_ref[:, i * PAGE_SIZE:(i + 1) * PAGE_SIZE] = score(pages[i][...])

    @pl.when(pl.program_id(1) == 0)
    def _new_keys():
        sn = score(iknew_ref[...])
        col = lax.broadcasted_iota(I32, (ts, PAGE_SIZE), 1)
        row = lax.broadcasted_iota(I32, (ts, PAGE_SIZE), 0)
        scn_ref[...] = jnp.where(col <= row, sn, NEG_INF)


def _samp_scores(page_table, kidx_t, layer, qi, w, iknew_t, npg):
    bs, n_pages = page_table.shape
    ts = qi.shape[1] // IDX_HEADS
    page_spec = lambda i: pl.BlockSpec(
        (None, None, IDX_DIM, PAGE_SIZE), lambda b, j, pt, i=i: (layer, pt[b, j * npg + i], 0, 0))
    per_b = lambda a: pl.BlockSpec((None,) + a.shape[1:], lambda b, j, pt: (b, 0, 0))
    gs = pltpu.PrefetchScalarGridSpec(
        num_scalar_prefetch=1,
        grid=(bs, n_pages // npg),
        in_specs=[page_spec(i) for i in range(npg)] + [per_b(qi), per_b(w), per_b(iknew_t)],
        out_specs=(pl.BlockSpec((ts, npg * PAGE_SIZE), lambda b, j, pt: (b, j)),
                   pl.BlockSpec((ts, PAGE_SIZE), lambda b, j, pt: (b, 0))),
    )
    return pl.pallas_call(
        functools.partial(_samp_score_kernel, npg=npg, ts=ts),
        grid_spec=gs,
        out_shape=(jax.ShapeDtypeStruct((bs * ts, n_pages * PAGE_SIZE), F32),
                   jax.ShapeDtypeStruct((bs * ts, PAGE_SIZE), F32)),
        compiler_params=_cparams(2),
        name="sample_scores",
    )(page_table, *([kidx_t] * npg), qi, w, iknew_t)


def _select_kernel(sc_ref, qpos_ref, tri_ref, bias_ref, *, kb, nk, topk):
    _select_bias(sc_ref, bias_ref, tri_ref, nk, kb, 0, qpos_ref[...], topk)


def _select(sc_t, qpos, topk, kb):
    nkeys, nq = sc_t.shape
    tri = _tri()
    col = pl.BlockSpec((nkeys, LANES), lambda i: (0, i))
    return pl.pallas_call(
        functools.partial(_select_kernel, kb=kb, nk=nkeys // kb, topk=topk),
        grid=(nq // LANES,),
        in_specs=[col, pl.BlockSpec((1, LANES), lambda i: (0, i)),
                  pl.BlockSpec(tri.shape, lambda i: (0, 0))],
        out_specs=col,
        out_shape=jax.ShapeDtypeStruct((nkeys, nq), F32),
        compiler_params=_cparams(1),
        name="sample_select",
    )(sc_t, qpos, tri)


def _samp_att_kernel(pt_ref, *refs, npg, ts):
    del pt_ref
    kp = refs[:npg]
    vp = refs[npg:2 * npg]
    q_ref, bias_ref, biasn_ref, kn_ref, vn_ref, ag_ref, y_ref, m_ref, l_ref, acc_ref = refs[2 * npg:]
    j = pl.program_id(1)
    nj = pl.num_programs(1)

    @pl.when(j == 0)
    def _init():
        m_ref[...] = jnp.full(m_ref.shape, NEG_INF, F32)
        l_ref[...] = jnp.zeros(l_ref.shape, F32)
        acc_ref[...] = jnp.zeros(acc_ref.shape, F32)

    q = q_ref[...]

    def update(kts, vts, bias):
        logit = jnp.concatenate(
            [jnp.dot(q, kt.astype(BF16), preferred_element_type=F32) for kt in kts], axis=1)
        logit = logit + jnp.concatenate([bias] * ATT_HEADS, axis=0)
        m_old = m_ref[...]
        m_new = jnp.maximum(m_old, jnp.max(logit, axis=1, keepdims=True))
        alpha = jnp.exp2(m_old - m_new)
        p = jnp.exp2(logit - m_new)
        l_ref[...] = alpha * l_ref[...] + jnp.sum(p, axis=1, keepdims=True)
        pb = p.astype(BF16)
        pvs = [lax.dot_general(pb[:, i * PAGE_SIZE:(i + 1) * PAGE_SIZE], vts[i].astype(BF16), _NT,
                               preferred_element_type=F32) for i in range(len(vts))]
        while len(pvs) > 1:
            pvs = [pvs[i] + pvs[i + 1] for i in range(0, len(pvs) - 1, 2)] + pvs[len(pvs) - len(pvs) % 2:]
        acc_ref[...] = alpha * acc_ref[...] + pvs[0]
        m_ref[...] = m_new

    update([r[...] for r in kp], [r[...] for r in vp], bias_ref[...])

    @pl.when(j == nj - 1)
    def _fin():
        update([kn_ref[...]], [vn_ref[...]], biasn_ref[...])
        o = acc_ref[...] / l_ref[...]
        gate = ag_ref[...]
        gsil = gate * jax.nn.sigmoid(gate)
        for h in range(ATT_HEADS):
            sl = slice(h * HEAD_DIM, (h + 1) * HEAD_DIM)
            y_ref[:, sl] = (o[h * ts:(h + 1) * ts, sl] * gsil[:, sl]).astype(BF16)


def _samp_attention(page_table, k_t, v_t, layer, qbd, bias, biasn, knew_t, vnew_t, ag, npg):
    bs, n_pages = page_table.shape
    ts = qbd.shape[1] // ATT_HEADS
    page_spec = lambda i: pl.BlockSpec(
        (None, None, D_ATT, PAGE_SIZE), lambda b, j, pt, i=i: (layer, pt[b, j * npg + i], 0, 0))
    per_b = lambda a: pl.BlockSpec((None,) + a.shape[1:], lambda b, j, pt: (b, 0, 0))
    gs = pltpu.PrefetchScalarGridSpec(
        num_scalar_prefetch=1,
        grid=(bs, n_pages // npg),
        in_specs=[page_spec(i) for i in range(npg)] * 2
        + [per_b(qbd),
           pl.BlockSpec((ts, npg * PAGE_SIZE), lambda b, j, pt: (b, j)),
           pl.BlockSpec((ts, PAGE_SIZE), lambda b, j, pt: (b, 0)),
           per_b(knew_t), per_b(vnew_t),
           pl.BlockSpec((ts, D_ATT), lambda b, j, pt: (b, 0))],
        out_specs=pl.BlockSpec((ts, D_ATT), lambda b, j, pt: (b, 0)),
        scratch_shapes=[pltpu.VMEM((ATT_HEADS * ts, 1), F32), pltpu.VMEM((ATT_HEADS * ts, 1), F32),
                        pltpu.VMEM((ATT_HEADS * ts, D_ATT), F32)],
    )
    return pl.pallas_call(
        functools.partial(_samp_att_kernel, npg=npg, ts=ts),
        grid_spec=gs,
        out_shape=jax.ShapeDtypeStruct((bs * ts, D_ATT), BF16),
        compiler_params=_cparams(2),
        name="sample_attention",
    )(page_table, *([k_t] * npg), *([v_t] * npg), qbd, bias, biasn, knew_t, vnew_t, ag)


def _dsa_sample(page_table, k_t, v_t, kidx_t, layer, ts, akt, avt, ikt, ag, aqb, iqb, iwt):
    bs, n_pages = page_table.shape
    past = n_pages * PAGE_SIZE
    n = bs * ts
    topk = min(TOPK_MAX, (past + ts) // 4)
    npg = math.gcd(n_pages, 16)

    def new_keys_t(a):
        a = a.reshape(a.shape[0], bs, ts).transpose(1, 0, 2)
        return jnp.pad(a, ((0, 0), (0, 0), (0, PAGE_SIZE - ts)))

    qi = iqb.reshape(bs, ts, IDX_HEADS, IDX_DIM).transpose(0, 2, 1, 3).reshape(bs, IDX_HEADS * ts, IDX_DIM)
    w = iwt[:IDX_HEADS].reshape(IDX_HEADS, bs, ts).transpose(1, 0, 2).reshape(bs, IDX_HEADS * ts, 1)
    w = jnp.broadcast_to(w, (bs, IDX_HEADS * ts, LANES))
    sc, scn = _samp_scores(page_table, kidx_t, layer, qi, w, new_keys_t(ikt), npg)
    sc_t = jnp.concatenate([sc, scn], axis=1).T
    qpos = (past + jnp.arange(n, dtype=I32) % ts).reshape(1, n)
    bias_all = _select(sc_t, qpos, topk, PAGE_SIZE).T
    eye = jnp.eye(ATT_HEADS, dtype=BF16)
    q4 = aqb.reshape(bs, ts, ATT_HEADS, HEAD_DIM)
    qbd = (eye[None, :, None, :, None] * q4.transpose(0, 2, 1, 3)[:, :, :, None, :]).reshape(
        bs, ATT_HEADS * ts, D_ATT)
    return _samp_attention(page_table, k_t, v_t, layer, qbd, bias_all[:, :past], bias_all[:, past:],
                           new_keys_t(akt), new_keys_t(avt), ag, npg)


def _mixer_layer(x, pos, h0r, h0i, s0, attend, g, w_in_l, w_out_l, ssm_prm, ret_g, gf, final, tm, tl):
    bv, tv, _ = x.shape
    n = bv * tv
    wm = w_in_l[:, :C_IW].astype(BF16)
    wt = jnp.concatenate([w_in_l[:, C_AK:C_AG].T, w_in_l[:, C_IK:C_END].T,
                          jnp.zeros((SUBLANES - IDX_HEADS, D_MODEL), w_in_l.dtype)], axis=0).astype(BF16)
    xf = x.reshape(n, D_MODEL)
    pb, pt = (bv, tv) if tv % tm == 0 else (1, n)
    tm = min(tm, pt)
    (u2, r4, ag, akt, avt, ikt, aqb, akb, avtb, iqb, ikb, iwt) = _inproj(
        xf, pb, pt, g.astype(F32).reshape(1, D_MODEL), wm, wt, tm)
    y_ssm, hr, hi = _ssm(u2.reshape(bv, tv, -1), h0r, h0i, ssm_prm, tl)
    y_ret, s_new = _ret(r4.reshape(bv, tv, -1), s0, pos, ret_g)
    y_att = attend(ag, akt, avt, ikt, aqb, akb, avtb, iqb, ikb, iwt)
    y = _outproj(xf, y_ssm.reshape(n, D_SSM), y_ret.reshape(n, D_RET), y_att,
                 w_out_l.astype(BF16), gf.astype(F32).reshape(1, D_MODEL), tm, final)
    tok = lambda a: a.reshape(pb, a.shape[1], -1, tv).transpose(0, 2, 3, 1).reshape(bv, tv, a.shape[1])
    return y.reshape(bv, tv, D_MODEL), (tok(akt), tok(avt), tok(ikt), hr, hi, s_new)


def kernel(x_prompt, x_sample, cache_k, cache_v, cache_kidx, state_ssm_re, state_ssm_im, state_ret,
           page_table, norm_g, w_in, w_out, ssm_lambda_re, ssm_lambda_im, ssm_b_re, ssm_b_im,
           ssm_c_re, ssm_c_im, ssm_d, ssm_log_step, ssm_w_glu, ssm_b_glu, ret_norm_g, final_norm_g):
    bp, tp, _ = x_prompt.shape
    bs, ts, _ = x_sample.shape
    depth = w_in.shape[0]
    past = page_table.shape[1] * PAGE_SIZE
    assert ts == SUBLANES and tp % Q_BLOCK == 0
    pos_p = jnp.arange(tp)
    pos_s = past + jnp.arange(ts)
    k_t = jnp.transpose(cache_k, (0, 1, 3, 4, 2)).reshape(cache_k.shape[:2] + (D_ATT, PAGE_SIZE))
    v_t = jnp.transpose(cache_v, (0, 1, 3, 4, 2)).reshape(cache_v.shape[:2] + (D_ATT, PAGE_SIZE))
    kidx_t = jnp.transpose(cache_kidx, (0, 1, 3, 2))
    zero_h = jnp.zeros((bp, 1, N_STATE), F32)
    zero_s = jnp.zeros((bp, RET_HEADS, HEAD_DIM, HEAD_DIM), F32)
    hp, hs = x_prompt, x_sample
    outs_p, outs_s = [], []
    for l in range(depth):
        final = l == depth - 1
        prm = _ssm_params(ssm_lambda_re[l], ssm_lambda_im[l], ssm_b_re[l], ssm_b_im[l], ssm_c_re[l],
                          ssm_c_im[l], ssm_d[l], ssm_log_step[l], ssm_w_glu[l], ssm_b_glu[l])

        def attend_p(ag, akt, avt, ikt, aqb, akb, avtb, iqb, ikb, iwt):
            return _dsa_prompt(bp, tp, ikb, akb, avtb, iqb, iwt, aqb, ag)

        hp, st_p = _mixer_layer(hp, pos_p, zero_h, zero_h, zero_s, attend_p, norm_g[l], w_in[l], w_out[l],
                                prm, ret_norm_g[l], final_norm_g, final, tm=256, tl=min(512, tp))

        def attend_s(ag, akt, avt, ikt, aqb, akb, avtb, iqb, ikb, iwt):
            return _dsa_sample(page_table, k_t, v_t, kidx_t, l, ts, akt[0], avt[0], ikt[0], ag,
                               aqb, iqb, iwt[0])

        hs, st_s = _mixer_layer(hs, pos_s, state_ssm_re[l].reshape(bs, 1, N_STATE),
                                state_ssm_im[l].reshape(bs, 1, N_STATE), state_ret[l], attend_s,
                                norm_g[l], w_in[l], w_out[l], prm, ret_norm_g[l], final_norm_g, final,
                                tm=256, tl=ts)
        outs_p.append(st_p)
        outs_s.append(st_s)

    def pack(outs, b, t):
        k = jnp.stack([o[0] for o in outs]).reshape(depth, b, t, ATT_HEADS, HEAD_DIM)
        v = jnp.stack([o[1] for o in outs]).reshape(depth, b, t, ATT_HEADS, HEAD_DIM)
        ki = jnp.stack([o[2] for o in outs])
        hr = jnp.stack([o[3] for o in outs]).reshape(depth, b, SSM_GROUPS, SSM_STATE)
        hi = jnp.stack([o[4] for o in outs]).reshape(depth, b, SSM_GROUPS, SSM_STATE)
        s = jnp.stack([o[5] for o in outs])
        return k, v, ki, hr, hi, s

    return (hp, hs) + pack(outs_p, bp, tp) + pack(outs_s, bs, ts)
```

```python
import functools
import math

import jax
import jax.numpy as jnp
import numpy as np
from jax import lax
from jax.experimental import pallas as pl
from jax.experimental.pallas import tpu as pltpu

F32 = jnp.float32
BF16 = jnp.bfloat16
I32 = jnp.int32

D_MODEL = 1024
HEAD_DIM = 64
SSM_GROUP = 16
D_SSM = 256
SSM_GROUPS = D_SSM // SSM_GROUP
SSM_STATE = 64
N_STATE = SSM_GROUPS * SSM_STATE
D_RET = 384
RET_HEADS = D_RET // HEAD_DIM
RET_CHUNK = 128
ROPE_BASE = 10000.0
D_ATT = 384
ATT_HEADS = D_ATT // HEAD_DIM
IDX_HEADS = 4
IDX_DIM = 64
TOPK_MAX = 256
Q_BLOCK = 128
PAGE_SIZE = 128
NORM_EPS = 1e-6
NEG_INF = -1e30

_SIZES = (D_SSM, D_SSM, D_RET, D_RET, D_RET, D_RET, D_ATT, D_ATT, D_ATT, D_ATT,
          IDX_HEADS * IDX_DIM, IDX_DIM, IDX_HEADS)
_OFF = np.concatenate([[0], np.cumsum(_SIZES)]).tolist()
C_U, C_RET, C_AQ, C_AK, C_AV, C_AG, C_IQ, C_IK, C_IW, C_END = (
    _OFF[0], _OFF[2], _OFF[6], _OFF[7], _OFF[8], _OFF[9], _OFF[10], _OFF[11], _OFF[12], _OFF[13])

LANES = 128
SUBLANES = 8
F32_INF_BITS = 0x7F800000
LOG2_E = 1.4426950408889634
V_ROWS = HEAD_DIM + 16
VMEM_LIMIT = 56 * 1024 * 1024

_NT = (((1,), (1,)), ((), ()))


def _cparams(n_axes):
    return pltpu.CompilerParams(dimension_semantics=("arbitrary",) * n_axes,
                                vmem_limit_bytes=VMEM_LIMIT)


_T_AK, _T_AV, _T_IK, _T_IW, _T_END = 0, D_ATT, 2 * D_ATT, 2 * D_ATT + IDX_DIM, 2 * D_ATT + IDX_DIM + SUBLANES


def _inproj_kernel(x_ref, g_ref, wm_ref, wt_ref,
                   u2_ref, r4_ref, ag_ref, akt_ref, avt_ref, ikt_ref,
                   aqb_ref, akb_ref, avtb_ref, iqb_ref, ikb_ref, iwt_ref):
    x = x_ref[...]
    ms = jnp.mean(x * x, axis=-1, keepdims=True)
    xn = (x * lax.rsqrt(ms + NORM_EPS) * g_ref[...]).astype(BF16)

    def seg(a, b):
        return jnp.dot(xn, wm_ref[:, a:b], preferred_element_type=F32)

    u2_ref[...] = seg(C_U, C_RET)
    r4_ref[...] = seg(C_RET, C_AQ)
    aqb_ref[...] = (seg(C_AQ, C_AK) * (LOG2_E * HEAD_DIM ** -0.5)).astype(BF16)
    akb_ref[...] = seg(C_AK, C_AV).astype(BF16)
    ag_ref[...] = seg(C_AG, C_IQ)
    iqb_ref[...] = seg(C_IQ, C_IK).astype(BF16)
    ikb_ref[...] = seg(C_IK, C_IW).astype(BF16)
    zt = lax.dot_general(wt_ref[...], xn, _NT, preferred_element_type=F32)
    akt_ref[...] = zt[_T_AK:_T_AV]
    avt = zt[_T_AV:_T_IK]
    avt_ref[...] = avt
    ones = jnp.ones((V_ROWS - HEAD_DIM, avt.shape[1]), BF16)
    for h in range(ATT_HEADS):
        avtb_ref[h * V_ROWS:h * V_ROWS + HEAD_DIM, :] = avt[h * HEAD_DIM:(h + 1) * HEAD_DIM].astype(BF16)
        avtb_ref[h * V_ROWS + HEAD_DIM:(h + 1) * V_ROWS, :] = ones
    ikt_ref[...] = zt[_T_IK:_T_IW]
    iwt_ref[...] = zt[_T_IW:_T_END] * (IDX_HEADS ** -0.5 * IDX_DIM ** -0.5)


def _inproj(x, bv, tv, g, wm, wt, tm):
    n = x.shape[0]
    nt = tv // tm
    row = lambda w: pl.BlockSpec((tm, w), lambda i: (i, 0))
    col = lambda h: pl.BlockSpec((None, h, tm), lambda i: (i // nt, 0, i % nt))
    full = lambda a: pl.BlockSpec(a.shape, lambda i: (0,) * a.ndim)
    out_shape = (
        jax.ShapeDtypeStruct((n, C_RET - C_U), F32),
        jax.ShapeDtypeStruct((n, C_AQ - C_RET), F32),
        jax.ShapeDtypeStruct((n, D_ATT), F32),
        jax.ShapeDtypeStruct((bv, D_ATT, tv), F32),
        jax.ShapeDtypeStruct((bv, D_ATT, tv), F32),
        jax.ShapeDtypeStruct((bv, IDX_DIM, tv), F32),
        jax.ShapeDtypeStruct((n, D_ATT), BF16),
        jax.ShapeDtypeStruct((n, D_ATT), BF16),
        jax.ShapeDtypeStruct((bv, ATT_HEADS * V_ROWS, tv), BF16),
        jax.ShapeDtypeStruct((n, IDX_HEADS * IDX_DIM), BF16),
        jax.ShapeDtypeStruct((n, IDX_DIM), BF16),
        jax.ShapeDtypeStruct((bv, SUBLANES, tv), F32),
    )
    out_specs = (
        row(C_RET - C_U), row(C_AQ - C_RET), row(D_ATT), col(D_ATT), col(D_ATT), col(IDX_DIM),
        row(D_ATT), row(D_ATT), col(ATT_HEADS * V_ROWS), row(IDX_HEADS * IDX_DIM),
        row(IDX_DIM), col(SUBLANES))
    return pl.pallas_call(
        _inproj_kernel,
        grid=(n // tm,),
        in_specs=[row(D_MODEL), full(g), full(wm), full(wt)],
        out_specs=out_specs,
        out_shape=out_shape,
        compiler_params=_cparams(1),
        name="inproj",
    )(x, g, wm, wt)


def _outproj_kernel(x_ref, ys_ref, yr_ref, ya_ref, wo_ref, gf_ref, o_ref, *, final):
    y = x_ref[...]
    y = y + jnp.dot(ys_ref[...], wo_ref[0:D_SSM, :], preferred_element_type=F32)
    y = y + jnp.dot(yr_ref[...], wo_ref[D_SSM:D_SSM + D_RET, :], preferred_element_type=F32)
    y = y + jnp.dot(ya_ref[...], wo_ref[D_SSM + D_RET:, :], preferred_element_type=F32)
    if final:
        ms = jnp.mean(y * y, axis=-1, keepdims=True)
        y = y * lax.rsqrt(ms + NORM_EPS) * gf_ref[...]
    o_ref[...] = y


def _outproj(x, ys, yr, ya, wo, gf, tm, final):
    n = x.shape[0]
    row = lambda w: pl.BlockSpec((tm, w), lambda i: (i, 0))
    full = lambda a: pl.BlockSpec(a.shape, lambda i: (0,) * a.ndim)
    return pl.pallas_call(
        functools.partial(_outproj_kernel, final=final),
        grid=(n // tm,),
        in_specs=[row(D_MODEL), row(D_SSM), row(D_RET), row(D_ATT), full(wo), full(gf)],
        out_specs=row(D_MODEL),
        out_shape=jax.ShapeDtypeStruct((n, D_MODEL), F32),
        compiler_params=_cparams(1),
        name="outproj",
    )(x, ys, yr, ya, wo, gf)


def _cmul(ar, ai, br, bi):
    return ar * br - ai * bi, ar * bi + ai * br


def _ssm_kernel(u2_ref, h0r_ref, h0i_ref, lr_ref, li_ref, ls_ref, btr_ref, bti_ref,
                ctr_ref, cti_ref, d_ref, wg_ref, bg_ref,
                y_ref, hr_out_ref, hi_out_ref,
                bbr_ref, bbi_ref, apr_ref, api_ref, cr_ref, ci_ref, hr_ref, hi_ref, *, tl):
    b = pl.program_id(0)
    c = pl.program_id(1)
    nc = pl.num_programs(1)

    @pl.when((b == 0) & (c == 0))
    def _prep():
        lr, li = lr_ref[...], li_ref[...]
        dt = jnp.exp(ls_ref[...])
        mag = jnp.exp(lr * dt)
        abr, abi = mag * jnp.cos(li * dt), mag * jnp.sin(li * dt)
        den = lr * lr + li * li
        nr, ni = abr - 1.0, abi
        fr = (nr * lr + ni * li) / den
        fi = (ni * lr - nr * li) / den
        bbr_ref[...] = (fr * btr_ref[...] - fi * bti_ref[...]).astype(BF16)
        bbi_ref[...] = (fr * bti_ref[...] + fi * btr_ref[...]).astype(BF16)
        pr, pi = abr, abi
        rows_r, rows_i = [pr], [pi]
        for _ in range(SUBLANES - 1):
            pr, pi = _cmul(pr, pi, abr, abi)
            rows_r.append(pr)
            rows_i.append(pi)
        apr_ref[...] = jnp.concatenate(rows_r, axis=0)
        api_ref[...] = jnp.concatenate(rows_i, axis=0)

    @pl.when(c == 0)
    def _init():
        cr_ref[...] = h0r_ref[...]
        ci_ref[...] = h0i_ref[...]

    u = u2_ref[:, 0:D_SSM]
    gate = u2_ref[:, D_SSM:2 * D_SSM]
    ub = u.astype(BF16)
    hr_ref[...] = jnp.dot(ub, bbr_ref[...], preferred_element_type=F32)
    hi_ref[...] = jnp.dot(ub, bbi_ref[...], preferred_element_type=F32)

    apr, api = apr_ref[...], api_ref[...]
    rowid = lax.broadcasted_iota(I32, (SUBLANES, N_STATE), 0)

    def blk(j, carry):
        cr, ci = carry
        r0 = pl.multiple_of(j * SUBLANES, SUBLANES)
        br = hr_ref[pl.ds(r0, SUBLANES), :]
        bi = hi_ref[pl.ds(r0, SUBLANES), :]
        for k in (1, 2, 4):
            keep = rowid >= k
            sr = jnp.where(keep, pltpu.roll(br, k, 0), 0.0)
            si = jnp.where(keep, pltpu.roll(bi, k, 0), 0.0)
            akr, aki = apr[k - 1:k, :], api[k - 1:k, :]
            br, bi = br + (akr * sr - aki * si), bi + (akr * si + aki * sr)
        hr = br + (apr * cr - api * ci)
        hi = bi + (apr * ci + api * cr)
        hr_ref[pl.ds(r0, SUBLANES), :] = hr
        hi_ref[pl.ds(r0, SUBLANES), :] = hi
        return hr[SUBLANES - 1:SUBLANES, :], hi[SUBLANES - 1:SUBLANES, :]

    cr, ci = lax.fori_loop(0, tl // SUBLANES, blk, (cr_ref[...], ci_ref[...]))
    cr_ref[...] = cr
    ci_ref[...] = ci

    @pl.when(c == nc - 1)
    def _fin():
        hr_out_ref[...] = cr
        hi_out_ref[...] = ci

    y = (jnp.dot(hr_ref[...].astype(BF16), ctr_ref[...], preferred_element_type=F32)
         - jnp.dot(hi_ref[...].astype(BF16), cti_ref[...], preferred_element_type=F32)
         + d_ref[...] * u)
    g = jax.nn.gelu(y)
    y = g * jax.nn.sigmoid(jnp.dot(g.astype(BF16), wg_ref[...], preferred_element_type=F32) + bg_ref[...])
    y_ref[...] = (y * (gate * jax.nn.sigmoid(gate))).astype(BF16)


def _ssm(u2, h0r, h0i, prm, tl):
    bv, tv, _ = u2.shape
    full = lambda a: pl.BlockSpec(a.shape, lambda b, c: (0,) * a.ndim)
    st = pl.BlockSpec((None, 1, N_STATE), lambda b, c: (b, 0, 0))
    return pl.pallas_call(
        functools.partial(_ssm_kernel, tl=tl),
        grid=(bv, tv // tl),
        in_specs=[pl.BlockSpec((None, tl, 2 * D_SSM), lambda b, c: (b, c, 0)), st, st]
        + [full(a) for a in prm],
        out_specs=(pl.BlockSpec((None, tl, D_SSM), lambda b, c: (b, c, 0)), st, st),
        out_shape=(jax.ShapeDtypeStruct((bv, tv, D_SSM), BF16),
                   jax.ShapeDtypeStruct((bv, 1, N_STATE), F32),
                   jax.ShapeDtypeStruct((bv, 1, N_STATE), F32)),
        scratch_shapes=[pltpu.VMEM((D_SSM, N_STATE), BF16), pltpu.VMEM((D_SSM, N_STATE), BF16),
                        pltpu.VMEM((SUBLANES, N_STATE), F32), pltpu.VMEM((SUBLANES, N_STATE), F32),
                        pltpu.VMEM((1, N_STATE), F32), pltpu.VMEM((1, N_STATE), F32),
                        pltpu.VMEM((tl, N_STATE), F32), pltpu.VMEM((tl, N_STATE), F32)],
        compiler_params=_cparams(2),
        name="ssm",
    )(u2, h0r, h0i, *prm)


def _ssm_params(lam_re, lam_im, b_re, b_im, c_re, c_im, d_skip, log_step, w_glu, b_glu):
    eye = jnp.eye(SSM_GROUPS, dtype=F32)

    def bdiag(m):
        g, r, c = m.shape
        return (eye[:, None, :, None] * m[:, :, None, :]).reshape(g * r, g * c)

    flat = lambda a: a.astype(F32).reshape(1, N_STATE)
    ls = jnp.broadcast_to(log_step.astype(F32)[:, None], (SSM_GROUPS, SSM_STATE))
    btr = bdiag(jnp.swapaxes(b_re.astype(F32), 1, 2))
    bti = bdiag(jnp.swapaxes(b_im.astype(F32), 1, 2))
    ctr = bdiag(jnp.swapaxes(c_re.astype(F32), 1, 2)).astype(BF16)
    cti = bdiag(jnp.swapaxes(c_im.astype(F32), 1, 2)).astype(BF16)
    return (flat(lam_re), flat(lam_im), flat(ls), btr, bti, ctr, cti,
            d_skip.astype(F32).reshape(1, D_SSM), w_glu.astype(BF16),
            b_glu.astype(F32).reshape(1, D_SSM))


def _swap_halves(x):
    lane = lax.broadcasted_iota(I32, (x.shape[0], LANES), 1)
    first = (lane % HEAD_DIM) < (HEAD_DIM // 2)
    tiles = []
    for t in range(x.shape[1] // LANES):
        xt = x[:, t * LANES:(t + 1) * LANES]
        up = pltpu.roll(xt, LANES - HEAD_DIM // 2, 1)
        dn = pltpu.roll(xt, HEAD_DIM // 2, 1)
        tiles.append(jnp.where(first, up, dn))
    return jnp.concatenate(tiles, axis=1)


def _ret_kernel(r4_ref, s0_ref, cos_ref, sin_ref, din_ref, dq_ref, dk_ref, ds_ref, g_ref,
                y_ref, s_out_ref, s_ref):
    c = pl.program_id(1)
    nc = pl.num_programs(1)

    @pl.when(c == 0)
    def _init():
        s_ref[...] = s0_ref[...]

    q = r4_ref[:, 0:D_RET]
    k = r4_ref[:, D_RET:2 * D_RET]
    v = r4_ref[:, 2 * D_RET:3 * D_RET]
    gate = r4_ref[:, 3 * D_RET:4 * D_RET]
    cos = jnp.concatenate([cos_ref[...]] * (D_RET // LANES), axis=1)
    sin = jnp.concatenate([sin_ref[...]] * (D_RET // LANES), axis=1)
    qr = q * cos + _swap_halves(q) * sin
    kr = (k * cos + _swap_halves(k) * sin) * HEAD_DIM ** -0.5
    qb = qr.astype(BF16)
    kb = kr.astype(BF16)
    vb = v.astype(BF16)
    kdt = (kr * dk_ref[...]).T.astype(BF16)
    dq = dq_ref[...]
    gsil = gate * jax.nn.sigmoid(gate) * g_ref[...]
    sls = [slice(h * HEAD_DIM, (h + 1) * HEAD_DIM) for h in range(RET_HEADS)]
    atts, crosses = [], []
    for h, sl in enumerate(sls):
        qh, kh, vh = qb[:, sl], kb[:, sl], vb[:, sl]
        s = s_ref[h]
        atts.append((lax.dot_general(qh, kh, _NT, preferred_element_type=F32) * din_ref[h]).astype(BF16))
        crosses.append(jnp.dot(qh, s.astype(BF16), preferred_element_type=F32) * dq[:, sl])
        s_ref[h] = s * ds_ref[h] + jnp.dot(kdt[sl, :], vh, preferred_element_type=F32)
    for h, sl in enumerate(sls):
        o = jnp.dot(atts[h], vb[:, sl], preferred_element_type=F32) + crosses[h]
        mu = jnp.mean(o, axis=-1, keepdims=True)
        var = jnp.mean(jnp.square(o - mu), axis=-1, keepdims=True)
        y_ref[:, sl] = ((o - mu) * lax.rsqrt(var + NORM_EPS) * gsil[:, sl]).astype(BF16)

    @pl.when(c == nc - 1)
    def _fin():
        s_out_ref[...] = s_ref[...]


def _ret_tables(pos, chunk):
    half = HEAD_DIM // 2
    inv = ROPE_BASE ** (-jnp.arange(half, dtype=F32) / half)
    ang = pos.astype(F32)[:, None] * inv[None, :]
    cos, sin = jnp.cos(ang), jnp.sin(ang)
    cos_t = jnp.concatenate([cos, cos] * (LANES // HEAD_DIM), axis=1)
    sin_t = jnp.concatenate([-sin, sin] * (LANES // HEAD_DIM), axis=1)
    log_g = jnp.log1p(-jnp.exp2(-5.0 - jnp.arange(RET_HEADS, dtype=F32)))
    j = jnp.arange(chunk, dtype=F32)
    rel = j[:, None] - j[None, :]
    din = jnp.where(rel[None] >= 0, jnp.exp(log_g[:, None, None] * jnp.maximum(rel, 0.0)[None]), 0.0)
    dq = jnp.repeat(jnp.exp(log_g[None, :] * (j[:, None] + 1.0)), HEAD_DIM, axis=1)
    dk = jnp.repeat(jnp.exp(log_g[None, :] * (chunk - 1.0 - j[:, None])), HEAD_DIM, axis=1)
    ds = jnp.broadcast_to(jnp.exp(log_g * chunk)[:, None, None], (RET_HEADS, HEAD_DIM, HEAD_DIM))
    return cos_t, sin_t, din, dq, dk, ds


def _ret(r4, s0, pos, norm_g):
    bv, tv, _ = r4.shape
    chunk = RET_CHUNK if tv % RET_CHUNK == 0 else tv
    cos_t, sin_t, din, dq, dk, ds = _ret_tables(pos, chunk)
    g = norm_g.astype(F32).reshape(1, D_RET)
    full = lambda a: pl.BlockSpec(a.shape, lambda b, c: (0,) * a.ndim)
    st = pl.BlockSpec((None, RET_HEADS, HEAD_DIM, HEAD_DIM), lambda b, c: (b, 0, 0, 0))
    tab = pl.BlockSpec((chunk, LANES), lambda b, c: (c, 0))
    return pl.pallas_call(
        _ret_kernel,
        grid=(bv, tv // chunk),
        in_specs=[pl.BlockSpec((None, chunk, 4 * D_RET), lambda b, c: (b, c, 0)), st, tab, tab,
                  full(din), full(dq), full(dk), full(ds), full(g)],
        out_specs=(pl.BlockSpec((None, chunk, D_RET), lambda b, c: (b, c, 0)), st),
        out_shape=(jax.ShapeDtypeStruct((bv, tv, D_RET), BF16),
                   jax.ShapeDtypeStruct((bv, RET_HEADS, HEAD_DIM, HEAD_DIM), F32)),
        scratch_shapes=[pltpu.VMEM((RET_HEADS, HEAD_DIM, HEAD_DIM), F32)],
        compiler_params=_cparams(2),
        name="retention",
    )(r4, s0, cos_t, sin_t, din, dq, dk, ds, g)


def _count(sc_ref, nk, kb, pred):
    n_acc = 4

    def body(c, cnts):
        r0 = pl.multiple_of(c * kb, kb)
        cnts = list(cnts)
        for j in range(kb // SUBLANES):
            blk = sc_ref[pl.ds(r0 + j * SUBLANES, SUBLANES), :]
            cnts[j % n_acc] = cnts[j % n_acc] + jnp.where(pred(blk), 1, 0)
        return tuple(cnts)
    zero = jnp.zeros((SUBLANES, LANES), I32)
    cnts = lax.fori_loop(0, nk, body, (zero,) * n_acc)
    return jnp.sum((cnts[0] + cnts[1]) + (cnts[2] + cnts[3]), axis=0, keepdims=True)


_ORD_FLIP = 0x7FFFFFFF
_ORD_POS_INF = F32_INF_BITS
_ORD_NEG_INF = -F32_INF_BITS - 1


def _unord(o):
    return lax.bitcast_convert_type(jnp.where(o >= 0, o, o ^ _ORD_FLIP), F32)


def _select_bias(sc_ref, bias_ref, tri_ref, nk, kb, n_beyond, qpos, topk, bufs=None):
    def count_ge(t):
        return _count(sc_ref, nk, kb, lambda blk: blk >= t) + jnp.where(t <= NEG_INF, n_beyond, 0)

    c_zero = count_ge(jnp.zeros((1, LANES), F32))
    pos = jnp.where(c_zero >= topk, 1, 0)
    lo = jnp.where(pos == 1, 0, _ORD_NEG_INF)
    hi = jnp.where(pos == 1, _ORD_POS_INF + 1, 0)
    c_lo = jnp.where(pos == 1, c_zero, nk * kb + n_beyond)
    n_steps = 31

    def step(_, carry):
        lo, hi, c_lo = carry
        mid = lo + lax.shift_right_arithmetic(hi - lo, 1)
        c = count_ge(_unord(mid))
        ok = c >= topk
        return jnp.where(ok, mid, lo), jnp.where(ok, hi, mid), jnp.where(ok, c, c_lo)

    lo, _, c_lo = lax.fori_loop(0, n_steps, step, (lo, hi, c_lo))
    tau = _unord(lo)
    surplus = c_lo - topk
    surplus_ties = jnp.max(surplus) > 0

    @pl.when(jnp.logical_not(surplus_ties))
    def _no_surplus():
        def body(c, _):
            r0 = pl.multiple_of(c * kb, kb)
            blk = sc_ref[pl.ds(r0, kb), :]
            kidx = r0 + lax.broadcasted_iota(I32, (kb, LANES), 0)
            bias_ref[pl.ds(r0, kb), :] = jnp.where((blk >= tau) & (kidx <= qpos), 0.0, NEG_INF)
            return 0
        lax.fori_loop(0, nk, body, 0)

    @pl.when(surplus_ties)
    def _ties_by_index():
        surplus_f = surplus.astype(F32)
        above = jnp.where(tau == NEG_INF, n_beyond, 0).astype(F32)

        def counts(r0, n_rows, buf, above):
            offs = []
            for b in reversed(range(n_rows // TIE_BLOCK)):
                rb = r0 + b * TIE_BLOCK
                eq01 = jnp.where(sc_ref[pl.ds(rb, TIE_BLOCK), :] == tau, 1.0, 0.0)
                buf[b * TIE_BLOCK:(b + 1) * TIE_BLOCK, :] = jnp.dot(tri_ref[...], eq01.astype(BF16),
                                                                    preferred_element_type=F32)
                offs.append(above)
                above = above + jnp.sum(eq01, axis=0, keepdims=True)
            return tuple(offs), above

        def select(r0, n_rows, buf, offs):
            for b, off in zip(reversed(range(n_rows // TIE_BLOCK)), offs):
                rb = r0 + b * TIE_BLOCK
                blk = sc_ref[pl.ds(rb, TIE_BLOCK), :]
                at_or_above = buf[b * TIE_BLOCK:(b + 1) * TIE_BLOCK, :] + off
                kidx = rb + lax.broadcasted_iota(I32, (TIE_BLOCK, LANES), 0)
                sel = ((blk > tau) | ((blk == tau) & (at_or_above > surplus_f))) & (kidx <= qpos)
                bias_ref[pl.ds(rb, TIE_BLOCK), :] = jnp.where(sel, 0.0, NEG_INF)

        if bufs is None:
            def body(c, above):
                r0 = pl.multiple_of((nk - 1 - c) * kb, kb)
                offs, above = counts(r0, kb, bias_ref.at[pl.ds(r0, kb)], above)
                select(r0, kb, bias_ref.at[pl.ds(r0, kb)], offs)
                return above

            lax.fori_loop(0, nk, body, above)
        else:
            half = kb // 2
            buf_hi, buf_lo = bufs

            def body(c, carry):
                above, offs_hi = carry
                cc = nk - 1 - c
                r0 = pl.multiple_of(cc * kb, kb)
                offs_lo, above = counts(r0, half, buf_lo, above)
                select(r0 + half, half, buf_hi, offs_hi)
                r_next = pl.multiple_of(jnp.maximum(cc - 1, 0) * kb, kb)
                offs_hi, above = counts(r_next + half, half, buf_hi, above)
                select(r0, half, buf_lo, offs_lo)
                return above, offs_hi

            offs_hi, above = counts((nk - 1) * kb + half, half, buf_hi, above)
            lax.fori_loop(0, nk, body, (above, offs_hi))


TIE_BLOCK = 128


def _tri():
    r = np.arange(TIE_BLOCK)
    return jnp.asarray((r[None, :] >= r[:, None]).astype(np.float32), dtype=BF16)


def _dsa_prompt_kernel(ikb_ref, akb_ref, avtb_ref, iqb_ref, iwt_ref, aqb_ref, ag_ref, tri_ref,
                       y_ref, sc_ref, bias_ref, lga_ref, lg_ref, *o_refs, kb, t_total, topk):
    i = pl.program_id(1)
    q0 = i * Q_BLOCK
    nk = (q0 + Q_BLOCK + kb - 1) // kb
    qpos = q0 + lax.broadcasted_iota(I32, (1, LANES), 1)
    iq = iqb_ref[...]
    iw = iwt_ref[...]
    qcat = jnp.concatenate([iq[:, h * IDX_DIM:(h + 1) * IDX_DIM] for h in range(IDX_HEADS)], axis=0)

    def score_body(c, _):
        r0 = pl.multiple_of(c * kb, kb)
        keys = ikb_ref[pl.ds(r0, kb), :]
        s = lax.dot_general(keys, qcat, _NT, preferred_element_type=F32)
        parts = [jnp.maximum(s[:, h * LANES:(h + 1) * LANES], 0.0) * iw[h:h + 1, :]
                 for h in range(IDX_HEADS)]
        acc = (parts[0] + parts[1]) + (parts[2] + parts[3])
        kidx = r0 + lax.broadcasted_iota(I32, (kb, LANES), 0)
        sc_ref[pl.ds(r0, kb), :] = jnp.where(kidx <= qpos, acc, NEG_INF)
        return 0

    lax.fori_loop(0, nk, score_body, 0)
    _select_bias(sc_ref, bias_ref, tri_ref, nk, kb, t_total - nk * kb, qpos, topk,
                 bufs=(lga_ref.at[0], lg_ref.at[0]))

    aq = aqb_ref[...]
    lane = lax.broadcasted_iota(I32, (Q_BLOCK, LANES), 1)
    qpairs = []
    for j in range(ATT_HEADS // 2):
        qp = aq[:, j * LANES:(j + 1) * LANES]
        zero = jnp.zeros_like(qp)
        qpairs.append(jnp.concatenate([jnp.where(lane < HEAD_DIM, qp, zero),
                                       jnp.where(lane >= HEAD_DIM, qp, zero)], axis=0))
    for o_ref in o_refs:
        o_ref[...] = jnp.zeros(o_ref.shape, F32)

    half = kb // 2

    def qk(r0, lg):
        bias = bias_ref[pl.ds(r0, half), :]
        mx = []
        for j in range(ATT_HEADS // 2):
            kpair = akb_ref[pl.ds(r0, half), j * LANES:(j + 1) * LANES]
            both = lax.dot_general(kpair, qpairs[j], _NT, preferred_element_type=F32)
            for i in range(2):
                logit = both[:, i * LANES:(i + 1) * LANES] + bias
                lg[2 * j + i] = logit
                mx.append(jnp.max(logit, axis=0, keepdims=True))
        return tuple(mx)

    def pv(r0, lg, ms, mx):
        ms_new = []
        for h in range(ATT_HEADS):
            m_new = jnp.maximum(ms[h], mx[h])
            alpha = jnp.exp2(ms[h] - m_new)
            p = jnp.exp2(lg[h] - m_new)
            ms_new.append(m_new)
            vt = avtb_ref[h * V_ROWS:(h + 1) * V_ROWS, pl.ds(r0, half)]
            o_refs[h][...] = alpha * o_refs[h][...] + jnp.dot(vt, p.astype(BF16), preferred_element_type=F32)
        return tuple(ms_new)

    def att_body(c, carry):
        ms, mx_a = carry
        r0 = pl.multiple_of(c * kb, kb)
        mx_b = qk(r0 + half, lg_ref)
        ms = pv(r0, lga_ref, ms, mx_a)
        r_next = pl.multiple_of(jnp.minimum(c + 1, nk - 1) * kb, kb)
        mx_a = qk(r_next, lga_ref)
        ms = pv(r0 + half, lg_ref, ms, mx_b)
        return ms, mx_a

    m0 = (jnp.full((1, LANES), NEG_INF, F32),) * ATT_HEADS
    lax.fori_loop(0, nk, att_body, (m0, qk(0, lga_ref)))
    ot = jnp.concatenate([o_refs[h][0:HEAD_DIM] / o_refs[h][HEAD_DIM:HEAD_DIM + 1]
                          for h in range(ATT_HEADS)], axis=0)
    gate = ag_ref[...]
    y_ref[...] = (ot.T * (gate * jax.nn.sigmoid(gate))).astype(BF16)


def _dsa_prompt(bv, tv, ikb, akb, avtb, iqb, iwt, aqb, ag, kb=512):
    assert ATT_HEADS % 2 == 0 and 2 * HEAD_DIM == LANES
    topk = min(TOPK_MAX, tv // 4)
    kb = min(kb, tv)
    nq = tv // Q_BLOCK
    tri = _tri()
    per_b = lambda w: pl.BlockSpec((tv, w), lambda b, i: (b, 0))
    qrow = lambda w: pl.BlockSpec((Q_BLOCK, w), lambda b, i: (b * nq + i, 0))
    return pl.pallas_call(
        functools.partial(_dsa_prompt_kernel, kb=kb, t_total=tv, topk=topk),
        grid=(bv, nq),
        in_specs=[per_b(IDX_DIM), per_b(D_ATT),
                  pl.BlockSpec((None, ATT_HEADS * V_ROWS, tv), lambda b, i: (b, 0, 0)),
                  qrow(IDX_HEADS * IDX_DIM),
                  pl.BlockSpec((None, SUBLANES, Q_BLOCK), lambda b, i: (b, 0, i)),
                  qrow(D_ATT), qrow(D_ATT),
                  pl.BlockSpec(tri.shape, lambda b, i: (0, 0))],
        out_specs=qrow(D_ATT),
        out_shape=jax.ShapeDtypeStruct((bv * tv, D_ATT), BF16),
        scratch_shapes=[pltpu.VMEM((tv, LANES), F32), pltpu.VMEM((tv, LANES), F32)]
        + [pltpu.VMEM((ATT_HEADS, kb // 2, LANES), F32)] * 2
        + [pltpu.VMEM((V_ROWS, LANES), F32)] * ATT_HEADS,
        compiler_params=_cparams(2),
        name="dsa_prompt",
    )(ikb, akb, avtb, iqb, iwt, aqb, ag, tri)


def _samp_score_kernel(pt_ref, *refs, npg, ts):
    del pt_ref
    pages = refs[:npg]
    qi_ref, w_ref, iknew_ref, sc_ref, scn_ref = refs[npg:]
    qi = qi_ref[...]
    w = w_ref[...]

    def score(keys_t):
        s = jnp.dot(qi, keys_t.astype(BF16), preferred_element_type=F32)
        r = jnp.maximum(s, 0.0) * w
        return (r[0:ts] + r[ts:2 * ts]) + (r[2 * ts:3 * ts] + r[3 * ts:4 * ts])

    for i in range(npg):
        sc_ref[:, i * PAGE_SIZE:(i + 1) * PAGE_SIZE] = score(pages[i][...])

    @pl.when(pl.program_id(1) == 0)
    def _new_keys():
        sn = score(iknew_ref[...])
        col = lax.broadcasted_iota(I32, (ts, PAGE_SIZE), 1)
        row = lax.broadcasted_iota(I32, (ts, PAGE_SIZE), 0)
        scn_ref[...] = jnp.where(col <= row, sn, NEG_INF)


def _samp_scores(page_table, kidx_t, layer, qi, w, iknew_t, npg):
    bs, n_pages = page_table.shape
    ts = qi.shape[1] // IDX_HEADS
    page_spec = lambda i: pl.BlockSpec(
        (None, None, IDX_DIM, PAGE_SIZE), lambda b, j, pt, i=i: (layer, pt[b, j * npg + i], 0, 0))
    per_b = lambda a: pl.BlockSpec((None,) + a.shape[1:], lambda b, j, pt: (b, 0, 0))
    gs = pltpu.PrefetchScalarGridSpec(
        num_scalar_prefetch=1,
        grid=(bs, n_pages // npg),
        in_specs=[page_spec(i) for i in range(npg)] + [per_b(qi), per_b(w), per_b(iknew_t)],
        out_specs=(pl.BlockSpec((ts, npg * PAGE_SIZE), lambda b, j, pt: (b, j)),
                   pl.BlockSpec((ts, PAGE_SIZE), lambda b, j, pt: (b, 0))),
    )
    return pl.pallas_call(
        functools.partial(_samp_score_kernel, npg=npg, ts=ts),
        grid_spec=gs,
        out_shape=(jax.ShapeDtypeStruct((bs * ts, n_pages * PAGE_SIZE), F32),
                   jax.ShapeDtypeStruct((bs * ts, PAGE_SIZE), F32)),
        compiler_params=_cparams(2),
        name="sample_scores",
    )(page_table, *([kidx_t] * npg), qi, w, iknew_t)


def _select_kernel(sc_ref, qpos_ref, tri_ref, bias_ref, *, kb, nk, topk):
    _select_bias(sc_ref, bias_ref, tri_ref, nk, kb, 0, qpos_ref[...], topk)


def _select(sc_t, qpos, topk, kb):
    nkeys, nq = sc_t.shape
    tri = _tri()
    col = pl.BlockSpec((nkeys, LANES), lambda i: (0, i))
    return pl.pallas_call(
        functools.partial(_select_kernel, kb=kb, nk=nkeys // kb, topk=topk),
        grid=(nq // LANES,),
        in_specs=[col, pl.BlockSpec((1, LANES), lambda i: (0, i)),
                  pl.BlockSpec(tri.shape, lambda i: (0, 0))],
        out_specs=col,
        out_shape=jax.ShapeDtypeStruct((nkeys, nq), F32),
        compiler_params=_cparams(1),
        name="sample_select",
    )(sc_t, qpos, tri)


def _samp_att_kernel(pt_ref, *refs, npg, ts):
    del pt_ref
    kp = refs[:npg]
    vp = refs[npg:2 * npg]
    q_ref, bias_ref, biasn_ref, kn_ref, vn_ref, ag_ref, y_ref, m_ref, l_ref, acc_ref = refs[2 * npg:]
    j = pl.program_id(1)
    nj = pl.num_programs(1)

    @pl.when(j == 0)
    def _init():
        m_ref[...] = jnp.full(m_ref.shape, NEG_INF, F32)
        l_ref[...] = jnp.zeros(l_ref.shape, F32)
        acc_ref[...] = jnp.zeros(acc_ref.shape, F32)

    q = q_ref[...]

    def update(kts, vts, bias):
        logit = jnp.concatenate(
            [jnp.dot(q, kt.astype(BF16), preferred_element_type=F32) for kt in kts], axis=1)
        logit = logit + jnp.concatenate([bias] * ATT_HEADS, axis=0)
        m_old = m_ref[...]
        m_new = jnp.maximum(m_old, jnp.max(logit, axis=1, keepdims=True))
        alpha = jnp.exp2(m_old - m_new)
        p = jnp.exp2(logit - m_new)
        l_ref[...] = alpha * l_ref[...] + jnp.sum(p, axis=1, keepdims=True)
        pb = p.astype(BF16)
        pvs = [lax.dot_general(pb[:, i * PAGE_SIZE:(i + 1) * PAGE_SIZE], vts[i].astype(BF16), _NT,
                               preferred_element_type=F32) for i in range(len(vts))]
        while len(pvs) > 1:
            pvs = [pvs[i] + pvs[i + 1] for i in range(0, len(pvs) - 1, 2)] + pvs[len(pvs) - len(pvs) % 2:]
        acc_ref[...] = alpha * acc_ref[...] + pvs[0]
        m_ref[...] = m_new

    update([r[...] for r in kp], [r[...] for r in vp], bias_ref[...])

    @pl.when(j == nj - 1)
    def _fin():
        update([kn_ref[...]], [vn_ref[...]], biasn_ref[...])
        o = acc_ref[...] / l_ref[...]
        gate = ag_ref[...]
        gsil = gate * jax.nn.sigmoid(gate)
        for h in range(ATT_HEADS):
            sl = slice(h * HEAD_DIM, (h + 1) * HEAD_DIM)
            y_ref[:, sl] = (o[h * ts:(h + 1) * ts, sl] * gsil[:, sl]).astype(BF16)


def _samp_attention(page_table, k_t, v_t, layer, qbd, bias, biasn, knew_t, vnew_t, ag, npg):
    bs, n_pages = page_table.shape
    ts = qbd.shape[1] // ATT_HEADS
    page_spec = lambda i: pl.BlockSpec(
        (None, None, D_ATT, PAGE_SIZE), lambda b, j, pt, i=i: (layer, pt[b, j * npg + i], 0, 0))
    per_b = lambda a: pl.BlockSpec((None,) + a.shape[1:], lambda b, j, pt: (b, 0, 0))
    gs = pltpu.PrefetchScalarGridSpec(
        num_scalar_prefetch=1,
        grid=(bs, n_pages // npg),
        in_specs=[page_spec(i) for i in range(npg)] * 2
        + [per_b(qbd),
           pl.BlockSpec((ts, npg * PAGE_SIZE), lambda b, j, pt: (b, j)),
           pl.BlockSpec((ts, PAGE_SIZE), lambda b, j, pt: (b, 0)),
           per_b(knew_t), per_b(vnew_t),
           pl.BlockSpec((ts, D_ATT), lambda b, j, pt: (b, 0))],
        out_specs=pl.BlockSpec((ts, D_ATT), lambda b, j, pt: (b, 0)),
        scratch_shapes=[pltpu.VMEM((ATT_HEADS * ts, 1), F32), pltpu.VMEM((ATT_HEADS * ts, 1), F32),
                        pltpu.VMEM((ATT_HEADS * ts, D_ATT), F32)],
    )
    return pl.pallas_call(
        functools.partial(_samp_att_kernel, npg=npg, ts=ts),
        grid_spec=gs,
        out_shape=jax.ShapeDtypeStruct((bs * ts, D_ATT), BF16),
        compiler_params=_cparams(2),
        name="sample_attention",
    )(page_table, *([k_t] * npg), *([v_t] * npg), qbd, bias, biasn, knew_t, vnew_t, ag)


def _dsa_sample(page_table, k_t, v_t, kidx_t, layer, ts, akt, avt, ikt, ag, aqb, iqb, iwt):
    bs, n_pages = page_table.shape
    past = n_pages * PAGE_SIZE
    n = bs * ts
    topk = min(TOPK_MAX, (past + ts) // 4)
    npg = math.gcd(n_pages, 16)

    def new_keys_t(a):
        a = a.reshape(a.shape[0], bs, ts).transpose(1, 0, 2)
        return jnp.pad(a, ((0, 0), (0, 0), (0, PAGE_SIZE - ts)))

    qi = iqb.reshape(bs, ts, IDX_HEADS, IDX_DIM).transpose(0, 2, 1, 3).reshape(bs, IDX_HEADS * ts, IDX_DIM)
    w = iwt[:IDX_HEADS].reshape(IDX_HEADS, bs, ts).transpose(1, 0, 2).reshape(bs, IDX_HEADS * ts, 1)
    w = jnp.broadcast_to(w, (bs, IDX_HEADS * ts, LANES))
    sc, scn = _samp_scores(page_table, kidx_t, layer, qi, w, new_keys_t(ikt), npg)
    sc_t = jnp.concatenate([sc, scn], axis=1).T
    qpos = (past + jnp.arange(n, dtype=I32) % ts).reshape(1, n)
    bias_all = _select(sc_t, qpos, topk, PAGE_SIZE).T
    eye = jnp.eye(ATT_HEADS, dtype=BF16)
    q4 = aqb.reshape(bs, ts, ATT_HEADS, HEAD_DIM)
    qbd = (eye[None, :, None, :, None] * q4.transpose(0, 2, 1, 3)[:, :, :, None, :]).reshape(
        bs, ATT_HEADS * ts, D_ATT)
    return _samp_attention(page_table, k_t, v_t, layer, qbd, bias_all[:, :past], bias_all[:, past:],
                           new_keys_t(akt), new_keys_t(avt), ag, npg)


def _mixer_layer(x, pos, h0r, h0i, s0, attend, g, w_in_l, w_out_l, ssm_prm, ret_g, gf, final, tm, tl):
    bv, tv, _ = x.shape
    n = bv * tv
    wm = w_in_l[:, :C_IW].astype(BF16)
    wt = jnp.concatenate([w_in_l[:, C_AK:C_AG].T, w_in_l[:, C_IK:C_END].T,
                          jnp.zeros((SUBLANES - IDX_HEADS, D_MODEL), w_in_l.dtype)], axis=0).astype(BF16)
    xf = x.reshape(n, D_MODEL)
    pb, pt = (bv, tv) if tv % tm == 0 else (1, n)
    tm = min(tm, pt)
    (u2, r4, ag, akt, avt, ikt, aqb, akb, avtb, iqb, ikb, iwt) = _inproj(
        xf, pb, pt, g.astype(F32).reshape(1, D_MODEL), wm, wt, tm)
    y_ssm, hr, hi = _ssm(u2.reshape(bv, tv, -1), h0r, h0i, ssm_prm, tl)
    y_ret, s_new = _ret(r4.reshape(bv, tv, -1), s0, pos, ret_g)
    y_att = attend(ag, akt, avt, ikt, aqb, akb, avtb, iqb, ikb, iwt)
    y = _outproj(xf, y_ssm.reshape(n, D_SSM), y_ret.reshape(n, D_RET), y_att,
                 w_out_l.astype(BF16), gf.astype(F32).reshape(1, D_MODEL), tm, final)
    tok = lambda a: a.reshape(pb, a.shape[1], -1, tv).transpose(0, 2, 3, 1).reshape(bv, tv, a.shape[1])
    return y.reshape(bv, tv, D_MODEL), (tok(akt), tok(avt), tok(ikt), hr, hi, s_new)


def kernel(x_prompt, x_sample, cache_k, cache_v, cache_kidx, state_ssm_re, state_ssm_im, state_ret,
           page_table, norm_g, w_in, w_out, ssm_lambda_re, ssm_lambda_im, ssm_b_re, ssm_b_im,
           ssm_c_re, ssm_c_im, ssm_d, ssm_log_step, ssm_w_glu, ssm_b_glu, ret_norm_g, final_norm_g):
    bp, tp, _ = x_prompt.shape
    bs, ts, _ = x_sample.shape
    depth = w_in.shape[0]
    past = page_table.shape[1] * PAGE_SIZE
    assert ts == SUBLANES and tp % Q_BLOCK == 0
    pos_p = jnp.arange(tp)
    pos_s = past + jnp.arange(ts)
    k_t = jnp.transpose(cache_k, (0, 1, 3, 4, 2)).reshape(cache_k.shape[:2] + (D_ATT, PAGE_SIZE))
    v_t = jnp.transpose(cache_v, (0, 1, 3, 4, 2)).reshape(cache_v.shape[:2] + (D_ATT, PAGE_SIZE))
    kidx_t = jnp.transpose(cache_kidx, (0, 1, 3, 2))
    zero_h = jnp.zeros((bp, 1, N_STATE), F32)
    zero_s = jnp.zeros((bp, RET_HEADS, HEAD_DIM, HEAD_DIM), F32)
    hp, hs = x_prompt, x_sample
    outs_p, outs_s = [], []
    for l in range(depth):
        final = l == depth - 1
        prm = _ssm_params(ssm_lambda_re[l], ssm_lambda_im[l], ssm_b_re[l], ssm_b_im[l], ssm_c_re[l],
                          ssm_c_im[l], ssm_d[l], ssm_log_step[l], ssm_w_glu[l], ssm_b_glu[l])

        def attend_p(ag, akt, avt, ikt, aqb, akb, avtb, iqb, ikb, iwt):
            return _dsa_prompt(bp, tp, ikb, akb, avtb, iqb, iwt, aqb, ag)

        hp, st_p = _mixer_layer(hp, pos_p, zero_h, zero_h, zero_s, attend_p, norm_g[l], w_in[l], w_out[l],
                                prm, ret_norm_g[l], final_norm_g, final, tm=256, tl=min(512, tp))

        def attend_s(ag, akt, avt, ikt, aqb, akb, avtb, iqb, ikb, iwt):
            return _dsa_sample(page_table, k_t, v_t, kidx_t, l, ts, akt[0], avt[0], ikt[0], ag,
                               aqb, iqb, iwt[0])

        hs, st_s = _mixer_layer(hs, pos_s, state_ssm_re[l].reshape(bs, 1, N_STATE),
                                state_ssm_im[l].reshape(bs, 1, N_STATE), state_ret[l], attend_s,
                                norm_g[l], w_in[l], w_out[l], prm, ret_norm_g[l], final_norm_g, final,
                                tm=256, tl=ts)
        outs_p.append(st_p)
        outs_s.append(st_s)

    def pack(outs, b, t):
        k = jnp.stack([o[0] for o in outs]).reshape(depth, b, t, ATT_HEADS, HEAD_DIM)
        v = jnp.stack([o[1] for o in outs]).reshape(depth, b, t, ATT_HEADS, HEAD_DIM)
        ki = jnp.stack([o[2] for o in outs])
        hr = jnp.stack([o[3] for o in outs]).reshape(depth, b, SSM_GROUPS, SSM_STATE)
        hi = jnp.stack([o[4] for o in outs]).reshape(depth, b, SSM_GROUPS, SSM_STATE)
        s = jnp.stack([o[5] for o in outs])
        return k, v, ki, hr, hi, s

    return (hp, hs) + pack(outs_p, bp, tp) + pack(outs_s, bs, ts)
```

```python
import functools
import math

import jax
import jax.numpy as jnp
import numpy as np
from jax import lax
from jax.experimental import pallas as pl
from jax.experimental.pallas import tpu as pltpu

F32 = jnp.float32
BF16 = jnp.bfloat16
I32 = jnp.int32

D_MODEL = 1024
HEAD_DIM = 64
SSM_GROUP = 16
D_SSM = 256
SSM_GROUPS = D_SSM // SSM_GROUP
SSM_STATE = 64
N_STATE = SSM_GROUPS * SSM_STATE
D_RET = 384
RET_HEADS = D_RET // HEAD_DIM
RET_CHUNK = 128
ROPE_BASE = 10000.0
D_ATT = 384
ATT_HEADS = D_ATT // HEAD_DIM
IDX_HEADS = 4
IDX_DIM = 64
TOPK_MAX = 256
Q_BLOCK = 128
PAGE_SIZE = 128
NORM_EPS = 1e-6
NEG_INF = -1e30

_SIZES = (D_SSM, D_SSM, D_RET, D_RET, D_RET, D_RET, D_ATT, D_ATT, D_ATT, D_ATT,
          IDX_HEADS * IDX_DIM, IDX_DIM, IDX_HEADS)
_OFF = np.concatenate([[0], np.cumsum(_SIZES)]).tolist()
C_U, C_RET, C_AQ, C_AK, C_AV, C_AG, C_IQ, C_IK, C_IW, C_END = (
    _OFF[0], _OFF[2], _OFF[6], _OFF[7], _OFF[8], _OFF[9], _OFF[10], _OFF[11], _OFF[12], _OFF[13])

LANES = 128
SUBLANES = 8
F32_INF_BITS = 0x7F800000
LOG2_E = 1.4426950408889634
V_ROWS = HEAD_DIM + 16
VMEM_LIMIT = 56 * 1024 * 1024

_NT = (((1,), (1,)), ((), ()))


def _cparams(n_axes):
    return pltpu.CompilerParams(dimension_semantics=("arbitrary",) * n_axes,
                                vmem_limit_bytes=VMEM_LIMIT)


_T_AK, _T_AV, _T_IK, _T_IW, _T_END = 0, D_ATT, 2 * D_ATT, 2 * D_ATT + IDX_DIM, 2 * D_ATT + IDX_DIM + SUBLANES


def _inproj_kernel(x_ref, g_ref, wm_ref, wt_ref,
                   u2_ref, r4_ref, ag_ref, akt_ref, avt_ref, ikt_ref,
                   aqb_ref, akb_ref, avtb_ref, iqb_ref, ikb_ref, iwt_ref):
    x = x_ref[...]
    ms = jnp.mean(x * x, axis=-1, keepdims=True)
    xn = (x * lax.rsqrt(ms + NORM_EPS) * g_ref[...]).astype(BF16)

    def seg(a, b):
        return jnp.dot(xn, wm_ref[:, a:b], preferred_element_type=F32)

    u2_ref[...] = seg(C_U, C_RET)
    r4_ref[...] = seg(C_RET, C_AQ)
    aqb_ref[...] = (seg(C_AQ, C_AK) * (LOG2_E * HEAD_DIM ** -0.5)).astype(BF16)
    akb_ref[...] = seg(C_AK, C_AV).astype(BF16)
    ag_ref[...] = seg(C_AG, C_IQ)
    iqb_ref[...] = seg(C_IQ, C_IK).astype(BF16)
    ikb_ref[...] = seg(C_IK, C_IW).astype(BF16)
    zt = lax.dot_general(wt_ref[...], xn, _NT, preferred_element_type=F32)
    akt_ref[...] = zt[_T_AK:_T_AV]
    avt = zt[_T_AV:_T_IK]
    avt_ref[...] = avt
    ones = jnp.ones((V_ROWS - HEAD_DIM, avt.shape[1]), BF16)
    for h in range(ATT_HEADS):
        avtb_ref[h * V_ROWS:h * V_ROWS + HEAD_DIM, :] = avt[h * HEAD_DIM:(h + 1) * HEAD_DIM].astype(BF16)
        avtb_ref[h * V_ROWS + HEAD_DIM:(h + 1) * V_ROWS, :] = ones
    ikt_ref[...] = zt[_T_IK:_T_IW]
    iwt_ref[...] = zt[_T_IW:_T_END] * (IDX_HEADS ** -0.5 * IDX_DIM ** -0.5)


def _inproj(x, bv, tv, g, wm, wt, tm):
    n = x.shape[0]
    nt = tv // tm
    row = lambda w: pl.BlockSpec((tm, w), lambda i: (i, 0))
    col = lambda h: pl.BlockSpec((None, h, tm), lambda i: (i // nt, 0, i % nt))
    full = lambda a: pl.BlockSpec(a.shape, lambda i: (0,) * a.ndim)
    out_shape = (
        jax.ShapeDtypeStruct((n, C_RET - C_U), F32),
        jax.ShapeDtypeStruct((n, C_AQ - C_RET), F32),
        jax.ShapeDtypeStruct((n, D_ATT), F32),
        jax.ShapeDtypeStruct((bv, D_ATT, tv), F32),
        jax.ShapeDtypeStruct((bv, D_ATT, tv), F32),
        jax.ShapeDtypeStruct((bv, IDX_DIM, tv), F32),
        jax.ShapeDtypeStruct((n, D_ATT), BF16),
        jax.ShapeDtypeStruct((n, D_ATT), BF16),
        jax.ShapeDtypeStruct((bv, ATT_HEADS * V_ROWS, tv), BF16),
        jax.ShapeDtypeStruct((n, IDX_HEADS * IDX_DIM), BF16),
        jax.ShapeDtypeStruct((n, IDX_DIM), BF16),
        jax.ShapeDtypeStruct((bv, SUBLANES, tv), F32),
    )
    out_specs = (
        row(C_RET - C_U), row(C_AQ - C_RET), row(D_ATT), col(D_ATT), col(D_ATT), col(IDX_DIM),
        row(D_ATT), row(D_ATT), col(ATT_HEADS * V_ROWS), row(IDX_HEADS * IDX_DIM),
        row(IDX_DIM), col(SUBLANES))
    return pl.pallas_call(
        _inproj_kernel,
        grid=(n // tm,),
        in_specs=[row(D_MODEL), full(g), full(wm), full(wt)],
        out_specs=out_specs,
        out_shape=out_shape,
        compiler_params=_cparams(1),
        name="inproj",
    )(x, g, wm, wt)


def _outproj_kernel(x_ref, ys_ref, yr_ref, ya_ref, wo_ref, gf_ref, o_ref, *, final):
    y = x_ref[...]
    y = y + jnp.dot(ys_ref[...], wo_ref[0:D_SSM, :], preferred_element_type=F32)
    y = y + jnp.dot(yr_ref[...], wo_ref[D_SSM:D_SSM + D_RET, :], preferred_element_type=F32)
    y = y + jnp.dot(ya_ref[...], wo_ref[D_SSM + D_RET:, :], preferred_element_type=F32)
    if final:
        ms = jnp.mean(y * y, axis=-1, keepdims=True)
        y = y * lax.rsqrt(ms + NORM_EPS) * gf_ref[...]
    o_ref[...] = y


def _outproj(x, ys, yr, ya, wo, gf, tm, final):
    n = x.shape[0]
    row = lambda w: pl.BlockSpec((tm, w), lambda i: (i, 0))
    full = lambda a: pl.BlockSpec(a.shape, lambda i: (0,) * a.ndim)
    return pl.pallas_call(
        functools.partial(_outproj_kernel, final=final),
        grid=(n // tm,),
        in_specs=[row(D_MODEL), row(D_SSM), row(D_RET), row(D_ATT), full(wo), full(gf)],
        out_specs=row(D_MODEL),
        out_shape=jax.ShapeDtypeStruct((n, D_MODEL), F32),
        compiler_params=_cparams(1),
        name="outproj",
    )(x, ys, yr, ya, wo, gf)


def _cmul(ar, ai, br, bi):
    return ar * br - ai * bi, ar * bi + ai * br


def _ssm_kernel(u2_ref, h0r_ref, h0i_ref, lr_ref, li_ref, ls_ref, btr_ref, bti_ref,
                ctr_ref, cti_ref, d_ref, wg_ref, bg_ref,
                y_ref, hr_out_ref, hi_out_ref,
                bbr_ref, bbi_ref, apr_ref, api_ref, cr_ref, ci_ref, hr_ref, hi_ref, *, tl):
    b = pl.program_id(0)
    c = pl.program_id(1)
    nc = pl.num_programs(1)

    @pl.when((b == 0) & (c == 0))
    def _prep():
        lr, li = lr_ref[...], li_ref[...]
        dt = jnp.exp(ls_ref[...])
        mag = jnp.exp(lr * dt)
        abr, abi = mag * jnp.cos(li * dt), mag * jnp.sin(li * dt)
        den = lr * lr + li * li
        nr, ni = abr - 1.0, abi
        fr = (nr * lr + ni * li) / den
        fi = (ni * lr - nr * li) / den
        bbr_ref[...] = (fr * btr_ref[...] - fi * bti_ref[...]).astype(BF16)
        bbi_ref[...] = (fr * bti_ref[...] + fi * btr_ref[...]).astype(BF16)
        pr, pi = abr, abi
        rows_r, rows_i = [pr], [pi]
        for _ in range(SUBLANES - 1):
            pr, pi = _cmul(pr, pi, abr, abi)
            rows_r.append(pr)
            rows_i.append(pi)
        apr_ref[...] = jnp.concatenate(rows_r, axis=0)
        api_ref[...] = jnp.concatenate(rows_i, axis=0)

    @pl.when(c == 0)
    def _init():
        cr_ref[...] = h0r_ref[...]
        ci_ref[...] = h0i_ref[...]

    u = u2_ref[:, 0:D_SSM]
    gate = u2_ref[:, D_SSM:2 * D_SSM]
    ub = u.astype(BF16)
    hr_ref[...] = jnp.dot(ub, bbr_ref[...], preferred_element_type=F32)
    hi_ref[...] = jnp.dot(ub, bbi_ref[...], preferred_element_type=F32)

    apr, api = apr_ref[...], api_ref[...]
    rowid = lax.broadcasted_iota(I32, (SUBLANES, N_STATE), 0)

    def blk(j, carry):
        cr, ci = carry
        r0 = pl.multiple_of(j * SUBLANES, SUBLANES)
        br = hr_ref[pl.ds(r0, SUBLANES), :]
        bi = hi_ref[pl.ds(r0, SUBLANES), :]
        for k in (1, 2, 4):
            keep = rowid >= k
            sr = jnp.where(keep, pltpu.roll(br, k, 0), 0.0)
            si = jnp.where(keep, pltpu.roll(bi, k, 0), 0.0)
            akr, aki = apr[k - 1:k, :], api[k - 1:k, :]
            br, bi = br + (akr * sr - aki * si), bi + (akr * si + aki * sr)
        hr = br + (apr * cr - api * ci)
        hi = bi + (apr * ci + api * cr)
        hr_ref[pl.ds(r0, SUBLANES), :] = hr
        hi_ref[pl.ds(r0, SUBLANES), :] = hi
        return hr[SUBLANES - 1:SUBLANES, :], hi[SUBLANES - 1:SUBLANES, :]

    cr, ci = lax.fori_loop(0, tl // SUBLANES, blk, (cr_ref[...], ci_ref[...]))
    cr_ref[...] = cr
    ci_ref[...] = ci

    @pl.when(c == nc - 1)
    def _fin():
        hr_out_ref[...] = cr
        hi_out_ref[...] = ci

    y = (jnp.dot(hr_ref[...].astype(BF16), ctr_ref[...], preferred_element_type=F32)
         - jnp.dot(hi_ref[...].astype(BF16), cti_ref[...], preferred_element_type=F32)
         + d_ref[...] * u)
    g = jax.nn.gelu(y)
    y = g * jax.nn.sigmoid(jnp.dot(g.astype(BF16), wg_ref[...], preferred_element_type=F32) + bg_ref[...])
    y_ref[...] = (y * (gate * jax.nn.sigmoid(gate))).astype(BF16)


def _ssm(u2, h0r, h0i, prm, tl):
    bv, tv, _ = u2.shape
    full = lambda a: pl.BlockSpec(a.shape, lambda b, c: (0,) * a.ndim)
    st = pl.BlockSpec((None, 1, N_STATE), lambda b, c: (b, 0, 0))
    return pl.pallas_call(
        functools.partial(_ssm_kernel, tl=tl),
        grid=(bv, tv // tl),
        in_specs=[pl.BlockSpec((None, tl, 2 * D_SSM), lambda b, c: (b, c, 0)), st, st]
        + [full(a) for a in prm],
        out_specs=(pl.BlockSpec((None, tl, D_SSM), lambda b, c: (b, c, 0)), st, st),
        out_shape=(jax.ShapeDtypeStruct((bv, tv, D_SSM), BF16),
                   jax.ShapeDtypeStruct((bv, 1, N_STATE), F32),
                   jax.ShapeDtypeStruct((bv, 1, N_STATE), F32)),
        scratch_shapes=[pltpu.VMEM((D_SSM, N_STATE), BF16), pltpu.VMEM((D_SSM, N_STATE), BF16),
                        pltpu.VMEM((SUBLANES, N_STATE), F32), pltpu.VMEM((SUBLANES, N_STATE), F32),
                        pltpu.VMEM((1, N_STATE), F32), pltpu.VMEM((1, N_STATE), F32),
                        pltpu.VMEM((tl, N_STATE), F32), pltpu.VMEM((tl, N_STATE), F32)],
        compiler_params=_cparams(2),
        name="ssm",
    )(u2, h0r, h0i, *prm)


def _ssm_params(lam_re, lam_im, b_re, b_im, c_re, c_im, d_skip, log_step, w_glu, b_glu):
    eye = jnp.eye(SSM_GROUPS, dtype=F32)

    def bdiag(m):
        g, r, c = m.shape
        return (eye[:, None, :, None] * m[:, :, None, :]).reshape(g * r, g * c)

    flat = lambda a: a.astype(F32).reshape(1, N_STATE)
    ls = jnp.broadcast_to(log_step.astype(F32)[:, None], (SSM_GROUPS, SSM_STATE))
    btr = bdiag(jnp.swapaxes(b_re.astype(F32), 1, 2))
    bti = bdiag(jnp.swapaxes(b_im.astype(F32), 1, 2))
    ctr = bdiag(jnp.swapaxes(c_re.astype(F32), 1, 2)).astype(BF16)
    cti = bdiag(jnp.swapaxes(c_im.astype(F32), 1, 2)).astype(BF16)
    return (flat(lam_re), flat(lam_im), flat(ls), btr, bti, ctr, cti,
            d_skip.astype(F32).reshape(1, D_SSM), w_glu.astype(BF16),
            b_glu.astype(F32).reshape(1, D_SSM))


def _swap_halves(x):
    lane = lax.broadcasted_iota(I32, (x.shape[0], LANES), 1)
    first = (lane % HEAD_DIM) < (HEAD_DIM // 2)
    tiles = []
    for t in range(x.shape[1] // LANES):
        xt = x[:, t * LANES:(t + 1) * LANES]
        up = pltpu.roll(xt, LANES - HEAD_DIM // 2, 1)
        dn = pltpu.roll(xt, HEAD_DIM // 2, 1)
        tiles.append(jnp.where(first, up, dn))
    return jnp.concatenate(tiles, axis=1)


def _ret_kernel(r4_ref, s0_ref, cos_ref, sin_ref, din_ref, dq_ref, dk_ref, ds_ref, g_ref,
                y_ref, s_out_ref, s_ref):
    c = pl.program_id(1)
    nc = pl.num_programs(1)

    @pl.when(c == 0)
    def _init():
        s_ref[...] = s0_ref[...]

    q = r4_ref[:, 0:D_RET]
    k = r4_ref[:, D_RET:2 * D_RET]
    v = r4_ref[:, 2 * D_RET:3 * D_RET]
    gate = r4_ref[:, 3 * D_RET:4 * D_RET]
    cos = jnp.concatenate([cos_ref[...]] * (D_RET // LANES), axis=1)
    sin = jnp.concatenate([sin_ref[...]] * (D_RET // LANES), axis=1)
    qr = q * cos + _swap_halves(q) * sin
    kr = (k * cos + _swap_halves(k) * sin) * HEAD_DIM ** -0.5
    qb = qr.astype(BF16)
    kb = kr.astype(BF16)
    vb = v.astype(BF16)
    kdt = (kr * dk_ref[...]).T.astype(BF16)
    dq = dq_ref[...]
    gsil = gate * jax.nn.sigmoid(gate) * g_ref[...]
    sls = [slice(h * HEAD_DIM, (h + 1) * HEAD_DIM) for h in range(RET_HEADS)]
    atts, crosses = [], []
    for h, sl in enumerate(sls):
        qh, kh, vh = qb[:, sl], kb[:, sl], vb[:, sl]
        s = s_ref[h]
        atts.append((lax.dot_general(qh, kh, _NT, preferred_element_type=F32) * din_ref[h]).astype(BF16))
        crosses.append(jnp.dot(qh, s.astype(BF16), preferred_element_type=F32) * dq[:, sl])
        s_ref[h] = s * ds_ref[h] + jnp.dot(kdt[sl, :], vh, preferred_element_type=F32)
    for h, sl in enumerate(sls):
        o = jnp.dot(atts[h], vb[:, sl], preferred_element_type=F32) + crosses[h]
        mu = jnp.mean(o, axis=-1, keepdims=True)
        var = jnp.mean(jnp.square(o - mu), axis=-1, keepdims=True)
        y_ref[:, sl] = ((o - mu) * lax.rsqrt(var + NORM_EPS) * gsil[:, sl]).astype(BF16)

    @pl.when(c == nc - 1)
    def _fin():
        s_out_ref[...] = s_ref[...]


def _ret_tables(pos, chunk):
    half = HEAD_DIM // 2
    inv = ROPE_BASE ** (-jnp.arange(half, dtype=F32) / half)
    ang = pos.astype(F32)[:, None] * inv[None, :]
    cos, sin = jnp.cos(ang), jnp.sin(ang)
    cos_t = jnp.concatenate([cos, cos] * (LANES // HEAD_DIM), axis=1)
    sin_t = jnp.concatenate([-sin, sin] * (LANES // HEAD_DIM), axis=1)
    log_g = jnp.log1p(-jnp.exp2(-5.0 - jnp.arange(RET_HEADS, dtype=F32)))
    j = jnp.arange(chunk, dtype=F32)
    rel = j[:, None] - j[None, :]
    din = jnp.where(rel[None] >= 0, jnp.exp(log_g[:, None, None] * jnp.maximum(rel, 0.0)[None]), 0.0)
    dq = jnp.repeat(jnp.exp(log_g[None, :] * (j[:, None] + 1.0)), HEAD_DIM, axis=1)
    dk = jnp.repeat(jnp.exp(log_g[None, :] * (chunk - 1.0 - j[:, None])), HEAD_DIM, axis=1)
    ds = jnp.broadcast_to(jnp.exp(log_g * chunk)[:, None, None], (RET_HEADS, HEAD_DIM, HEAD_DIM))
    return cos_t, sin_t, din, dq, dk, ds


def _ret(r4, s0, pos, norm_g):
    bv, tv, _ = r4.shape
    chunk = RET_CHUNK if tv % RET_CHUNK == 0 else tv
    cos_t, sin_t, din, dq, dk, ds = _ret_tables(pos, chunk)
    g = norm_g.astype(F32).reshape(1, D_RET)
    full = lambda a: pl.BlockSpec(a.shape, lambda b, c: (0,) * a.ndim)
    st = pl.BlockSpec((None, RET_HEADS, HEAD_DIM, HEAD_DIM), lambda b, c: (b, 0, 0, 0))
    tab = pl.BlockSpec((chunk, LANES), lambda b, c: (c, 0))
    return pl.pallas_call(
        _ret_kernel,
        grid=(bv, tv // chunk),
        in_specs=[pl.BlockSpec((None, chunk, 4 * D_RET), lambda b, c: (b, c, 0)), st, tab, tab,
                  full(din), full(dq), full(dk), full(ds), full(g)],
        out_specs=(pl.BlockSpec((None, chunk, D_RET), lambda b, c: (b, c, 0)), st),
        out_shape=(jax.ShapeDtypeStruct((bv, tv, D_RET), BF16),
                   jax.ShapeDtypeStruct((bv, RET_HEADS, HEAD_DIM, HEAD_DIM), F32)),
        scratch_shapes=[pltpu.VMEM((RET_HEADS, HEAD_DIM, HEAD_DIM), F32)],
        compiler_params=_cparams(2),
        name="retention",
    )(r4, s0, cos_t, sin_t, din, dq, dk, ds, g)


def _count(sc_ref, nk, kb, pred):
    n_acc = 4

    def body(c, cnts):
        r0 = pl.multiple_of(c * kb, kb)
        cnts = list(cnts)
        for j in range(kb // SUBLANES):
            blk = sc_ref[pl.ds(r0 + j * SUBLANES, SUBLANES), :]
            cnts[j % n_acc] = cnts[j % n_acc] + jnp.where(pred(blk), 1, 0)
        return tuple(cnts)
    zero = jnp.zeros((SUBLANES, LANES), I32)
    cnts = lax.fori_loop(0, nk, body, (zero,) * n_acc)
    return jnp.sum((cnts[0] + cnts[1]) + (cnts[2] + cnts[3]), axis=0, keepdims=True)


_ORD_FLIP = 0x7FFFFFFF
_ORD_POS_INF = F32_INF_BITS
_ORD_NEG_INF = -F32_INF_BITS - 1


def _unord(o):
    return lax.bitcast_convert_type(jnp.where(o >= 0, o, o ^ _ORD_FLIP), F32)


def _select_bias(sc_ref, bias_ref, tri_ref, nk, kb, n_beyond, qpos, topk, bufs=None):
    def count_ge(t):
        return _count(sc_ref, nk, kb, lambda blk: blk >= t) + jnp.where(t <= NEG_INF, n_beyond, 0)

    c_zero = count_ge(jnp.zeros((1, LANES), F32))
    pos = jnp.where(c_zero >= topk, 1, 0)
    lo = jnp.where(pos == 1, 0, _ORD_NEG_INF)
    hi = jnp.where(pos == 1, _ORD_POS_INF + 1, 0)
    c_lo = jnp.where(pos == 1, c_zero, nk * kb + n_beyond)
    n_steps = 31

    def step(_, carry):
        lo, hi, c_lo = carry
        mid = lo + lax.shift_right_arithmetic(hi - lo, 1)
        c = count_ge(_unord(mid))
        ok = c >= topk
        return jnp.where(ok, mid, lo), jnp.where(ok, hi, mid), jnp.where(ok, c, c_lo)

    lo, _, c_lo = lax.fori_loop(0, n_steps, step, (lo, hi, c_lo))
    tau = _unord(lo)
    surplus = c_lo - topk
    surplus_ties = jnp.max(surplus) > 0

    @pl.when(jnp.logical_not(surplus_ties))
    def _no_surplus():
        def body(c, _):
            r0 = pl.multiple_of(c * kb, kb)
            blk = sc_ref[pl.ds(r0, kb), :]
            kidx = r0 + lax.broadcasted_iota(I32, (kb, LANES), 0)
            bias_ref[pl.ds(r0, kb), :] = jnp.where((blk >= tau) & (kidx <= qpos), 0.0, NEG_INF)
            return 0
        lax.fori_loop(0, nk, body, 0)

    @pl.when(surplus_ties)
    def _ties_by_index():
        surplus_f = surplus.astype(F32)
        above = jnp.where(tau == NEG_INF, n_beyond, 0).astype(F32)

        def counts(r0, n_rows, buf, above):
            offs = []
            for b in reversed(range(n_rows // TIE_BLOCK)):
                rb = r0 + b * TIE_BLOCK
                eq01 = jnp.where(sc_ref[pl.ds(rb, TIE_BLOCK), :] == tau, 1.0, 0.0)
                buf[b * TIE_BLOCK:(b + 1) * TIE_BLOCK, :] = jnp.dot(tri_ref[...], eq01.astype(BF16),
                                                                    preferred_element_type=F32)
                offs.append(above)
                above = above + jnp.sum(eq01, axis=0, keepdims=True)
            return tuple(offs), above

        def select(r0, n_rows, buf, offs):
            for b, off in zip(reversed(range(n_rows // TIE_BLOCK)), offs):
                rb = r0 + b * TIE_BLOCK
                blk = sc_ref[pl.ds(rb, TIE_BLOCK), :]
                at_or_above = buf[b * TIE_BLOCK:(b + 1) * TIE_BLOCK, :] + off
                kidx = rb + lax.broadcasted_iota(I32, (TIE_BLOCK, LANES), 0)
                sel = ((blk > tau) | ((blk == tau) & (at_or_above > surplus_f))) & (kidx <= qpos)
                bias_ref[pl.ds(rb, TIE_BLOCK), :] = jnp.where(sel, 0.0, NEG_INF)

        if bufs is None:
            def body(c, above):
                r0 = pl.multiple_of((nk - 1 - c) * kb, kb)
                offs, above = counts(r0, kb, bias_ref.at[pl.ds(r0, kb)], above)
                select(r0, kb, bias_ref.at[pl.ds(r0, kb)], offs)
                return above

            lax.fori_loop(0, nk, body, above)
        else:
            half = kb // 2
            buf_hi, buf_lo = bufs

            def body(c, carry):
                above, offs_hi = carry
                cc = nk - 1 - c
                r0 = pl.multiple_of(cc * kb, kb)
                offs_lo, above = counts(r0, half, buf_lo, above)
                select(r0 + half, half, buf_hi, offs_hi)
                r_next = pl.multiple_of(jnp.maximum(cc - 1, 0) * kb, kb)
                offs_hi, above = counts(r_next + half, half, buf_hi, above)
                select(r0, half, buf_lo, offs_lo)
                return above, offs_hi

            offs_hi, above = counts((nk - 1) * kb + half, half, buf_hi, above)
            lax.fori_loop(0, nk, body, (above, offs_hi))


TIE_BLOCK = 128


def _tri():
    r = np.arange(TIE_BLOCK)
    return jnp.asarray((r[None, :] >= r[:, None]).astype(np.float32), dtype=BF16)


def _dsa_prompt_kernel(ikb_ref, akb_ref, avtb_ref, iqb_ref, iwt_ref, aqb_ref, ag_ref, tri_ref,
                       y_ref, sc_ref, bias_ref, lga_ref, lg_ref, *o_refs, kb, t_total, topk):
    i = pl.program_id(1)
    q0 = i * Q_BLOCK
    nk = (q0 + Q_BLOCK + kb - 1) // kb
    qpos = q0 + lax.broadcasted_iota(I32, (1, LANES), 1)
    iq = iqb_ref[...]
    iw = iwt_ref[...]
    qcat = jnp.concatenate([iq[:, h * IDX_DIM:(h + 1) * IDX_DIM] for h in range(IDX_HEADS)], axis=0)

    def score_body(c, _):
        r0 = pl.multiple_of(c * kb, kb)
        keys = ikb_ref[pl.ds(r0, kb), :]
        s = lax.dot_general(keys, qcat, _NT, preferred_element_type=F32)
        parts = [jnp.maximum(s[:, h * LANES:(h + 1) * LANES], 0.0) * iw[h:h + 1, :]
                 for h in range(IDX_HEADS)]
        acc = (parts[0] + parts[1]) + (parts[2] + parts[3])
        kidx = r0 + lax.broadcasted_iota(I32, (kb, LANES), 0)
        sc_ref[pl.ds(r0, kb), :] = jnp.where(kidx <= qpos, acc, NEG_INF)
        return 0

    lax.fori_loop(0, nk, score_body, 0)
    _select_bias(sc_ref, bias_ref, tri_ref, nk, kb, t_total - nk * kb, qpos, topk,
                 bufs=(lga_ref.at[0], lg_ref.at[0]))

    aq = aqb_ref[...]
    lane = lax.broadcasted_iota(I32, (Q_BLOCK, LANES), 1)
    qpairs = []
    for j in range(ATT_HEADS // 2):
        qp = aq[:, j * LANES:(j + 1) * LANES]
        zero = jnp.zeros_like(qp)
        qpairs.append(jnp.concatenate([jnp.where(lane < HEAD_DIM, qp, zero),
                                       jnp.where(lane >= HEAD_DIM, qp, zero)], axis=0))
    for o_ref in o_refs:
        o_ref[...] = jnp.zeros(o_ref.shape, F32)

    half = kb // 2

    def qk(r0, lg):
        bias = bias_ref[pl.ds(r0, half), :]
        mx = []
        for j in range(ATT_HEADS // 2):
            kpair = akb_ref[pl.ds(r0, half), j * LANES:(j + 1) * LANES]
            both = lax.dot_general(kpair, qpairs[j], _NT, preferred_element_type=F32)
            for i in range(2):
                logit = both[:, i * LANES:(i + 1) * LANES] + bias
                lg[2 * j + i] = logit
                mx.append(jnp.max(logit, axis=0, keepdims=True))
        return tuple(mx)

    def pv(r0, lg, ms, mx):
        ms_new = []
        for h in range(ATT_HEADS):
            m_new = jnp.maximum(ms[h], mx[h])
            alpha = jnp.exp2(ms[h] - m_new)
            p = jnp.exp2(lg[h] - m_new)
            ms_new.append(m_new)
            vt = avtb_ref[h * V_ROWS:(h + 1) * V_ROWS, pl.ds(r0, half)]
            o_refs[h][...] = alpha * o_refs[h][...] + jnp.dot(vt, p.astype(BF16), preferred_element_type=F32)
        return tuple(ms_new)

    def att_body(c, carry):
        ms, mx_a = carry
        r0 = pl.multiple_of(c * kb, kb)
        mx_b = qk(r0 + half, lg_ref)
        ms = pv(r0, lga_ref, ms, mx_a)
        r_next = pl.multiple_of(jnp.minimum(c + 1, nk - 1) * kb, kb)
        mx_a = qk(r_next, lga_ref)
        ms = pv(r0 + half, lg_ref, ms, mx_b)
        return ms, mx_a

    m0 = (jnp.full((1, LANES), NEG_INF, F32),) * ATT_HEADS
    lax.fori_loop(0, nk, att_body, (m0, qk(0, lga_ref)))
    ot = jnp.concatenate([o_refs[h][0:HEAD_DIM] / o_refs[h][HEAD_DIM:HEAD_DIM + 1]
                          for h in range(ATT_HEADS)], axis=0)
    gate = ag_ref[...]
    y_ref[...] = (ot.T * (gate * jax.nn.sigmoid(gate))).astype(BF16)


def _dsa_prompt(bv, tv, ikb, akb, avtb, iqb, iwt, aqb, ag, kb=512):
    assert ATT_HEADS % 2 == 0 and 2 * HEAD_DIM == LANES
    topk = min(TOPK_MAX, tv // 4)
    kb = min(kb, tv)
    nq = tv // Q_BLOCK
    tri = _tri()
    per_b = lambda w: pl.BlockSpec((tv, w), lambda b, i: (b, 0))
    qrow = lambda w: pl.BlockSpec((Q_BLOCK, w), lambda b, i: (b * nq + i, 0))
    return pl.pallas_call(
        functools.partial(_dsa_prompt_kernel, kb=kb, t_total=tv, topk=topk),
        grid=(bv, nq),
        in_specs=[per_b(IDX_DIM), per_b(D_ATT),
                  pl.BlockSpec((None, ATT_HEADS * V_ROWS, tv), lambda b, i: (b, 0, 0)),
                  qrow(IDX_HEADS * IDX_DIM),
                  pl.BlockSpec((None, SUBLANES, Q_BLOCK), lambda b, i: (b, 0, i)),
                  qrow(D_ATT), qrow(D_ATT),
                  pl.BlockSpec(tri.shape, lambda b, i: (0, 0))],
        out_specs=qrow(D_ATT),
        out_shape=jax.ShapeDtypeStruct((bv * tv, D_ATT), BF16),
        scratch_shapes=[pltpu.VMEM((tv, LANES), F32), pltpu.VMEM((tv, LANES), F32)]
        + [pltpu.VMEM((ATT_HEADS, kb // 2, LANES), F32)] * 2
        + [pltpu.VMEM((V_ROWS, LANES), F32)] * ATT_HEADS,
        compiler_params=_cparams(2),
        name="dsa_prompt",
    )(ikb, akb, avtb, iqb, iwt, aqb, ag, tri)


def _samp_score_kernel(pt_ref, *refs, npg, ts):
    del pt_ref
    pages = refs[:npg]
    qi_ref, w_ref, iknew_ref, sc_ref, scn_ref = refs[npg:]
    qi = qi_ref[...]
    w = w_ref[...]

    def score(keys_t):
        s = jnp.dot(qi, keys_t.astype(BF16), preferred_element_type=F32)
        r = jnp.maximum(s, 0.0) * w
        return (r[0:ts] + r[ts:2 * ts]) + (r[2 * ts:3 * ts] + r[3 * ts:4 * ts])

    for i in range(npg):
        sc_ref[:, i * PAGE_SIZE:(i + 1) * PAGE_SIZE] = score(pages[i][...])

    @pl.when(pl.program_id(1) == 0)
    def _new_keys():
        sn = score(iknew_ref[...])
        col = lax.broadcasted_iota(I32, (ts, PAGE_SIZE), 1)
        row = lax.broadcasted_iota(I32, (ts, PAGE_SIZE), 0)
        scn_ref[...] = jnp.where(col <= row, sn, NEG_INF)


def _samp_scores(page_table, kidx_t, layer, qi, w, iknew_t, npg):
    bs, n_pages = page_table.shape
    ts = qi.shape[1] // IDX_HEADS
    page_spec = lambda i: pl.BlockSpec(
        (None, None, IDX_DIM, PAGE_SIZE), lambda b, j, pt, i=i: (layer, pt[b, j * npg + i], 0, 0))
    per_b = lambda a: pl.BlockSpec((None,) + a.shape[1:], lambda b, j, pt: (b, 0, 0))
    gs = pltpu.PrefetchScalarGridSpec(
        num_scalar_prefetch=1,
        grid=(bs, n_pages // npg),
        in_specs=[page_spec(i) for i in range(npg)] + [per_b(qi), per_b(w), per_b(iknew_t)],
        out_specs=(pl.BlockSpec((ts, npg * PAGE_SIZE), lambda b, j, pt: (b, j)),
                   pl.BlockSpec((ts, PAGE_SIZE), lambda b, j, pt: (b, 0))),
    )
    return pl.pallas_call(
        functools.partial(_samp_score_kernel, npg=npg, ts=ts),
        grid_spec=gs,
        out_shape=(jax.ShapeDtypeStruct((bs * ts, n_pages * PAGE_SIZE), F32),
                   jax.ShapeDtypeStruct((bs * ts, PAGE_SIZE), F32)),
        compiler_params=_cparams(2),
        name="sample_scores",
    )(page_table, *([kidx_t] * npg), qi, w, iknew_t)


def _select_kernel(sc_ref, qpos_ref, tri_ref, bias_ref, *, kb, nk, topk):
    _select_bias(sc_ref, bias_ref, tri_ref, nk, kb, 0, qpos_ref[...], topk)


def _select(sc_t, qpos, topk, kb):
    nkeys, nq = sc_t.shape
    tri = _tri()
    col = pl.BlockSpec((nkeys, LANES), lambda i: (0, i))
    return pl.pallas_call(
        functools.partial(_select_kernel, kb=kb, nk=nkeys // kb, topk=topk),
        grid=(nq // LANES,),
        in_specs=[col, pl.BlockSpec((1, LANES), lambda i: (0, i)),
                  pl.BlockSpec(tri.shape, lambda i: (0, 0))],
        out_specs=col,
        out_shape=jax.ShapeDtypeStruct((nkeys, nq), F32),
        compiler_params=_cparams(1),
        name="sample_select",
    )(sc_t, qpos, tri)


def _samp_att_kernel(pt_ref, *refs, npg, ts):
    del pt_ref
    kp = refs[:npg]
    vp = refs[npg:2 * npg]
    q_ref, bias_ref, biasn_ref, kn_ref, vn_ref, ag_ref, y_ref, m_ref, l_ref, acc_ref = refs[2 * npg:]
    j = pl.program_id(1)
    nj = pl.num_programs(1)

    @pl.when(j == 0)
    def _init():
        m_ref[...] = jnp.full(m_ref.shape, NEG_INF, F32)
        l_ref[...] = jnp.zeros(l_ref.shape, F32)
        acc_ref[...] = jnp.zeros(acc_ref.shape, F32)

    q = q_ref[...]

    def update(kts, vts, bias):
        logit = jnp.concatenate(
            [jnp.dot(q, kt.astype(BF16), preferred_element_type=F32) for kt in kts], axis=1)
        logit = logit + jnp.concatenate([bias] * ATT_HEADS, axis=0)
        m_old = m_ref[...]
        m_new = jnp.maximum(m_old, jnp.max(logit, axis=1, keepdims=True))
        alpha = jnp.exp2(m_old - m_new)
        p = jnp.exp2(logit - m_new)
        l_ref[...] = alpha * l_ref[...] + jnp.sum(p, axis=1, keepdims=True)
        pb = p.astype(BF16)
        pvs = [lax.dot_general(pb[:, i * PAGE_SIZE:(i + 1) * PAGE_SIZE], vts[i].astype(BF16), _NT,
                               preferred_element_type=F32) for i in range(len(vts))]
        while len(pvs) > 1:
            pvs = [pvs[i] + pvs[i + 1] for i in range(0, len(pvs) - 1, 2)] + pvs[len(pvs) - len(pvs) % 2:]
        acc_ref[...] = alpha * acc_ref[...] + pvs[0]
        m_ref[...] = m_new

    update([r[...] for r in kp], [r[...] for r in vp], bias_ref[...])

    @pl.when(j == nj - 1)
    def _fin():
        update([kn_ref[...]], [vn_ref[...]], biasn_ref[...])
        o = acc_ref[...] / l_ref[...]
        gate = ag_ref[...]
        gsil = gate * jax.nn.sigmoid(gate)
        for h in range(ATT_HEADS):
            sl = slice(h * HEAD_DIM, (h + 1) * HEAD_DIM)
            y_ref[:, sl] = (o[h * ts:(h + 1) * ts, sl] * gsil[:, sl]).astype(BF16)


def _samp_attention(page_table, k_t, v_t, layer, qbd, bias, biasn, knew_t, vnew_t, ag, npg):
    bs, n_pages = page_table.shape
    ts = qbd.shape[1] // ATT_HEADS
    page_spec = lambda i: pl.BlockSpec(
        (None, None, D_ATT, PAGE_SIZE), lambda b, j, pt, i=i: (layer, pt[b, j * npg + i], 0, 0))
    per_b = lambda a: pl.BlockSpec((None,) + a.shape[1:], lambda b, j, pt: (b, 0, 0))
    gs = pltpu.PrefetchScalarGridSpec(
        num_scalar_prefetch=1,
        grid=(bs, n_pages // npg),
        in_specs=[page_spec(i) for i in range(npg)] * 2
        + [per_b(qbd),
           pl.BlockSpec((ts, npg * PAGE_SIZE), lambda b, j, pt: (b, j)),
           pl.BlockSpec((ts, PAGE_SIZE), lambda b, j, pt: (b, 0)),
           per_b(knew_t), per_b(vnew_t),
           pl.BlockSpec((ts, D_ATT), lambda b, j, pt: (b, 0))],
        out_specs=pl.BlockSpec((ts, D_ATT), lambda b, j, pt: (b, 0)),
        scratch_shapes=[pltpu.VMEM((ATT_HEADS * ts, 1), F32), pltpu.VMEM((ATT_HEADS * ts, 1), F32),
                        pltpu.VMEM((ATT_HEADS * ts, D_ATT), F32)],
    )
    return pl.pallas_call(
        functools.partial(_samp_att_kernel, npg=npg, ts=ts),
        grid_spec=gs,
        out_shape=jax.ShapeDtypeStruct((bs * ts, D_ATT), BF16),
        compiler_params=_cparams(2),
        name="sample_attention",
    )(page_table, *([k_t] * npg), *([v_t] * npg), qbd, bias, biasn, knew_t, vnew_t, ag)


def _dsa_sample(page_table, k_t, v_t, kidx_t, layer, ts, akt, avt, ikt, ag, aqb, iqb, iwt):
    bs, n_pages = page_table.shape
    past = n_pages * PAGE_SIZE
    n = bs * ts
    topk = min(TOPK_MAX, (past + ts) // 4)
    npg = math.gcd(n_pages, 32)

    def new_keys_t(a):
        a = a.reshape(a.shape[0], bs, ts).transpose(1, 0, 2)
        return jnp.pad(a, ((0, 0), (0, 0), (0, PAGE_SIZE - ts)))

    qi = iqb.reshape(bs, ts, IDX_HEADS, IDX_DIM).transpose(0, 2, 1, 3).reshape(bs, IDX_HEADS * ts, IDX_DIM)
    w = iwt[:IDX_HEADS].reshape(IDX_HEADS, bs, ts).transpose(1, 0, 2).reshape(bs, IDX_HEADS * ts, 1)
    w = jnp.broadcast_to(w, (bs, IDX_HEADS * ts, LANES))
    sc, scn = _samp_scores(page_table, kidx_t, layer, qi, w, new_keys_t(ikt), npg)
    sc_t = jnp.concatenate([sc, scn], axis=1).T
    qpos = (past + jnp.arange(n, dtype=I32) % ts).reshape(1, n)
    bias_all = _select(sc_t, qpos, topk, PAGE_SIZE).T
    eye = jnp.eye(ATT_HEADS, dtype=BF16)
    q4 = aqb.reshape(bs, ts, ATT_HEADS, HEAD_DIM)
    qbd = (eye[None, :, None, :, None] * q4.transpose(0, 2, 1, 3)[:, :, :, None, :]).reshape(
        bs, ATT_HEADS * ts, D_ATT)
    return _samp_attention(page_table, k_t, v_t, layer, qbd, bias_all[:, :past], bias_all[:, past:],
                           new_keys_t(akt), new_keys_t(avt), ag, npg)


def _mixer_layer(x, pos, h0r, h0i, s0, attend, g, w_in_l, w_out_l, ssm_prm, ret_g, gf, final, tm, tl):
    bv, tv, _ = x.shape
    n = bv * tv
    wm = w_in_l[:, :C_IW].astype(BF16)
    wt = jnp.concatenate([w_in_l[:, C_AK:C_AG].T, w_in_l[:, C_IK:C_END].T,
                          jnp.zeros((SUBLANES - IDX_HEADS, D_MODEL), w_in_l.dtype)], axis=0).astype(BF16)
    xf = x.reshape(n, D_MODEL)
    pb, pt = (bv, tv) if tv % tm == 0 else (1, n)
    tm = min(tm, pt)
    (u2, r4, ag, akt, avt, ikt, aqb, akb, avtb, iqb, ikb, iwt) = _inproj(
        xf, pb, pt, g.astype(F32).reshape(1, D_MODEL), wm, wt, tm)
    y_ssm, hr, hi = _ssm(u2.reshape(bv, tv, -1), h0r, h0i, ssm_prm, tl)
    y_ret, s_new = _ret(r4.reshape(bv, tv, -1), s0, pos, ret_g)
    y_att = attend(ag, akt, avt, ikt, aqb, akb, avtb, iqb, ikb, iwt)
    y = _outproj(xf, y_ssm.reshape(n, D_SSM), y_ret.reshape(n, D_RET), y_att,
                 w_out_l.astype(BF16), gf.astype(F32).reshape(1, D_MODEL), tm, final)
    tok = lambda a: a.reshape(pb, a.shape[1], -1, tv).transpose(0, 2, 3, 1).reshape(bv, tv, a.shape[1])
    return y.reshape(bv, tv, D_MODEL), (tok(akt), tok(avt), tok(ikt), hr, hi, s_new)


def kernel(x_prompt, x_sample, cache_k, cache_v, cache_kidx, state_ssm_re, state_ssm_im, state_ret,
           page_table, norm_g, w_in, w_out, ssm_lambda_re, ssm_lambda_im, ssm_b_re, ssm_b_im,
           ssm_c_re, ssm_c_im, ssm_d, ssm_log_step, ssm_w_glu, ssm_b_glu, ret_norm_g, final_norm_g):
    bp, tp, _ = x_prompt.shape
    bs, ts, _ = x_sample.shape
    depth = w_in.shape[0]
    past = page_table.shape[1] * PAGE_SIZE
    assert ts == SUBLANES and tp % Q_BLOCK == 0
    pos_p = jnp.arange(tp)
    pos_s = past + jnp.arange(ts)
    k_t = jnp.transpose(cache_k, (0, 1, 3, 4, 2)).reshape(cache_k.shape[:2] + (D_ATT, PAGE_SIZE))
    v_t = jnp.transpose(cache_v, (0, 1, 3, 4, 2)).reshape(cache_v.shape[:2] + (D_ATT, PAGE_SIZE))
    kidx_t = jnp.transpose(cache_kidx, (0, 1, 3, 2))
    zero_h = jnp.zeros((bp, 1, N_STATE), F32)
    zero_s = jnp.zeros((bp, RET_HEADS, HEAD_DIM, HEAD_DIM), F32)
    hp, hs = x_prompt, x_sample
    outs_p, outs_s = [], []
    for l in range(depth):
        final = l == depth - 1
        prm = _ssm_params(ssm_lambda_re[l], ssm_lambda_im[l], ssm_b_re[l], ssm_b_im[l], ssm_c_re[l],
                          ssm_c_im[l], ssm_d[l], ssm_log_step[l], ssm_w_glu[l], ssm_b_glu[l])

        def attend_p(ag, akt, avt, ikt, aqb, akb, avtb, iqb, ikb, iwt):
            return _dsa_prompt(bp, tp, ikb, akb, avtb, iqb, iwt, aqb, ag)

        hp, st_p = _mixer_layer(hp, pos_p, zero_h, zero_h, zero_s, attend_p, norm_g[l], w_in[l], w_out[l],
                                prm, ret_norm_g[l], final_norm_g, final, tm=512, tl=min(512, tp))

        def attend_s(ag, akt, avt, ikt, aqb, akb, avtb, iqb, ikb, iwt):
            return _dsa_sample(page_table, k_t, v_t, kidx_t, l, ts, akt[0], avt[0], ikt[0], ag,
                               aqb, iqb, iwt[0])

        hs, st_s = _mixer_layer(hs, pos_s, state_ssm_re[l].reshape(bs, 1, N_STATE),
                                state_ssm_im[l].reshape(bs, 1, N_STATE), state_ret[l], attend_s,
                                norm_g[l], w_in[l], w_out[l], prm, ret_norm_g[l], final_norm_g, final,
                                tm=256, tl=ts)
        outs_p.append(st_p)
        outs_s.append(st_s)

    def pack(outs, b, t):
        k = jnp.stack([o[0] for o in outs]).reshape(depth, b, t, ATT_HEADS, HEAD_DIM)
        v = jnp.stack([o[1] for o in outs]).reshape(depth, b, t, ATT_HEADS, HEAD_DIM)
        ki = jnp.stack([o[2] for o in outs])
        hr = jnp.stack([o[3] for o in outs]).reshape(depth, b, SSM_GROUPS, SSM_STATE)
        hi = jnp.stack([o[4] for o in outs]).reshape(depth, b, SSM_GROUPS, SSM_STATE)
        s = jnp.stack([o[5] for o in outs])
        return k, v, ki, hr, hi, s

    return (hp, hs) + pack(outs_p, bp, tp) + pack(outs_s, bs, ts)
```

```python
import functools
import math

import jax
import jax.numpy as jnp
import numpy as np
from jax import lax
from jax.experimental import pallas as pl
from jax.experimental.pallas import tpu as pltpu

F32 = jnp.float32
BF16 = jnp.bfloat16
I32 = jnp.int32

D_MODEL = 1024
HEAD_DIM = 64
SSM_GROUP = 16
D_SSM = 256
SSM_GROUPS = D_SSM // SSM_GROUP
SSM_STATE = 64
N_STATE = SSM_GROUPS * SSM_STATE
D_RET = 384
RET_HEADS = D_RET // HEAD_DIM
RET_CHUNK = 128
ROPE_BASE = 10000.0
D_ATT = 384
ATT_HEADS = D_ATT // HEAD_DIM
IDX_HEADS = 4
IDX_DIM = 64
TOPK_MAX = 256
Q_BLOCK = 128
PAGE_SIZE = 128
NORM_EPS = 1e-6
NEG_INF = -1e30

_SIZES = (D_SSM, D_SSM, D_RET, D_RET, D_RET, D_RET, D_ATT, D_ATT, D_ATT, D_ATT,
          IDX_HEADS * IDX_DIM, IDX_DIM, IDX_HEADS)
_OFF = np.concatenate([[0], np.cumsum(_SIZES)]).tolist()
C_U, C_RET, C_AQ, C_AK, C_AV, C_AG, C_IQ, C_IK, C_IW, C_END = (
    _OFF[0], _OFF[2], _OFF[6], _OFF[7], _OFF[8], _OFF[9], _OFF[10], _OFF[11], _OFF[12], _OFF[13])

LANES = 128
SUBLANES = 8
F32_INF_BITS = 0x7F800000
LOG2_E = 1.4426950408889634
V_ROWS = HEAD_DIM + 16
VMEM_LIMIT = 56 * 1024 * 1024

_NT = (((1,), (1,)), ((), ()))


def _cparams(n_axes):
    return pltpu.CompilerParams(dimension_semantics=("arbitrary",) * n_axes,
                                vmem_limit_bytes=VMEM_LIMIT)


_T_AK, _T_AV, _T_IK, _T_IW, _T_END = 0, D_ATT, 2 * D_ATT, 2 * D_ATT + IDX_DIM, 2 * D_ATT + IDX_DIM + SUBLANES


def _inproj_kernel(x_ref, g_ref, wm_ref, wt_ref,
                   u2_ref, r4_ref, ag_ref, akt_ref, avt_ref, ikt_ref,
                   aqb_ref, akb_ref, avtb_ref, iqb_ref, ikb_ref, iwt_ref):
    x = x_ref[...]
    ms = jnp.mean(x * x, axis=-1, keepdims=True)
    xn = (x * lax.rsqrt(ms + NORM_EPS) * g_ref[...]).astype(BF16)

    def seg(a, b):
        return jnp.dot(xn, wm_ref[:, a:b], preferred_element_type=F32)

    u2_ref[...] = seg(C_U, C_RET)
    r4_ref[...] = seg(C_RET, C_AQ)
    aqb_ref[...] = (seg(C_AQ, C_AK) * (LOG2_E * HEAD_DIM ** -0.5)).astype(BF16)
    akb_ref[...] = seg(C_AK, C_AV).astype(BF16)
    ag_ref[...] = seg(C_AG, C_IQ)
    iqb_ref[...] = seg(C_IQ, C_IK).astype(BF16)
    ikb_ref[...] = seg(C_IK, C_IW).astype(BF16)
    zt = lax.dot_general(wt_ref[...], xn, _NT, preferred_element_type=F32)
    akt_ref[...] = zt[_T_AK:_T_AV]
    avt = zt[_T_AV:_T_IK]
    avt_ref[...] = avt
    ones = jnp.ones((V_ROWS - HEAD_DIM, avt.shape[1]), BF16)
    for h in range(ATT_HEADS):
        avtb_ref[h * V_ROWS:h * V_ROWS + HEAD_DIM, :] = avt[h * HEAD_DIM:(h + 1) * HEAD_DIM].astype(BF16)
        avtb_ref[h * V_ROWS + HEAD_DIM:(h + 1) * V_ROWS, :] = ones
    ikt_ref[...] = zt[_T_IK:_T_IW]
    iwt_ref[...] = zt[_T_IW:_T_END] * (IDX_HEADS ** -0.5 * IDX_DIM ** -0.5)


def _inproj(x, bv, tv, g, wm, wt, tm):
    n = x.shape[0]
    nt = tv // tm
    row = lambda w: pl.BlockSpec((tm, w), lambda i: (i, 0))
    col = lambda h: pl.BlockSpec((None, h, tm), lambda i: (i // nt, 0, i % nt))
    full = lambda a: pl.BlockSpec(a.shape, lambda i: (0,) * a.ndim)
    out_shape = (
        jax.ShapeDtypeStruct((n, C_RET - C_U), F32),
        jax.ShapeDtypeStruct((n, C_AQ - C_RET), F32),
        jax.ShapeDtypeStruct((n, D_ATT), F32),
        jax.ShapeDtypeStruct((bv, D_ATT, tv), F32),
        jax.ShapeDtypeStruct((bv, D_ATT, tv), F32),
        jax.ShapeDtypeStruct((bv, IDX_DIM, tv), F32),
        jax.ShapeDtypeStruct((n, D_ATT), BF16),
        jax.ShapeDtypeStruct((n, D_ATT), BF16),
        jax.ShapeDtypeStruct((bv, ATT_HEADS * V_ROWS, tv), BF16),
        jax.ShapeDtypeStruct((n, IDX_HEADS * IDX_DIM), BF16),
        jax.ShapeDtypeStruct((n, IDX_DIM), BF16),
        jax.ShapeDtypeStruct((bv, SUBLANES, tv), F32),
    )
    out_specs = (
        row(C_RET - C_U), row(C_AQ - C_RET), row(D_ATT), col(D_ATT), col(D_ATT), col(IDX_DIM),
        row(D_ATT), row(D_ATT), col(ATT_HEADS * V_ROWS), row(IDX_HEADS * IDX_DIM),
        row(IDX_DIM), col(SUBLANES))
    return pl.pallas_call(
        _inproj_kernel,
        grid=(n // tm,),
        in_specs=[row(D_MODEL), full(g), full(wm), full(wt)],
        out_specs=out_specs,
        out_shape=out_shape,
        compiler_params=_cparams(1),
        name="inproj",
    )(x, g, wm, wt)


def _outproj_kernel(x_ref, ys_ref, yr_ref, ya_ref, wo_ref, gf_ref, o_ref, *, final):
    y = x_ref[...]
    y = y + jnp.dot(ys_ref[...], wo_ref[0:D_SSM, :], preferred_element_type=F32)
    y = y + jnp.dot(yr_ref[...], wo_ref[D_SSM:D_SSM + D_RET, :], preferred_element_type=F32)
    y = y + jnp.dot(ya_ref[...], wo_ref[D_SSM + D_RET:, :], preferred_element_type=F32)
    if final:
        ms = jnp.mean(y * y, axis=-1, keepdims=True)
        y = y * lax.rsqrt(ms + NORM_EPS) * gf_ref[...]
    o_ref[...] = y


def _outproj(x, ys, yr, ya, wo, gf, tm, final):
    n = x.shape[0]
    row = lambda w: pl.BlockSpec((tm, w), lambda i: (i, 0))
    full = lambda a: pl.BlockSpec(a.shape, lambda i: (0,) * a.ndim)
    return pl.pallas_call(
        functools.partial(_outproj_kernel, final=final),
        grid=(n // tm,),
        in_specs=[row(D_MODEL), row(D_SSM), row(D_RET), row(D_ATT), full(wo), full(gf)],
        out_specs=row(D_MODEL),
        out_shape=jax.ShapeDtypeStruct((n, D_MODEL), F32),
        compiler_params=_cparams(1),
        name="outproj",
    )(x, ys, yr, ya, wo, gf)


def _cmul(ar, ai, br, bi):
    return ar * br - ai * bi, ar * bi + ai * br


def _ssm_kernel(u2_ref, h0r_ref, h0i_ref, lr_ref, li_ref, ls_ref, btr_ref, bti_ref,
                ctr_ref, cti_ref, d_ref, wg_ref, bg_ref,
                y_ref, hr_out_ref, hi_out_ref,
                bbr_ref, bbi_ref, apr_ref, api_ref, cr_ref, ci_ref, hr_ref, hi_ref, *, tl):
    b = pl.program_id(0)
    c = pl.program_id(1)
    nc = pl.num_programs(1)

    @pl.when((b == 0) & (c == 0))
    def _prep():
        lr, li = lr_ref[...], li_ref[...]
        dt = jnp.exp(ls_ref[...])
        mag = jnp.exp(lr * dt)
        abr, abi = mag * jnp.cos(li * dt), mag * jnp.sin(li * dt)
        den = lr * lr + li * li
        nr, ni = abr - 1.0, abi
        fr = (nr * lr + ni * li) / den
        fi = (ni * lr - nr * li) / den
        bbr_ref[...] = (fr * btr_ref[...] - fi * bti_ref[...]).astype(BF16)
        bbi_ref[...] = (fr * bti_ref[...] + fi * btr_ref[...]).astype(BF16)
        pr, pi = abr, abi
        rows_r, rows_i = [pr], [pi]
        for _ in range(SUBLANES - 1):
            pr, pi = _cmul(pr, pi, abr, abi)
            rows_r.append(pr)
            rows_i.append(pi)
        apr_ref[...] = jnp.concatenate(rows_r, axis=0)
        api_ref[...] = jnp.concatenate(rows_i, axis=0)

    @pl.when(c == 0)
    def _init():
        cr_ref[...] = h0r_ref[...]
        ci_ref[...] = h0i_ref[...]

    u = u2_ref[:, 0:D_SSM]
    gate = u2_ref[:, D_SSM:2 * D_SSM]
    ub = u.astype(BF16)
    hr_ref[...] = jnp.dot(ub, bbr_ref[...], preferred_element_type=F32)
    hi_ref[...] = jnp.dot(ub, bbi_ref[...], preferred_element_type=F32)

    apr, api = apr_ref[...], api_ref[...]
    rowid = lax.broadcasted_iota(I32, (SUBLANES, N_STATE), 0)

    def blk(j, carry):
        cr, ci = carry
        r0 = pl.multiple_of(j * SUBLANES, SUBLANES)
        br = hr_ref[pl.ds(r0, SUBLANES), :]
        bi = hi_ref[pl.ds(r0, SUBLANES), :]
        for k in (1, 2, 4):
            keep = rowid >= k
            sr = jnp.where(keep, pltpu.roll(br, k, 0), 0.0)
            si = jnp.where(keep, pltpu.roll(bi, k, 0), 0.0)
            akr, aki = apr[k - 1:k, :], api[k - 1:k, :]
            br, bi = br + (akr * sr - aki * si), bi + (akr * si + aki * sr)
        hr = br + (apr * cr - api * ci)
        hi = bi + (apr * ci + api * cr)
        hr_ref[pl.ds(r0, SUBLANES), :] = hr
        hi_ref[pl.ds(r0, SUBLANES), :] = hi
        return hr[SUBLANES - 1:SUBLANES, :], hi[SUBLANES - 1:SUBLANES, :]

    cr, ci = lax.fori_loop(0, tl // SUBLANES, blk, (cr_ref[...], ci_ref[...]))
    cr_ref[...] = cr
    ci_ref[...] = ci

    @pl.when(c == nc - 1)
    def _fin():
        hr_out_ref[...] = cr
        hi_out_ref[...] = ci

    y = (jnp.dot(hr_ref[...].astype(BF16), ctr_ref[...], preferred_element_type=F32)
         - jnp.dot(hi_ref[...].astype(BF16), cti_ref[...], preferred_element_type=F32)
         + d_ref[...] * u)
    g = jax.nn.gelu(y)
    y = g * jax.nn.sigmoid(jnp.dot(g.astype(BF16), wg_ref[...], preferred_element_type=F32) + bg_ref[...])
    y_ref[...] = (y * (gate * jax.nn.sigmoid(gate))).astype(BF16)


def _ssm(u2, h0r, h0i, prm, tl):
    bv, tv, _ = u2.shape
    full = lambda a: pl.BlockSpec(a.shape, lambda b, c: (0,) * a.ndim)
    st = pl.BlockSpec((None, 1, N_STATE), lambda b, c: (b, 0, 0))
    return pl.pallas_call(
        functools.partial(_ssm_kernel, tl=tl),
        grid=(bv, tv // tl),
        in_specs=[pl.BlockSpec((None, tl, 2 * D_SSM), lambda b, c: (b, c, 0)), st, st]
        + [full(a) for a in prm],
        out_specs=(pl.BlockSpec((None, tl, D_SSM), lambda b, c: (b, c, 0)), st, st),
        out_shape=(jax.ShapeDtypeStruct((bv, tv, D_SSM), BF16),
                   jax.ShapeDtypeStruct((bv, 1, N_STATE), F32),
                   jax.ShapeDtypeStruct((bv, 1, N_STATE), F32)),
        scratch_shapes=[pltpu.VMEM((D_SSM, N_STATE), BF16), pltpu.VMEM((D_SSM, N_STATE), BF16),
                        pltpu.VMEM((SUBLANES, N_STATE), F32), pltpu.VMEM((SUBLANES, N_STATE), F32),
                        pltpu.VMEM((1, N_STATE), F32), pltpu.VMEM((1, N_STATE), F32),
                        pltpu.VMEM((tl, N_STATE), F32), pltpu.VMEM((tl, N_STATE), F32)],
        compiler_params=_cparams(2),
        name="ssm",
    )(u2, h0r, h0i, *prm)


def _ssm_params(lam_re, lam_im, b_re, b_im, c_re, c_im, d_skip, log_step, w_glu, b_glu):
    eye = jnp.eye(SSM_GROUPS, dtype=F32)

    def bdiag(m):
        g, r, c = m.shape
        return (eye[:, None, :, None] * m[:, :, None, :]).reshape(g * r, g * c)

    flat = lambda a: a.astype(F32).reshape(1, N_STATE)
    ls = jnp.broadcast_to(log_step.astype(F32)[:, None], (SSM_GROUPS, SSM_STATE))
    btr = bdiag(jnp.swapaxes(b_re.astype(F32), 1, 2))
    bti = bdiag(jnp.swapaxes(b_im.astype(F32), 1, 2))
    ctr = bdiag(jnp.swapaxes(c_re.astype(F32), 1, 2)).astype(BF16)
    cti = bdiag(jnp.swapaxes(c_im.astype(F32), 1, 2)).astype(BF16)
    return (flat(lam_re), flat(lam_im), flat(ls), btr, bti, ctr, cti,
            d_skip.astype(F32).reshape(1, D_SSM), w_glu.astype(BF16),
            b_glu.astype(F32).reshape(1, D_SSM))


def _swap_halves(x):
    lane = lax.broadcasted_iota(I32, (x.shape[0], LANES), 1)
    first = (lane % HEAD_DIM) < (HEAD_DIM // 2)
    tiles = []
    for t in range(x.shape[1] // LANES):
        xt = x[:, t * LANES:(t + 1) * LANES]
        up = pltpu.roll(xt, LANES - HEAD_DIM // 2, 1)
        dn = pltpu.roll(xt, HEAD_DIM // 2, 1)
        tiles.append(jnp.where(first, up, dn))
    return jnp.concatenate(tiles, axis=1)


def _ret_kernel(r4_ref, s0_ref, cos_ref, sin_ref, din_ref, dq_ref, dk_ref, ds_ref, g_ref,
                y_ref, s_out_ref, s_ref):
    c = pl.program_id(1)
    nc = pl.num_programs(1)

    @pl.when(c == 0)
    def _init():
        s_ref[...] = s0_ref[...]

    q = r4_ref[:, 0:D_RET]
    k = r4_ref[:, D_RET:2 * D_RET]
    v = r4_ref[:, 2 * D_RET:3 * D_RET]
    gate = r4_ref[:, 3 * D_RET:4 * D_RET]
    cos = jnp.concatenate([cos_ref[...]] * (D_RET // LANES), axis=1)
    sin = jnp.concatenate([sin_ref[...]] * (D_RET // LANES), axis=1)
    qr = q * cos + _swap_halves(q) * sin
    kr = (k * cos + _swap_halves(k) * sin) * HEAD_DIM ** -0.5
    qb = qr.astype(BF16)
    kb = kr.astype(BF16)
    vb = v.astype(BF16)
    kdt = (kr * dk_ref[...]).T.astype(BF16)
    dq = dq_ref[...]
    gsil = gate * jax.nn.sigmoid(gate) * g_ref[...]
    sls = [slice(h * HEAD_DIM, (h + 1) * HEAD_DIM) for h in range(RET_HEADS)]
    atts, crosses = [], []
    for h, sl in enumerate(sls):
        qh, kh, vh = qb[:, sl], kb[:, sl], vb[:, sl]
        s = s_ref[h]
        atts.append((lax.dot_general(qh, kh, _NT, preferred_element_type=F32) * din_ref[h]).astype(BF16))
        crosses.append(jnp.dot(qh, s.astype(BF16), preferred_element_type=F32) * dq[:, sl])
        s_ref[h] = s * ds_ref[h] + jnp.dot(kdt[sl, :], vh, preferred_element_type=F32)
    for h, sl in enumerate(sls):
        o = jnp.dot(atts[h], vb[:, sl], preferred_element_type=F32) + crosses[h]
        mu = jnp.mean(o, axis=-1, keepdims=True)
        var = jnp.mean(jnp.square(o - mu), axis=-1, keepdims=True)
        y_ref[:, sl] = ((o - mu) * lax.rsqrt(var + NORM_EPS) * gsil[:, sl]).astype(BF16)

    @pl.when(c == nc - 1)
    def _fin():
        s_out_ref[...] = s_ref[...]


def _ret_tables(pos, chunk):
    half = HEAD_DIM // 2
    inv = ROPE_BASE ** (-jnp.arange(half, dtype=F32) / half)
    ang = pos.astype(F32)[:, None] * inv[None, :]
    cos, sin = jnp.cos(ang), jnp.sin(ang)
    cos_t = jnp.concatenate([cos, cos] * (LANES // HEAD_DIM), axis=1)
    sin_t = jnp.concatenate([-sin, sin] * (LANES // HEAD_DIM), axis=1)
    log_g = jnp.log1p(-jnp.exp2(-5.0 - jnp.arange(RET_HEADS, dtype=F32)))
    j = jnp.arange(chunk, dtype=F32)
    rel = j[:, None] - j[None, :]
    din = jnp.where(rel[None] >= 0, jnp.exp(log_g[:, None, None] * jnp.maximum(rel, 0.0)[None]), 0.0)
    dq = jnp.repeat(jnp.exp(log_g[None, :] * (j[:, None] + 1.0)), HEAD_DIM, axis=1)
    dk = jnp.repeat(jnp.exp(log_g[None, :] * (chunk - 1.0 - j[:, None])), HEAD_DIM, axis=1)
    ds = jnp.broadcast_to(jnp.exp(log_g * chunk)[:, None, None], (RET_HEADS, HEAD_DIM, HEAD_DIM))
    return cos_t, sin_t, din, dq, dk, ds


def _ret(r4, s0, pos, norm_g):
    bv, tv, _ = r4.shape
    chunk = RET_CHUNK if tv % RET_CHUNK == 0 else tv
    cos_t, sin_t, din, dq, dk, ds = _ret_tables(pos, chunk)
    g = norm_g.astype(F32).reshape(1, D_RET)
    full = lambda a: pl.BlockSpec(a.shape, lambda b, c: (0,) * a.ndim)
    st = pl.BlockSpec((None, RET_HEADS, HEAD_DIM, HEAD_DIM), lambda b, c: (b, 0, 0, 0))
    tab = pl.BlockSpec((chunk, LANES), lambda b, c: (c, 0))
    return pl.pallas_call(
        _ret_kernel,
        grid=(bv, tv // chunk),
        in_specs=[pl.BlockSpec((None, chunk, 4 * D_RET), lambda b, c: (b, c, 0)), st, tab, tab,
                  full(din), full(dq), full(dk), full(ds), full(g)],
        out_specs=(pl.BlockSpec((None, chunk, D_RET), lambda b, c: (b, c, 0)), st),
        out_shape=(jax.ShapeDtypeStruct((bv, tv, D_RET), BF16),
                   jax.ShapeDtypeStruct((bv, RET_HEADS, HEAD_DIM, HEAD_DIM), F32)),
        scratch_shapes=[pltpu.VMEM((RET_HEADS, HEAD_DIM, HEAD_DIM), F32)],
        compiler_params=_cparams(2),
        name="retention",
    )(r4, s0, cos_t, sin_t, din, dq, dk, ds, g)


def _count(sc_ref, nk, kb, pred):
    n_acc = 4

    def body(c, cnts):
        r0 = pl.multiple_of(c * kb, kb)
        cnts = list(cnts)
        for j in range(kb // SUBLANES):
            blk = sc_ref[pl.ds(r0 + j * SUBLANES, SUBLANES), :]
            cnts[j % n_acc] = cnts[j % n_acc] + jnp.where(pred(blk), 1, 0)
        return tuple(cnts)
    zero = jnp.zeros((SUBLANES, LANES), I32)
    cnts = lax.fori_loop(0, nk, body, (zero,) * n_acc)
    return jnp.sum((cnts[0] + cnts[1]) + (cnts[2] + cnts[3]), axis=0, keepdims=True)


_ORD_FLIP = 0x7FFFFFFF
_ORD_POS_INF = F32_INF_BITS
_ORD_NEG_INF = -F32_INF_BITS - 1


def _unord(o):
    return lax.bitcast_convert_type(jnp.where(o >= 0, o, o ^ _ORD_FLIP), F32)


def _select_bias(sc_ref, bias_ref, tri_ref, nk, kb, n_beyond, qpos, topk, bufs=None):
    def count_ge(t):
        return _count(sc_ref, nk, kb, lambda blk: blk >= t) + jnp.where(t <= NEG_INF, n_beyond, 0)

    c_zero = count_ge(jnp.zeros((1, LANES), F32))
    pos = jnp.where(c_zero >= topk, 1, 0)
    lo = jnp.where(pos == 1, 0, _ORD_NEG_INF)
    hi = jnp.where(pos == 1, _ORD_POS_INF + 1, 0)
    c_lo = jnp.where(pos == 1, c_zero, nk * kb + n_beyond)
    n_steps = 31

    def step(_, carry):
        lo, hi, c_lo = carry
        mid = lo + lax.shift_right_arithmetic(hi - lo, 1)
        c = count_ge(_unord(mid))
        ok = c >= topk
        return jnp.where(ok, mid, lo), jnp.where(ok, hi, mid), jnp.where(ok, c, c_lo)

    lo, _, c_lo = lax.fori_loop(0, n_steps, step, (lo, hi, c_lo))
    tau = _unord(lo)
    surplus = c_lo - topk
    surplus_ties = jnp.max(surplus) > 0

    @pl.when(jnp.logical_not(surplus_ties))
    def _no_surplus():
        def body(c, _):
            r0 = pl.multiple_of(c * kb, kb)
            blk = sc_ref[pl.ds(r0, kb), :]
            kidx = r0 + lax.broadcasted_iota(I32, (kb, LANES), 0)
            bias_ref[pl.ds(r0, kb), :] = jnp.where((blk >= tau) & (kidx <= qpos), 0.0, NEG_INF)
            return 0
        lax.fori_loop(0, nk, body, 0)

    @pl.when(surplus_ties)
    def _ties_by_index():
        surplus_f = surplus.astype(F32)
        above = jnp.where(tau == NEG_INF, n_beyond, 0).astype(F32)

        def counts(r0, n_rows, buf, above):
            offs = []
            for b in reversed(range(n_rows // TIE_BLOCK)):
                rb = r0 + b * TIE_BLOCK
                eq01 = jnp.where(sc_ref[pl.ds(rb, TIE_BLOCK), :] == tau, 1.0, 0.0)
                buf[b * TIE_BLOCK:(b + 1) * TIE_BLOCK, :] = jnp.dot(tri_ref[...], eq01.astype(BF16),
                                                                    preferred_element_type=F32)
                offs.append(above)
                above = above + jnp.sum(eq01, axis=0, keepdims=True)
            return tuple(offs), above

        def select(r0, n_rows, buf, offs):
            for b, off in zip(reversed(range(n_rows // TIE_BLOCK)), offs):
                rb = r0 + b * TIE_BLOCK
                blk = sc_ref[pl.ds(rb, TIE_BLOCK), :]
                at_or_above = buf[b * TIE_BLOCK:(b + 1) * TIE_BLOCK, :] + off
                kidx = rb + lax.broadcasted_iota(I32, (TIE_BLOCK, LANES), 0)
                sel = ((blk > tau) | ((blk == tau) & (at_or_above > surplus_f))) & (kidx <= qpos)
                bias_ref[pl.ds(rb, TIE_BLOCK), :] = jnp.where(sel, 0.0, NEG_INF)

        if bufs is None:
            def body(c, above):
                r0 = pl.multiple_of((nk - 1 - c) * kb, kb)
                offs, above = counts(r0, kb, bias_ref.at[pl.ds(r0, kb)], above)
                select(r0, kb, bias_ref.at[pl.ds(r0, kb)], offs)
                return above

            lax.fori_loop(0, nk, body, above)
        else:
            half = kb // 2
            buf_hi, buf_lo = bufs

            def body(c, carry):
                above, offs_hi = carry
                cc = nk - 1 - c
                r0 = pl.multiple_of(cc * kb, kb)
                offs_lo, above = counts(r0, half, buf_lo, above)
                select(r0 + half, half, buf_hi, offs_hi)
                r_next = pl.multiple_of(jnp.maximum(cc - 1, 0) * kb, kb)
                offs_hi, above = counts(r_next + half, half, buf_hi, above)
                select(r0, half, buf_lo, offs_lo)
                return above, offs_hi

            offs_hi, above = counts((nk - 1) * kb + half, half, buf_hi, above)
            lax.fori_loop(0, nk, body, (above, offs_hi))


TIE_BLOCK = 128


def _tri():
    r = np.arange(TIE_BLOCK)
    return jnp.asarray((r[None, :] >= r[:, None]).astype(np.float32), dtype=BF16)


def _dsa_prompt_kernel(ikb_ref, akb_ref, avtb_ref, iqb_ref, iwt_ref, aqb_ref, ag_ref, tri_ref,
                       y_ref, sc_ref, bias_ref, lga_ref, lg_ref, *o_refs, kb, t_total, topk):
    i = pl.program_id(1)
    q0 = i * Q_BLOCK
    nk = (q0 + Q_BLOCK + kb - 1) // kb
    qpos = q0 + lax.broadcasted_iota(I32, (1, LANES), 1)
    iq = iqb_ref[...]
    iw = iwt_ref[...]
    qcat = jnp.concatenate([iq[:, h * IDX_DIM:(h + 1) * IDX_DIM] for h in range(IDX_HEADS)], axis=0)

    def score_body(c, _):
        r0 = pl.multiple_of(c * kb, kb)
        keys = ikb_ref[pl.ds(r0, kb), :]
        s = lax.dot_general(keys, qcat, _NT, preferred_element_type=F32)
        parts = [jnp.maximum(s[:, h * LANES:(h + 1) * LANES], 0.0) * iw[h:h + 1, :]
                 for h in range(IDX_HEADS)]
        acc = (parts[0] + parts[1]) + (parts[2] + parts[3])
        kidx = r0 + lax.broadcasted_iota(I32, (kb, LANES), 0)
        sc_ref[pl.ds(r0, kb), :] = jnp.where(kidx <= qpos, acc, NEG_INF)
        return 0

    lax.fori_loop(0, nk, score_body, 0)
    _select_bias(sc_ref, bias_ref, tri_ref, nk, kb, t_total - nk * kb, qpos, topk,
                 bufs=(lga_ref.at[0], lg_ref.at[0]))

    aq = aqb_ref[...]
    lane = lax.broadcasted_iota(I32, (Q_BLOCK, LANES), 1)
    qpairs = []
    for j in range(ATT_HEADS // 2):
        qp = aq[:, j * LANES:(j + 1) * LANES]
        zero = jnp.zeros_like(qp)
        qpairs.append(jnp.concatenate([jnp.where(lane < HEAD_DIM, qp, zero),
                                       jnp.where(lane >= HEAD_DIM, qp, zero)], axis=0))
    for o_ref in o_refs:
        o_ref[...] = jnp.zeros(o_ref.shape, F32)

    half = kb // 2

    def qk(r0, lg):
        bias = bias_ref[pl.ds(r0, half), :]
        mx = []
        for j in range(ATT_HEADS // 2):
            kpair = akb_ref[pl.ds(r0, half), j * LANES:(j + 1) * LANES]
            both = lax.dot_general(kpair, qpairs[j], _NT, preferred_element_type=F32)
            for i in range(2):
                logit = both[:, i * LANES:(i + 1) * LANES] + bias
                lg[2 * j + i] = logit
                mx.append(jnp.max(logit, axis=0, keepdims=True))
        return tuple(mx)

    def pv(r0, lg, ms, mx):
        ms_new = []
        for h in range(ATT_HEADS):
            m_new = jnp.maximum(ms[h], mx[h])
            alpha = jnp.exp2(ms[h] - m_new)
            p = jnp.exp2(lg[h] - m_new)
            ms_new.append(m_new)
            vt = avtb_ref[h * V_ROWS:(h + 1) * V_ROWS, pl.ds(r0, half)]
            o_refs[h][...] = alpha * o_refs[h][...] + jnp.dot(vt, p.astype(BF16), preferred_element_type=F32)
        return tuple(ms_new)

    def att_body(c, carry):
        ms, mx_a = carry
        r0 = pl.multiple_of(c * kb, kb)
        mx_b = qk(r0 + half, lg_ref)
        ms = pv(r0, lga_ref, ms, mx_a)
        r_next = pl.multiple_of(jnp.minimum(c + 1, nk - 1) * kb, kb)
        mx_a = qk(r_next, lga_ref)
        ms = pv(r0 + half, lg_ref, ms, mx_b)
        return ms, mx_a

    m0 = (jnp.full((1, LANES), NEG_INF, F32),) * ATT_HEADS
    lax.fori_loop(0, nk, att_body, (m0, qk(0, lga_ref)))
    ot = jnp.concatenate([o_refs[h][0:HEAD_DIM] / o_refs[h][HEAD_DIM:HEAD_DIM + 1]
                          for h in range(ATT_HEADS)], axis=0)
    gate = ag_ref[...]
    y_ref[...] = (ot.T * (gate * jax.nn.sigmoid(gate))).astype(BF16)


def _dsa_prompt(bv, tv, ikb, akb, avtb, iqb, iwt, aqb, ag, kb=512):
    assert ATT_HEADS % 2 == 0 and 2 * HEAD_DIM == LANES
    topk = min(TOPK_MAX, tv // 4)
    kb = min(kb, tv)
    nq = tv // Q_BLOCK
    tri = _tri()
    per_b = lambda w: pl.BlockSpec((tv, w), lambda b, i: (b, 0))
    qrow = lambda w: pl.BlockSpec((Q_BLOCK, w), lambda b, i: (b * nq + i, 0))
    return pl.pallas_call(
        functools.partial(_dsa_prompt_kernel, kb=kb, t_total=tv, topk=topk),
        grid=(bv, nq),
        in_specs=[per_b(IDX_DIM), per_b(D_ATT),
                  pl.BlockSpec((None, ATT_HEADS * V_ROWS, tv), lambda b, i: (b, 0, 0)),
                  qrow(IDX_HEADS * IDX_DIM),
                  pl.BlockSpec((None, SUBLANES, Q_BLOCK), lambda b, i: (b, 0, i)),
                  qrow(D_ATT), qrow(D_ATT),
                  pl.BlockSpec(tri.shape, lambda b, i: (0, 0))],
        out_specs=qrow(D_ATT),
        out_shape=jax.ShapeDtypeStruct((bv * tv, D_ATT), BF16),
        scratch_shapes=[pltpu.VMEM((tv, LANES), F32), pltpu.VMEM((tv, LANES), F32)]
        + [pltpu.VMEM((ATT_HEADS, kb // 2, LANES), F32)] * 2
        + [pltpu.VMEM((V_ROWS, LANES), F32)] * ATT_HEADS,
        compiler_params=_cparams(2),
        name="dsa_prompt",
    )(ikb, akb, avtb, iqb, iwt, aqb, ag, tri)


def _samp_score_kernel(pt_ref, *refs, npg, ts):
    del pt_ref
    pages = refs[:npg]
    qi_ref, w_ref, iknew_ref, sc_ref, scn_ref = refs[npg:]
    qi = qi_ref[...]
    w = w_ref[...]

    def score(keys_t):
        s = jnp.dot(qi, keys_t.astype(BF16), preferred_element_type=F32)
        r = jnp.maximum(s, 0.0) * w
        return (r[0:ts] + r[ts:2 * ts]) + (r[2 * ts:3 * ts] + r[3 * ts:4 * ts])

    for i in range(npg):
        sc_ref[:, i * PAGE_SIZE:(i + 1) * PAGE_SIZE] = score(pages[i][...])

    @pl.when(pl.program_id(1) == 0)
    def _new_keys():
        sn = score(iknew_ref[...])
        col = lax.broadcasted_iota(I32, (ts, PAGE_SIZE), 1)
        row = lax.broadcasted_iota(I32, (ts, PAGE_SIZE), 0)
        scn_ref[...] = jnp.where(col <= row, sn, NEG_INF)


def _samp_scores(page_table, kidx_t, layer, qi, w, iknew_t, npg):
    bs, n_pages = page_table.shape
    ts = qi.shape[1] // IDX_HEADS
    page_spec = lambda i: pl.BlockSpec(
        (None, None, IDX_DIM, PAGE_SIZE), lambda b, j, pt, i=i: (layer, pt[b, j * npg + i], 0, 0))
    per_b = lambda a: pl.BlockSpec((None,) + a.shape[1:], lambda b, j, pt: (b, 0, 0))
    gs = pltpu.PrefetchScalarGridSpec(
        num_scalar_prefetch=1,
        grid=(bs, n_pages // npg),
        in_specs=[page_spec(i) for i in range(npg)] + [per_b(qi), per_b(w), per_b(iknew_t)],
        out_specs=(pl.BlockSpec((ts, npg * PAGE_SIZE), lambda b, j, pt: (b, j)),
                   pl.BlockSpec((ts, PAGE_SIZE), lambda b, j, pt: (b, 0))),
    )
    return pl.pallas_call(
        functools.partial(_samp_score_kernel, npg=npg, ts=ts),
        grid_spec=gs,
        out_shape=(jax.ShapeDtypeStruct((bs * ts, n_pages * PAGE_SIZE), F32),
                   jax.ShapeDtypeStruct((bs * ts, PAGE_SIZE), F32)),
        compiler_params=_cparams(2),
        name="sample_scores",
    )(page_table, *([kidx_t] * npg), qi, w, iknew_t)


def _select_kernel(sc_ref, qpos_ref, tri_ref, bias_ref, *, kb, nk, topk):
    _select_bias(sc_ref, bias_ref, tri_ref, nk, kb, 0, qpos_ref[...], topk)


def _select(sc_t, qpos, topk, kb):
    nkeys, nq = sc_t.shape
    tri = _tri()
    col = pl.BlockSpec((nkeys, LANES), lambda i: (0, i))
    return pl.pallas_call(
        functools.partial(_select_kernel, kb=kb, nk=nkeys // kb, topk=topk),
        grid=(nq // LANES,),
        in_specs=[col, pl.BlockSpec((1, LANES), lambda i: (0, i)),
                  pl.BlockSpec(tri.shape, lambda i: (0, 0))],
        out_specs=col,
        out_shape=jax.ShapeDtypeStruct((nkeys, nq), F32),
        compiler_params=_cparams(1),
        name="sample_select",
    )(sc_t, qpos, tri)


def _samp_att_kernel(pt_ref, *refs, npg, ts):
    del pt_ref
    kp = refs[:npg]
    vp = refs[npg:2 * npg]
    q_ref, bias_ref, biasn_ref, kn_ref, vn_ref, ag_ref, y_ref, m_ref, l_ref, acc_ref = refs[2 * npg:]
    j = pl.program_id(1)
    nj = pl.num_programs(1)

    @pl.when(j == 0)
    def _init():
        m_ref[...] = jnp.full(m_ref.shape, NEG_INF, F32)
        l_ref[...] = jnp.zeros(l_ref.shape, F32)
        acc_ref[...] = jnp.zeros(acc_ref.shape, F32)

    q = q_ref[...]

    def update(kts, vts, bias):
        logit = jnp.concatenate(
            [jnp.dot(q, kt.astype(BF16), preferred_element_type=F32) for kt in kts], axis=1)
        logit = logit + jnp.concatenate([bias] * ATT_HEADS, axis=0)
        m_old = m_ref[...]
        m_new = jnp.maximum(m_old, jnp.max(logit, axis=1, keepdims=True))
        alpha = jnp.exp2(m_old - m_new)
        p = jnp.exp2(logit - m_new)
        l_ref[...] = alpha * l_ref[...] + jnp.sum(p, axis=1, keepdims=True)
        pb = p.astype(BF16)
        pvs = [lax.dot_general(pb[:, i * PAGE_SIZE:(i + 1) * PAGE_SIZE], vts[i].astype(BF16), _NT,
                               preferred_element_type=F32) for i in range(len(vts))]
        while len(pvs) > 1:
            pvs = [pvs[i] + pvs[i + 1] for i in range(0, len(pvs) - 1, 2)] + pvs[len(pvs) - len(pvs) % 2:]
        acc_ref[...] = alpha * acc_ref[...] + pvs[0]
        m_ref[...] = m_new

    update([r[...] for r in kp], [r[...] for r in vp], bias_ref[...])

    @pl.when(j == nj - 1)
    def _fin():
        update([kn_ref[...]], [vn_ref[...]], biasn_ref[...])
        o = acc_ref[...] / l_ref[...]
        gate = ag_ref[...]
        gsil = gate * jax.nn.sigmoid(gate)
        for h in range(ATT_HEADS):
            sl = slice(h * HEAD_DIM, (h + 1) * HEAD_DIM)
            y_ref[:, sl] = (o[h * ts:(h + 1) * ts, sl] * gsil[:, sl]).astype(BF16)


def _samp_attention(page_table, k_t, v_t, layer, qbd, bias, biasn, knew_t, vnew_t, ag, npg):
    bs, n_pages = page_table.shape
    ts = qbd.shape[1] // ATT_HEADS
    page_spec = lambda i: pl.BlockSpec(
        (None, None, D_ATT, PAGE_SIZE), lambda b, j, pt, i=i: (layer, pt[b, j * npg + i], 0, 0))
    per_b = lambda a: pl.BlockSpec((None,) + a.shape[1:], lambda b, j, pt: (b, 0, 0))
    gs = pltpu.PrefetchScalarGridSpec(
        num_scalar_prefetch=1,
        grid=(bs, n_pages // npg),
        in_specs=[page_spec(i) for i in range(npg)] * 2
        + [per_b(qbd),
           pl.BlockSpec((ts, npg * PAGE_SIZE), lambda b, j, pt: (b, j)),
           pl.BlockSpec((ts, PAGE_SIZE), lambda b, j, pt: (b, 0)),
           per_b(knew_t), per_b(vnew_t),
           pl.BlockSpec((ts, D_ATT), lambda b, j, pt: (b, 0))],
        out_specs=pl.BlockSpec((ts, D_ATT), lambda b, j, pt: (b, 0)),
        scratch_shapes=[pltpu.VMEM((ATT_HEADS * ts, 1), F32), pltpu.VMEM((ATT_HEADS * ts, 1), F32),
                        pltpu.VMEM((ATT_HEADS * ts, D_ATT), F32)],
    )
    return pl.pallas_call(
        functools.partial(_samp_att_kernel, npg=npg, ts=ts),
        grid_spec=gs,
        out_shape=jax.ShapeDtypeStruct((bs * ts, D_ATT), BF16),
        compiler_params=_cparams(2),
        name="sample_attention",
    )(page_table, *([k_t] * npg), *([v_t] * npg), qbd, bias, biasn, knew_t, vnew_t, ag)


def _dsa_sample(page_table, k_t, v_t, kidx_t, layer, ts, akt, avt, ikt, ag, aqb, iqb, iwt):
    bs, n_pages = page_table.shape
    past = n_pages * PAGE_SIZE
    n = bs * ts
    topk = min(TOPK_MAX, (past + ts) // 4)
    npg = math.gcd(n_pages, 32)

    def new_keys_t(a):
        a = a.reshape(a.shape[0], bs, ts).transpose(1, 0, 2)
        return jnp.pad(a, ((0, 0), (0, 0), (0, PAGE_SIZE - ts)))

    qi = iqb.reshape(bs, ts, IDX_HEADS, IDX_DIM).transpose(0, 2, 1, 3).reshape(bs, IDX_HEADS * ts, IDX_DIM)
    w = iwt[:IDX_HEADS].reshape(IDX_HEADS, bs, ts).transpose(1, 0, 2).reshape(bs, IDX_HEADS * ts, 1)
    w = jnp.broadcast_to(w, (bs, IDX_HEADS * ts, LANES))
    sc, scn = _samp_scores(page_table, kidx_t, layer, qi, w, new_keys_t(ikt), math.gcd(n_pages, 64))
    sc_t = jnp.concatenate([sc, scn], axis=1).T
    qpos = (past + jnp.arange(n, dtype=I32) % ts).reshape(1, n)
    bias_all = _select(sc_t, qpos, topk, PAGE_SIZE).T
    eye = jnp.eye(ATT_HEADS, dtype=BF16)
    q4 = aqb.reshape(bs, ts, ATT_HEADS, HEAD_DIM)
    qbd = (eye[None, :, None, :, None] * q4.transpose(0, 2, 1, 3)[:, :, :, None, :]).reshape(
        bs, ATT_HEADS * ts, D_ATT)
    return _samp_attention(page_table, k_t, v_t, layer, qbd, bias_all[:, :past], bias_all[:, past:],
                           new_keys_t(akt), new_keys_t(avt), ag, npg)


def _mixer_layer(x, pos, h0r, h0i, s0, attend, g, w_in_l, w_out_l, ssm_prm, ret_g, gf, final, tm, tl):
    bv, tv, _ = x.shape
    n = bv * tv
    wm = w_in_l[:, :C_IW].astype(BF16)
    wt = jnp.concatenate([w_in_l[:, C_AK:C_AG].T, w_in_l[:, C_IK:C_END].T,
                          jnp.zeros((SUBLANES - IDX_HEADS, D_MODEL), w_in_l.dtype)], axis=0).astype(BF16)
    xf = x.reshape(n, D_MODEL)
    pb, pt = (bv, tv) if tv % tm == 0 else (1, n)
    tm = min(tm, pt)
    (u2, r4, ag, akt, avt, ikt, aqb, akb, avtb, iqb, ikb, iwt) = _inproj(
        xf, pb, pt, g.astype(F32).reshape(1, D_MODEL), wm, wt, tm)
    y_ssm, hr, hi = _ssm(u2.reshape(bv, tv, -1), h0r, h0i, ssm_prm, tl)
    y_ret, s_new = _ret(r4.reshape(bv, tv, -1), s0, pos, ret_g)
    y_att = attend(ag, akt, avt, ikt, aqb, akb, avtb, iqb, ikb, iwt)
    y = _outproj(xf, y_ssm.reshape(n, D_SSM), y_ret.reshape(n, D_RET), y_att,
                 w_out_l.astype(BF16), gf.astype(F32).reshape(1, D_MODEL), min(2 * tm, n), final)
    tok = lambda a: a.reshape(pb, a.shape[1], -1, tv).transpose(0, 2, 3, 1).reshape(bv, tv, a.shape[1])
    return y.reshape(bv, tv, D_MODEL), (tok(akt), tok(avt), tok(ikt), hr, hi, s_new)


def kernel(x_prompt, x_sample, cache_k, cache_v, cache_kidx, state_ssm_re, state_ssm_im, state_ret,
           page_table, norm_g, w_in, w_out, ssm_lambda_re, ssm_lambda_im, ssm_b_re, ssm_b_im,
           ssm_c_re, ssm_c_im, ssm_d, ssm_log_step, ssm_w_glu, ssm_b_glu, ret_norm_g, final_norm_g):
    bp, tp, _ = x_prompt.shape
    bs, ts, _ = x_sample.shape
    depth = w_in.shape[0]
    past = page_table.shape[1] * PAGE_SIZE
    assert ts == SUBLANES and tp % Q_BLOCK == 0
    pos_p = jnp.arange(tp)
    pos_s = past + jnp.arange(ts)
    k_t = jnp.transpose(cache_k, (0, 1, 3, 4, 2)).reshape(cache_k.shape[:2] + (D_ATT, PAGE_SIZE))
    v_t = jnp.transpose(cache_v, (0, 1, 3, 4, 2)).reshape(cache_v.shape[:2] + (D_ATT, PAGE_SIZE))
    kidx_t = jnp.transpose(cache_kidx, (0, 1, 3, 2))
    zero_h = jnp.zeros((bp, 1, N_STATE), F32)
    zero_s = jnp.zeros((bp, RET_HEADS, HEAD_DIM, HEAD_DIM), F32)
    hp, hs = x_prompt, x_sample
    outs_p, outs_s = [], []
    for l in range(depth):
        final = l == depth - 1
        prm = _ssm_params(ssm_lambda_re[l], ssm_lambda_im[l], ssm_b_re[l], ssm_b_im[l], ssm_c_re[l],
                          ssm_c_im[l], ssm_d[l], ssm_log_step[l], ssm_w_glu[l], ssm_b_glu[l])

        def attend_p(ag, akt, avt, ikt, aqb, akb, avtb, iqb, ikb, iwt):
            return _dsa_prompt(bp, tp, ikb, akb, avtb, iqb, iwt, aqb, ag)

        hp, st_p = _mixer_layer(hp, pos_p, zero_h, zero_h, zero_s, attend_p, norm_g[l], w_in[l], w_out[l],
                                prm, ret_norm_g[l], final_norm_g, final, tm=512, tl=min(512, tp))

        def attend_s(ag, akt, avt, ikt, aqb, akb, avtb, iqb, ikb, iwt):
            return _dsa_sample(page_table, k_t, v_t, kidx_t, l, ts, akt[0], avt[0], ikt[0], ag,
                               aqb, iqb, iwt[0])

        hs, st_s = _mixer_layer(hs, pos_s, state_ssm_re[l].reshape(bs, 1, N_STATE),
                                state_ssm_im[l].reshape(bs, 1, N_STATE), state_ret[l], attend_s,
                                norm_g[l], w_in[l], w_out[l], prm, ret_norm_g[l], final_norm_g, final,
                                tm=256, tl=ts)
        outs_p.append(st_p)
        outs_s.append(st_s)

    def pack(outs, b, t):
        k = jnp.stack([o[0] for o in outs]).reshape(depth, b, t, ATT_HEADS, HEAD_DIM)
        v = jnp.stack([o[1] for o in outs]).reshape(depth, b, t, ATT_HEADS, HEAD_DIM)
        ki = jnp.stack([o[2] for o in outs])
        hr = jnp.stack([o[3] for o in outs]).reshape(depth, b, SSM_GROUPS, SSM_STATE)
        hi = jnp.stack([o[4] for o in outs]).reshape(depth, b, SSM_GROUPS, SSM_STATE)
        s = jnp.stack([o[5] for o in outs])
        return k, v, ki, hr, hi, s

    return (hp, hs) + pack(outs_p, bp, tp) + pack(outs_s, bs, ts)
```
